```python
import math
import jax
import jax.numpy as jnp
from jax import lax
import numpy as np

D_MODEL = 1024
BATCH = 2
SEQ = 16384
DEPTH = 4
DEC_BATCH = 16
DEC_SEQ = 2048
PAST_LEN = 128

GRID_W = 64
D_FF = 4 * D_MODEL
EPS = 1e-6
NEG_INF = -1e30
Q_BLOCK = 128

A_HEADS = 8
A_KV_HEADS = 2
A_HEAD_DIM = 64
ROPE_THETA = 10000.0
B_WIDTH = 512
B_BLOCKS = 8
B_BLOCK_DIM = B_WIDTH // B_BLOCKS
B_CONV = 4
RG_C = 8.0
C_HEADS = 4
C_HEAD_DIM = 128
C_WIDTH = C_HEADS * C_HEAD_DIM
C_CONV = 4
C_CHUNK = 64
D_GROUPS = ((128, 1), (512, 4), (2048, 16))
D_HEADS_PER_GROUP = 4
D_HEAD_DIM = 64
D_NHEADS = len(D_GROUPS) * D_HEADS_PER_GROUP
D_WIDTH = D_NHEADS * D_HEAD_DIM
N_BUCKETS = 32
MAX_DISTANCE = 1024

A_Q = A_HEADS * A_HEAD_DIM
A_KV = A_KV_HEADS * A_HEAD_DIM
EVEN_IN = A_Q + 2 * A_KV + 2 * B_WIDTH
EVEN_OUT = A_Q + B_WIDTH
ODD_IN = 4 * C_WIDTH + 4 * C_HEADS + 3 * D_WIDTH
ODD_OUT = C_WIDTH + D_HEADS_PER_GROUP * D_HEAD_DIM
N_EVEN = (DEPTH + 1) // 2
N_ODD = DEPTH // 2

kernel_name = 'hybrid_bidir_encoder'


def _split(x, sizes):
    return jnp.split(x, np.cumsum(sizes)[:-1].tolist(), axis=-1)


def rms_norm(x, g):
    xf = x.astype(jnp.float32)
    y = xf * lax.rsqrt(jnp.mean(xf * xf, axis=-1, keepdims=True) + EPS)
    return (y * g.astype(jnp.float32)).astype(x.dtype)


def l2_norm(x):
    return x * lax.rsqrt(jnp.sum(x * x, axis=-1, keepdims=True) + EPS)


def centred_dwconv(x, w, b=None):
    W = w.shape[0]
    left = W // 2
    S = x.shape[1]
    xp = jnp.pad(x, ((0, 0), (left, W - 1 - left), (0, 0)))
    y = xp[:, 0:S] * w[0]
    for j in range(1, W):
        y = y + xp[:, j:j + S] * w[j]
    if b is not None:
        y = y + b
    return y


def axial_rope_tables(S):
    rows = S // GRID_W
    row = jnp.repeat(jnp.arange(rows, dtype=jnp.float32), GRID_W)
    col = jnp.tile(jnp.arange(GRID_W, dtype=jnp.float32), rows)
    n_freq = A_HEAD_DIM // 4
    inv = ROPE_THETA ** (-jnp.arange(n_freq, dtype=jnp.float32) / n_freq)
    ang = jnp.stack([row[:, None] * inv, col[:, None] * inv], axis=1)
    return jnp.cos(ang), jnp.sin(ang)


def apply_axial_rope(x, cos, sin):
    Bn, S, H, dh = x.shape
    xr = x.astype(jnp.float32).reshape(Bn, S, H, 2, 2, dh // 4)
    a, b = xr[..., 0, :], xr[..., 1, :]
    c = cos[None, :, None]
    s = sin[None, :, None]
    out = jnp.stack([a * c - b * s, a * s + b * c], axis=-2)
    return out.reshape(Bn, S, H, dh)


def gqa_block_attention(q, k, v):
    Bn, S, _, dh = q.shape
    G = A_HEADS // A_KV_HEADS
    nb = S // Q_BLOCK
    qb = q.reshape(Bn, nb, Q_BLOCK, A_KV_HEADS, G, dh).transpose(1, 0, 2, 3, 4, 5)
    scale = dh ** -0.5

    def block(qi):
        s = jnp.einsum('bqhgd,bkhd->bhgqk', qi, k).astype(jnp.float32) * scale
        p = jax.nn.softmax(s, axis=-1)
        return jnp.einsum('bhgqk,bkhd->bqhgd', p.astype(v.dtype), v)

    o = lax.map(block, qb)
    return o.transpose(1, 0, 2, 3, 4, 5).reshape(Bn, S, A_HEADS * dh)


def _lin_combine(left, right):
    a1, b1 = left
    a2, b2 = right
    return a1 * a2, a2 * b1 + b2


def rglru_scan(x, wr, br, wi, bi, lam):
    Bn, S, W = x.shape
    xb = x.reshape(Bn, S, B_BLOCKS, B_BLOCK_DIM)
    r = jax.nn.sigmoid(jnp.einsum('bsnd,nde->bsne', xb, wr.astype(jnp.float32)).reshape(Bn, S, W) + br.astype(jnp.float32))
    i = jax.nn.sigmoid(jnp.einsum('bsnd,nde->bsne', xb, wi.astype(jnp.float32)).reshape(Bn, S, W) + bi.astype(jnp.float32))
    log_a = -RG_C * r * jax.nn.softplus(-lam.astype(jnp.float32))
    a = jnp.exp(log_a)
    u = jnp.sqrt(-jnp.expm1(2.0 * log_a)) * (i * x)
    _, h = lax.associative_scan(_lin_combine, (a, u), axis=1)
    return h


def _even_mixer(h, w_in, w_out, q_gain, k_gain, conv_w, conv_b, wr, br, wi, bi, lam):
    Bn, S, _ = h.shape
    dt = h.dtype
    proj = h @ w_in
    q, k, v, xr, gr = _split(proj, (A_Q, A_KV, A_KV, B_WIDTH, B_WIDTH))
    cos, sin = axial_rope_tables(S)
    q = rms_norm(q.reshape(Bn, S, A_HEADS, A_HEAD_DIM), q_gain)
    k = rms_norm(k.reshape(Bn, S, A_KV_HEADS, A_HEAD_DIM), k_gain)
    q = apply_axial_rope(q, cos, sin).astype(dt)
    k = apply_axial_rope(k, cos, sin).astype(dt)
    v = v.reshape(Bn, S, A_KV_HEADS, A_HEAD_DIM)
    y_a = gqa_block_attention(q, k, v)
    xc = centred_dwconv(xr.astype(jnp.float32), conv_w.astype(jnp.float32), conv_b.astype(jnp.float32))
    y_f = rglru_scan(xc, wr[0], br[0], wi[0], bi[0], lam[0])
    y_r = jnp.flip(rglru_scan(jnp.flip(xc, 1), wr[1], br[1], wi[1], bi[1], lam[1]), 1)
    y_b = (y_f + y_r) * jax.nn.gelu(gr.astype(jnp.float32))
    return jnp.concatenate([y_a, y_b.astype(dt)], axis=-1) @ w_out


def gated_delta_chunked(q, k, v, g, beta):
    Bn, S, H, dk = q.shape
    dv = v.shape[-1]
    C = C_CHUNK
    n = S // C

    def chunks(t):
        t = t.reshape(Bn, n, C, H, *t.shape[3:])
        return jnp.moveaxis(t, (1, 3), (0, 2))

    qc = chunks(q * dk ** -0.5)
    kc = chunks(k)
    vc = chunks(v)
    gc = jnp.cumsum(chunks(g), axis=-1)
    bc = chunks(beta)
    tril = jnp.tril(jnp.ones((C, C), dtype=bool))
    strict = jnp.tril(jnp.ones((C, C), dtype=bool), -1)
    decay = jnp.exp(jnp.where(tril, gc[..., :, None] - gc[..., None, :], -jnp.inf))
    kb = kc * bc[..., None]
    L = jnp.where(strict, jnp.einsum('nbhid,nbhjd->nbhij', kb, kc) * decay, 0.0)
    eye = jnp.eye(C, dtype=L.dtype)
    rhs = jnp.concatenate([vc * bc[..., None], kb * jnp.exp(gc)[..., None]], axis=-1)
    sol = lax.linalg.triangular_solve(L + eye, rhs, left_side=True, lower=True, unit_diagonal=True)
    u = sol[..., :dv]
    w = sol[..., dv:]
    intra = jnp.where(tril, jnp.einsum('nbhid,nbhjd->nbhij', qc, kc) * decay, 0.0)

    def step(state, xs):
        q_i, k_i, u_i, w_i, g_i, a_i = xs
        v_new = u_i - jnp.einsum('bhcd,bhde->bhce', w_i, state)
        o = jnp.einsum('bhcd,bhde->bhce', q_i * jnp.exp(g_i)[..., None], state) + jnp.einsum('bhij,bhje->bhie', a_i, v_new)
        g_last = g_i[..., -1:]
        state = state * jnp.exp(g_last)[..., None] + jnp.einsum('bhcd,bhce->bhde', k_i * jnp.exp(g_last - g_i)[..., None], v_new)
        return state, o

    state0 = jnp.zeros((Bn, H, dk, dv), q.dtype)
    _, o = lax.scan(step, state0, (qc, kc, u, w, gc, intra))
    return jnp.moveaxis(o, (0, 2), (1, 3)).reshape(Bn, S, H, dv)


def t5_bucket(rel):
    nb = N_BUCKETS // 2
    max_exact = nb // 2
    n = np.abs(rel)
    large = max_exact + (np.log(np.maximum(n, 1) / max_exact) / math.log(MAX_DISTANCE / max_exact) * (nb - max_exact)).astype(np.int64)
    large = np.minimum(large, nb - 1)
    return (np.where(rel > 0, nb, 0) + np.where(n < max_exact, n, large)).astype(np.int32)


def banded_dilated_attention(q, k, v, bias, dil, steps):
    Bn, S, H, dh = q.shape
    M = S // dil
    blk = steps
    nb = -(-M // blk)
    Mp = nb * blk
    Z = Bn * dil

    def to_sub(t):
        return t.reshape(Bn, M, dil, H, dh).transpose(0, 2, 1, 3, 4).reshape(Z, M, H, dh)

    def key_windows(t):
        tp = jnp.pad(to_sub(t), ((0, 0), (blk, Mp - M + blk), (0, 0), (0, 0))).reshape(Z, nb + 2, blk, H, dh)
        return jnp.concatenate([tp[:, :-2], tp[:, 1:-1], tp[:, 2:]], axis=2)

    qs = jnp.pad(to_sub(q), ((0, 0), (0, Mp - M), (0, 0), (0, 0))).reshape(Z, nb, blk, H, dh)
    kw = key_windows(k)
    vw = key_windows(v)
    delta = jnp.arange(3 * blk)[None, :] - blk - jnp.arange(blk)[:, None]
    in_band = jnp.abs(delta) <= steps
    kpos = jnp.arange(nb)[:, None] * blk + jnp.arange(3 * blk)[None, :] - blk
    valid = (kpos >= 0) & (kpos < M)
    mask = in_band[None] & valid[:, None, :]
    bias_m = bias.astype(jnp.float32)[:, jnp.clip(delta + steps, 0, 2 * steps)]
    s = jnp.einsum('znqhd,znkhd->znhqk', qs, kw).astype(jnp.float32) * dh ** -0.5 + bias_m[None, None]
    s = jnp.where(mask[None, :, None], s, NEG_INF)
    lse = jax.nn.logsumexp(s, axis=-1)
    p = jnp.exp(s - lse[..., None])
    o = jnp.einsum('znhqk,znkhd->znqhd', p.astype(v.dtype), vw).reshape(Z, Mp, H, dh)[:, :M]
    lse = lse.transpose(0, 1, 3, 2).reshape(Z, Mp, H)[:, :M]

    def from_sub(t):
        return t.reshape(Bn, dil, M, *t.shape[2:]).swapaxes(1, 2).reshape(Bn, S, *t.shape[2:])

    return from_sub(o), from_sub(lse)


def dilated_mixture_attention(q, k, v, rel_bias):
    Bn, S, _ = q.shape
    Hg = D_HEADS_PER_GROUP
    shp = (Bn, S, len(D_GROUPS), Hg, D_HEAD_DIM)
    q = q.reshape(shp)
    k = k.reshape(shp)
    v = v.reshape(shp)
    outs = []
    lses = []
    for gi, (window, dil) in enumerate(D_GROUPS):
        steps = window // (2 * dil)
        buckets = t5_bucket(np.arange(-steps, steps + 1) * dil)
        bias = rel_bias[jnp.asarray(buckets)][:, gi * Hg:(gi + 1) * Hg].T
        o, lse = banded_dilated_attention(q[:, :, gi], k[:, :, gi], v[:, :, gi], bias, dil, steps)
        outs.append(o.astype(jnp.float32))
        lses.append(lse)
    wts = jax.nn.softmax(jnp.stack(lses, axis=0), axis=0)
    o = jnp.sum(wts[..., None] * jnp.stack(outs, axis=0), axis=0)
    return o.reshape(Bn, S, Hg * D_HEAD_DIM)


def _odd_mixer(h, w_in, w_out, conv_w, a_log, dt_bias, o_gain, rel_bias):
    Bn, S, _ = h.shape
    dt = h.dtype
    proj = h @ w_in
    qkv, z, beta_l, alpha_l, dq, dk, dv = _split(proj, (3 * C_WIDTH, C_WIDTH, 2 * C_HEADS, 2 * C_HEADS, D_WIDTH, D_WIDTH, D_WIDTH))
    qkv = jax.nn.silu(centred_dwconv(qkv.astype(jnp.float32), conv_w.astype(jnp.float32)))
    cq, ck, cv = [t.reshape(Bn, S, C_HEADS, C_HEAD_DIM) for t in jnp.split(qkv, 3, axis=-1)]
    cq = l2_norm(cq)
    ck = l2_norm(ck)
    beta = jax.nn.sigmoid(beta_l.astype(jnp.float32)).reshape(Bn, S, 2, C_HEADS)
    g = -jnp.exp(a_log.astype(jnp.float32)) * jax.nn.softplus(alpha_l.astype(jnp.float32).reshape(Bn, S, 2, C_HEADS) + dt_bias.astype(jnp.float32))
    o_f = gated_delta_chunked(cq, ck, cv, g[:, :, 0], beta[:, :, 0])
    rev = lambda t: jnp.flip(t, 1)
    o_r = rev(gated_delta_chunked(rev(cq), rev(ck), rev(cv), rev(g[:, :, 1]), rev(beta[:, :, 1])))
    o_c = rms_norm(o_f + o_r, o_gain) * jax.nn.silu(z.astype(jnp.float32).reshape(Bn, S, C_HEADS, C_HEAD_DIM))
    y_c = o_c.reshape(Bn, S, C_WIDTH).astype(dt)
    y_d = dilated_mixture_attention(dq, dk, dv, rel_bias).astype(dt)
    return jnp.concatenate([y_c, y_d], axis=-1) @ w_out


def _sq_relu_mlp(h, w1, w2):
    return jnp.square(jax.nn.relu(h @ w1)) @ w2


def _trunk(x, p):
    for layer in range(DEPTH):
        h = rms_norm(x, p['norm_mix'][layer])
        j = layer // 2
        if layer % 2 == 0:
            y = _even_mixer(h, p['w_in_e'][j], p['w_out_e'][j], p['a_qnorm'][j], p['a_knorm'][j],
                            p['b_conv_w'][j], p['b_conv_b'][j], p['b_wr'][j], p['b_br'][j],
                            p['b_wi'][j], p['b_bi'][j], p['b_lambda'][j])
        else:
            y = _odd_mixer(h, p['w_in_o'][j], p['w_out_o'][j], p['c_conv_w'][j], p['c_a_log'][j],
                           p['c_dt_bias'][j], p['c_norm'][j], p['rel_bias'])
        x = x + y.astype(x.dtype)
        h = rms_norm(x, p['norm_ff'][layer])
        x = x + _sq_relu_mlp(h, p['w_ff1'][layer], p['w_ff2'][layer]).astype(x.dtype)
    return rms_norm(x, p['norm_final'])


def setup_inputs(seed: int = 0) -> dict:
    key = jax.random.key(seed)
    ks = jax.random.split(key, 25)
    f32 = jnp.float32

    def nrm(k, shape, scale):
        return jax.random.normal(k, shape, f32) * scale

    def gain(k, shape):
        return 1.0 + 0.02 * jax.random.normal(k, shape, f32)

    x_prompt = nrm(ks[0], (BATCH, SEQ, D_MODEL), 1.0)
    x_sample = nrm(ks[1], (DEC_BATCH, DEC_SEQ, D_MODEL), 1.0)
    rel_bias = nrm(ks[2], (N_BUCKETS, D_NHEADS), 0.5)
    norm_mix = gain(ks[3], (DEPTH, D_MODEL))
    norm_ff = gain(ks[4], (DEPTH, D_MODEL))
    norm_final = gain(ks[5], (D_MODEL,))
    w_ff1 = nrm(ks[6], (DEPTH, D_MODEL, D_FF), D_MODEL ** -0.5)
    w_ff2 = nrm(ks[7], (DEPTH, D_FF, D_MODEL), D_FF ** -0.5)
    w_in_e = nrm(ks[8], (N_EVEN, D_MODEL, EVEN_IN), D_MODEL ** -0.5)
    w_out_e = nrm(ks[9], (N_EVEN, EVEN_OUT, D_MODEL), EVEN_OUT ** -0.5)
    a_qnorm = gain(ks[10], (N_EVEN, A_HEAD_DIM))
    a_knorm = gain(ks[11], (N_EVEN, A_HEAD_DIM))
    b_conv_w = nrm(ks[12], (N_EVEN, B_CONV, B_WIDTH), B_CONV ** -0.5)
    b_conv_b = nrm(ks[13], (N_EVEN, B_WIDTH), 0.02)
    b_wr = nrm(ks[14], (N_EVEN, 2, B_BLOCKS, B_BLOCK_DIM, B_BLOCK_DIM), B_BLOCK_DIM ** -0.5)
    b_br = nrm(ks[15], (N_EVEN, 2, B_WIDTH), 0.1)
    b_wi = nrm(ks[16], (N_EVEN, 2, B_BLOCKS, B_BLOCK_DIM, B_BLOCK_DIM), B_BLOCK_DIM ** -0.5)
    b_bi = nrm(ks[17], (N_EVEN, 2, B_WIDTH), 0.1)
    a0 = jax.random.uniform(ks[18], (N_EVEN, 2, B_WIDTH), f32, 0.9, 0.999)
    s0 = a0 ** (1.0 / RG_C)
    b_lambda = jnp.log(s0) - jnp.log1p(-s0)
    w_in_o = nrm(ks[19], (N_ODD, D_MODEL, ODD_IN), D_MODEL ** -0.5)
    w_out_o = nrm(ks[20], (N_ODD, ODD_OUT, D_MODEL), ODD_OUT ** -0.5)
    c_conv_w = nrm(ks[21], (N_ODD, C_CONV, 3 * C_WIDTH), C_CONV ** -0.5)
    c_a_log = jnp.log(jax.random.uniform(ks[22], (N_ODD, 2, C_HEADS), f32, 1.0, 16.0))
    dtv = jnp.exp(jax.random.uniform(ks[23], (N_ODD, 2, C_HEADS), f32, math.log(1e-3), math.log(1e-1)))
    c_dt_bias = dtv + jnp.log(-jnp.expm1(-dtv))
    c_norm = gain(ks[24], (N_ODD, C_HEAD_DIM))
    return {'x_prompt': x_prompt, 'x_sample': x_sample, 'rel_bias': rel_bias,
            'norm_mix': norm_mix, 'norm_ff': norm_ff, 'norm_final': norm_final,
            'w_ff1': w_ff1, 'w_ff2': w_ff2,
            'w_in_e': w_in_e, 'w_out_e': w_out_e, 'a_qnorm': a_qnorm, 'a_knorm': a_knorm,
            'b_conv_w': b_conv_w, 'b_conv_b': b_conv_b, 'b_wr': b_wr, 'b_br': b_br,
            'b_wi': b_wi, 'b_bi': b_bi, 'b_lambda': b_lambda,
            'w_in_o': w_in_o, 'w_out_o': w_out_o, 'c_conv_w': c_conv_w, 'c_a_log': c_a_log,
            'c_dt_bias': c_dt_bias, 'c_norm': c_norm}


def reference(x_prompt, x_sample, rel_bias, norm_mix, norm_ff, norm_final, w_ff1, w_ff2,
              w_in_e, w_out_e, a_qnorm, a_knorm, b_conv_w, b_conv_b, b_wr, b_br, b_wi, b_bi, b_lambda,
              w_in_o, w_out_o, c_conv_w, c_a_log, c_dt_bias, c_norm):
    params = dict(rel_bias=rel_bias, norm_mix=norm_mix, norm_ff=norm_ff, norm_final=norm_final,
                  w_ff1=w_ff1, w_ff2=w_ff2, w_in_e=w_in_e, w_out_e=w_out_e,
                  a_qnorm=a_qnorm, a_knorm=a_knorm, b_conv_w=b_conv_w, b_conv_b=b_conv_b,
                  b_wr=b_wr, b_br=b_br, b_wi=b_wi, b_bi=b_bi, b_lambda=b_lambda,
                  w_in_o=w_in_o, w_out_o=w_out_o, c_conv_w=c_conv_w, c_a_log=c_a_log,
                  c_dt_bias=c_dt_bias, c_norm=c_norm)
    y_prompt = _trunk(x_prompt, params)
    y_sample = _trunk(x_sample, params)
    return (y_prompt, y_sample)
```

```python
import functools
import math

import numpy as np
import jax
import jax.numpy as jnp
from jax import lax
from jax.experimental import pallas as pl
from jax.experimental.pallas import tpu as pltpu

F32 = jnp.float32
BF16 = jnp.bfloat16

D_MODEL = 1024
D_FF = 4 * D_MODEL
DEPTH = 4
EPS = 1e-6
NEG_INF = -1e30
GRID_W = 64
LANES = 128

A_HEADS = 8
A_KV_HEADS = 2
A_HEAD_DIM = 64
A_Q = A_HEADS * A_HEAD_DIM
A_KV = A_KV_HEADS * A_HEAD_DIM
ROPE_THETA = 10000.0
B_WIDTH = 512
B_BLOCKS = 8
B_BLOCK_DIM = B_WIDTH // B_BLOCKS
RG_C = 8.0
C_HEADS = 4
C_HEAD_DIM = 128
C_WIDTH = C_HEADS * C_HEAD_DIM
C_CHUNK = 64
D_GROUPS = ((128, 1), (512, 4), (2048, 16))
D_HEADS_PER_GROUP = 4
D_HEAD_DIM = 64
D_NHEADS = len(D_GROUPS) * D_HEADS_PER_GROUP
D_WIDTH = D_NHEADS * D_HEAD_DIM
D_GW = D_HEADS_PER_GROUP * D_HEAD_DIM
D_STEPS = 64
N_BUCKETS = 32
MAX_DISTANCE = 1024
HALO = 8

VMEM_LIMIT = 56 * 1024 * 1024


def _cparams(*sem):
    return pltpu.CompilerParams(dimension_semantics=sem, vmem_limit_bytes=VMEM_LIMIT)


def _rms(x, gain):
    return x * lax.rsqrt(jnp.mean(x * x, axis=-1, keepdims=True) + EPS) * gain


def _softplus(x):
    return jnp.maximum(x, 0.0) + jnp.log1p(jnp.exp(-jnp.abs(x)))


def _sigmoid(x):
    return 1.0 / (1.0 + jnp.exp(-x))


def _silu(x):
    return x * _sigmoid(x)


def _gelu_tanh(x):
    return 0.5 * x * (1.0 + jnp.tanh(math.sqrt(2.0 / math.pi) * (x + 0.044715 * (x * x * x))))


def _conv4(prev, cur, nxt, w, first, last):
    rows = cur.shape[0]
    prev = jnp.where(first, 0.0, prev)
    nxt = jnp.where(last, 0.0, nxt)
    full = jnp.concatenate([prev, cur, nxt], axis=0)
    y = full[HALO - 2:HALO - 2 + rows] * w[0:1]
    for j in range(1, 4):
        y = y + full[HALO - 2 + j:HALO - 2 + j + rows] * w[j:j + 1]
    return y


def _even_proj_kernel(x_ref, g_ref, w_ref, qg_ref, kg_ref, cos_ref, sin_ref,
                      q_ref, k_ref, v_ref, xr_ref, gr_ref):
    xn = _rms(x_ref[...], g_ref[...]).astype(BF16)
    proj = jnp.dot(xn, w_ref[...], preferred_element_type=F32)
    cos = cos_ref[...]
    sin = sin_ref[...]
    lane = lax.broadcasted_iota(jnp.int32, cos.shape, 1)
    first_half = (lane % 32) < 16
    lo = lane < A_HEAD_DIM

    def rope(t):
        swapped = jnp.where(first_half, pltpu.roll(t, LANES - 16, 1), pltpu.roll(t, 16, 1))
        return t * cos + swapped * sin

    for h in range(A_HEADS):
        t = proj[:, h * LANES:(h + 1) * LANES]
        r = lax.rsqrt(jnp.sum(t * t, axis=-1, keepdims=True) * (1.0 / A_HEAD_DIM) + EPS)
        t = rope(t * r * qg_ref[:, h * LANES:(h + 1) * LANES])
        q_ref[:, h * LANES:(h + 1) * LANES] = (t * (A_HEAD_DIM ** -0.5)).astype(BF16)
    off = A_HEADS * LANES
    t = proj[:, off:off + LANES]
    t2 = t * t
    s_lo = jnp.sum(jnp.where(lo, t2, 0.0), axis=-1, keepdims=True)
    s_hi = jnp.sum(jnp.where(lo, 0.0, t2), axis=-1, keepdims=True)
    r = lax.rsqrt(jnp.where(lo, s_lo, s_hi) * (1.0 / A_HEAD_DIM) + EPS)
    k_ref[...] = rope(t * r * kg_ref[...]).astype(BF16)
    off += LANES
    v_ref[...] = proj[:, off:off + LANES].astype(BF16)
    off += LANES
    xr_ref[...] = proj[:, off:off + B_WIDTH]
    off += B_WIDTH
    gr_ref[...] = proj[:, off:off + B_WIDTH]


def _even_proj(x, gain, w, qg, kg, cos, sin, S, tm=512):
    M = x.shape[0]
    nS = S // tm
    n_in = w.shape[1]
    row = lambda i: (i, 0)
    const = lambda i: (0, 0)
    return pl.pallas_call(
        _even_proj_kernel,
        grid=(M // tm,),
        in_specs=[pl.BlockSpec((tm, D_MODEL), row), pl.BlockSpec((1, D_MODEL), const),
                  pl.BlockSpec((D_MODEL, n_in), const),
                  pl.BlockSpec((1, A_HEADS * LANES), const), pl.BlockSpec((1, LANES), const),
                  pl.BlockSpec((tm, LANES), lambda i: (i % nS, 0)),
                  pl.BlockSpec((tm, LANES), lambda i: (i % nS, 0))],
        out_specs=[pl.BlockSpec((tm, A_HEADS * LANES), row), pl.BlockSpec((tm, LANES), row),
                   pl.BlockSpec((tm, LANES), row), pl.BlockSpec((tm, B_WIDTH), row),
                   pl.BlockSpec((tm, B_WIDTH), row)],
        out_shape=[jax.ShapeDtypeStruct((M, A_HEADS * LANES), BF16),
                   jax.ShapeDtypeStruct((M, LANES), BF16), jax.ShapeDtypeStruct((M, LANES), BF16),
                   jax.ShapeDtypeStruct((M, B_WIDTH), F32), jax.ShapeDtypeStruct((M, B_WIDTH), F32)],
        compiler_params=_cparams("parallel"),
        name="even_proj",
    )(x, gain, w, qg, kg, cos, sin)


def _gqa_kernel(q_ref, k_ref, v_ref, o_ref, qs_ref, *, tq, tk, nk):
    for h in range(A_HEADS):
        qs_ref[h * tq:(h + 1) * tq, :] = q_ref[:, h * LANES:(h + 1) * LANES]
    qs = qs_ref[...]
    rows = A_HEADS * tq

    def body(j, carry):
        m, l, acc = carry
        start = pl.multiple_of(j * tk, tk)
        kj = k_ref[pl.ds(start, tk), :]
        vj = v_ref[pl.ds(start, tk), :]
        s = lax.dot_general(qs, kj, (((1,), (1,)), ((), ())), preferred_element_type=F32)
        m_new = jnp.maximum(m, jnp.max(s, axis=-1, keepdims=True))
        alpha = jnp.exp(m - m_new)
        p = jnp.exp(s - m_new)
        l = alpha * l + jnp.sum(p, axis=-1, keepdims=True)
        acc = alpha * acc + jnp.dot(p.astype(BF16), vj, preferred_element_type=F32)
        return m_new, l, acc

    init = (jnp.full((rows, 1), NEG_INF, F32), jnp.zeros((rows, 1), F32), jnp.zeros((rows, LANES), F32))
    m, l, acc = lax.fori_loop(0, nk, body, init)
    out = acc / l
    for pair in range(A_HEADS // 2):
        g = (2 * pair) // (A_HEADS // A_KV_HEADS)
        a = out[(2 * pair) * tq:(2 * pair + 1) * tq, g * A_HEAD_DIM:(g + 1) * A_HEAD_DIM]
        b = out[(2 * pair + 1) * tq:(2 * pair + 2) * tq, g * A_HEAD_DIM:(g + 1) * A_HEAD_DIM]
        o_ref[:, pair * LANES:(pair + 1) * LANES] = jnp.concatenate([a, b], axis=1).astype(o_ref.dtype)


def _gqa(q, k, v, B, S, tq=128, tk=512):
    M = q.shape[0]
    nq = S // tq
    kern = functools.partial(_gqa_kernel, tq=tq, tk=tk, nk=S // tk)
    return pl.pallas_call(
        kern,
        grid=(B, nq),
        in_specs=[pl.BlockSpec((tq, A_HEADS * LANES), lambda b, i: (b * nq + i, 0)),
                  pl.BlockSpec((S, LANES), lambda b, i: (b, 0)),
                  pl.BlockSpec((S, LANES), lambda b, i: (b, 0))],
        out_specs=pl.BlockSpec((tq, A_Q), lambda b, i: (b * nq + i, 0)),
        out_shape=jax.ShapeDtypeStruct((M, A_Q), BF16),
        scratch_shapes=[pltpu.VMEM((A_HEADS * tq, LANES), BF16)],
        compiler_params=_cparams("parallel", "parallel"),
        name="gqa",
    )(q, k, v)


def _rglru_kernel(xp_ref, xc_ref, xn_ref, yp_ref, yc_ref, yn_ref, cw_ref, cb_ref, wg_ref, bg_ref, lam_ref,
                  hf_ref, hr_ref, a_s, u_s, carry_s, *, tm, nblk):
    t = pl.program_id(1)

    @pl.when(t == 0)
    def _():
        carry_s[...] = jnp.zeros_like(carry_s)

    row8 = lax.broadcasted_iota(jnp.int32, (tm, B_WIDTH), 0) % 8
    cw = cw_ref[...]
    for d, (p_ref, c_ref, n_ref, out_ref) in enumerate(((xp_ref, xc_ref, xn_ref, hf_ref),
                                                       (yp_ref, yc_ref, yn_ref, hr_ref))):
        tt = t if d == 0 else nblk - 1 - t
        xc = _conv4(p_ref[...], c_ref[...], n_ref[...], cw, tt == 0, tt == nblk - 1) + cb_ref[...]
        xb = xc.astype(BF16)
        r = _sigmoid(jnp.dot(xb, wg_ref[2 * d], preferred_element_type=F32) + bg_ref[2 * d:2 * d + 1])
        i = _sigmoid(jnp.dot(xb, wg_ref[2 * d + 1], preferred_element_type=F32) + bg_ref[2 * d + 1:2 * d + 2])
        log_a = (-RG_C) * r * _softplus(-lam_ref[d:d + 1])
        a = jnp.exp(log_a)
        u = jnp.sqrt(1.0 - jnp.exp(2.0 * log_a)) * (i * xc)
        for sh in (1, 2, 4):
            if d == 0:
                keep = row8 >= sh
                a_sh = pltpu.roll(a, sh, 0)
                u_sh = pltpu.roll(u, sh, 0)
            else:
                keep = row8 < 8 - sh
                a_sh = pltpu.roll(a, tm - sh, 0)
                u_sh = pltpu.roll(u, tm - sh, 0)
            u = u + a * jnp.where(keep, u_sh, 0.0)
            a = a * jnp.where(keep, a_sh, 1.0)
        a_s[...] = a
        u_s[...] = u
        ngrp = tm // 8

        def body(gidx, h):
            gg = gidx if d == 0 else ngrp - 1 - gidx
            start = pl.multiple_of(gg * 8, 8)
            hg = u_s[pl.ds(start, 8), :] + a_s[pl.ds(start, 8), :] * h
            out_ref[pl.ds(start, 8), :] = hg
            edge = hg[7:8, :] if d == 0 else hg[0:1, :]
            return jnp.broadcast_to(edge, (8, B_WIDTH))

        carry_s[d] = lax.fori_loop(0, ngrp, body, carry_s[d], unroll=4)


def _rglru(xr, cw, cb, wg, bg, lam, B, S, tm=256):
    M = xr.shape[0]
    nblk = S // tm
    r = tm // HALO
    nrb = M // HALO
    fwd = lambda b, t: b * nblk + t
    rev = lambda b, t: b * nblk + (nblk - 1 - t)
    specs = []
    for blk in (fwd, rev):
        specs += [pl.BlockSpec((HALO, B_WIDTH), lambda b, t, blk=blk: (jnp.maximum(blk(b, t) * r - 1, 0), 0)),
                  pl.BlockSpec((tm, B_WIDTH), lambda b, t, blk=blk: (blk(b, t), 0)),
                  pl.BlockSpec((HALO, B_WIDTH), lambda b, t, blk=blk: (jnp.minimum((blk(b, t) + 1) * r, nrb - 1), 0))]
    const2 = lambda b, t: (0, 0)
    const3 = lambda b, t: (0, 0, 0)
    kern = functools.partial(_rglru_kernel, tm=tm, nblk=nblk)
    return pl.pallas_call(
        kern,
        grid=(B, nblk),
        in_specs=specs + [pl.BlockSpec((4, B_WIDTH), const2), pl.BlockSpec((1, B_WIDTH), const2),
                          pl.BlockSpec((4, B_WIDTH, B_WIDTH), const3), pl.BlockSpec((4, B_WIDTH), const2),
                          pl.BlockSpec((2, B_WIDTH), const2)],
        out_specs=[pl.BlockSpec((tm, B_WIDTH), lambda b, t: (fwd(b, t), 0)),
                   pl.BlockSpec((tm, B_WIDTH), lambda b, t: (rev(b, t), 0))],
        out_shape=[jax.ShapeDtypeStruct((M, B_WIDTH), F32), jax.ShapeDtypeStruct((M, B_WIDTH), F32)],
        scratch_shapes=[pltpu.VMEM((tm, B_WIDTH), F32), pltpu.VMEM((tm, B_WIDTH), F32),
                        pltpu.VMEM((2, 8, B_WIDTH), F32)],
        compiler_params=_cparams("arbitrary", "arbitrary"),
        name="rglru",
    )(xr, xr, xr, xr, xr, xr, cw, cb, wg, bg, lam)


def _even_out_kernel(x_ref, ya_ref, hf_ref, hr_ref, gr_ref, wa_ref, wb_ref, o_ref):
    yb = ((hf_ref[...] + hr_ref[...]) * _gelu_tanh(gr_ref[...])).astype(BF16)
    y = jnp.dot(ya_ref[...], wa_ref[...], preferred_element_type=F32)
    y = y + jnp.dot(yb, wb_ref[...], preferred_element_type=F32)
    o_ref[...] = x_ref[...] + y


def _even_out(x, ya, hf, hr, gr, wa, wb, tm=512):
    M = x.shape[0]
    row = lambda i: (i, 0)
    const = lambda i: (0, 0)
    return pl.pallas_call(
        _even_out_kernel,
        grid=(M // tm,),
        in_specs=[pl.BlockSpec((tm, D_MODEL), row), pl.BlockSpec((tm, A_Q), row),
                  pl.BlockSpec((tm, B_WIDTH), row), pl.BlockSpec((tm, B_WIDTH), row),
                  pl.BlockSpec((tm, B_WIDTH), row),
                  pl.BlockSpec((A_Q, D_MODEL), const), pl.BlockSpec((B_WIDTH, D_MODEL), const)],
        out_specs=pl.BlockSpec((tm, D_MODEL), row),
        out_shape=jax.ShapeDtypeStruct((M, D_MODEL), F32),
        compiler_params=_cparams("parallel"),
        name="even_out",
    )(x, ya, hf, hr, gr, wa, wb)


def _mlp_kernel(x_ref, g_ref, w1_ref, w2_ref, o_ref, *, tf):
    x = x_ref[...]
    xn = _rms(x, g_ref[...]).astype(BF16)
    acc = x
    for c in range(D_FF // tf):
        h = jnp.dot(xn, w1_ref[:, c * tf:(c + 1) * tf], preferred_element_type=F32)
        h = jnp.square(jnp.maximum(h, 0.0)).astype(BF16)
        acc = acc + jnp.dot(h, w2_ref[c * tf:(c + 1) * tf, :], preferred_element_type=F32)
    o_ref[...] = acc


def _mlp(x, gain, w1, w2, tm=512, tf=1024):
    M = x.shape[0]
    row = lambda i: (i, 0)
    const = lambda i: (0, 0)
    return pl.pallas_call(
        functools.partial(_mlp_kernel, tf=tf),
        grid=(M // tm,),
        in_specs=[pl.BlockSpec((tm, D_MODEL), row), pl.BlockSpec((1, D_MODEL), const),
                  pl.BlockSpec((D_MODEL, D_FF), const), pl.BlockSpec((D_FF, D_MODEL), const)],
        out_specs=pl.BlockSpec((tm, D_MODEL), row),
        out_shape=jax.ShapeDtypeStruct((M, D_MODEL), F32),
        compiler_params=_cparams("parallel"),
        name="mlp",
    )(x, gain, w1, w2)


def _final_norm_kernel(x_ref, g_ref, o_ref):
    o_ref[...] = _rms(x_ref[...], g_ref[...])


def _final_norm(x, gain, tm=1024):
    M = x.shape[0]
    return pl.pallas_call(
        _final_norm_kernel,
        grid=(M // tm,),
        in_specs=[pl.BlockSpec((tm, D_MODEL), lambda i: (i, 0)), pl.BlockSpec((1, D_MODEL), lambda i: (0, 0))],
        out_specs=pl.BlockSpec((tm, D_MODEL), lambda i: (i, 0)),
        out_shape=jax.ShapeDtypeStruct((M, D_MODEL), F32),
        compiler_params=_cparams("parallel"),
        name="final_norm",
    )(x, gain)


def _rope_tables(S):
    rows = S // GRID_W
    row = jnp.repeat(jnp.arange(rows, dtype=F32), GRID_W)
    col = jnp.tile(jnp.arange(GRID_W, dtype=F32), rows)
    n_freq = A_HEAD_DIM // 4
    inv = ROPE_THETA ** (-jnp.arange(n_freq, dtype=F32) / n_freq)
    ang_r = row[:, None] * inv
    ang_c = col[:, None] * inv
    cos = jnp.concatenate([jnp.cos(ang_r)] * 2 + [jnp.cos(ang_c)] * 2, axis=1)
    sin = jnp.concatenate([-jnp.sin(ang_r), jnp.sin(ang_r), -jnp.sin(ang_c), jnp.sin(ang_c)], axis=1)
    return jnp.tile(cos, (1, 2)), jnp.tile(sin, (1, 2))


def _prep_even(w_in, w_out, qn, kn, conv_w, conv_b, wr, br, wi, bi, lam):
    G = A_HEADS // A_KV_HEADS
    wq = w_in[:, :A_Q].reshape(D_MODEL, A_HEADS, A_HEAD_DIM)
    zeros = jnp.zeros_like(wq)
    half = (jnp.arange(A_HEADS) // G)[None, :, None]
    wq_pad = jnp.concatenate([jnp.where(half == 0, wq, zeros), jnp.where(half == 1, wq, zeros)], axis=-1)
    w = jnp.concatenate([wq_pad.reshape(D_MODEL, A_HEADS * LANES), w_in[:, A_Q:]], axis=1).astype(BF16)
    qg = jnp.tile(qn.astype(F32), 2 * A_HEADS)[None]
    kg = jnp.tile(kn.astype(F32), 2)[None]

    def dense(blocks):
        eye = jnp.eye(B_BLOCKS, dtype=blocks.dtype)
        return jnp.einsum('nde,nm->ndme', blocks, eye).reshape(B_WIDTH, B_WIDTH)

    wg = jnp.stack([dense(wr[0]), dense(wi[0]), dense(wr[1]), dense(wi[1])]).astype(BF16)
    bg = jnp.stack([br[0], bi[0], br[1], bi[1]]).astype(F32)
    return dict(w=w, qg=qg, kg=kg, cw=conv_w.astype(F32), cb=conv_b.astype(F32)[None], wg=wg, bg=bg,
                lam=lam.astype(F32), wa=w_out[:A_Q].astype(BF16), wb=w_out[A_Q:].astype(BF16))


def _even_layer(x, gain, p, cos, sin, B, S):
    q, k, v, xr, gr = _even_proj(x, gain, p['w'], p['qg'], p['kg'], cos, sin, S)
    ya = _gqa(q, k, v, B, S)
    hf, hr = _rglru(xr, p['cw'], p['cb'], p['wg'], p['bg'], p['lam'], B, S)
    return _even_out(x, ya, hf, hr, gr, p['wa'], p['wb'])


def _odd_proj_kernel(x_ref, g_ref, w_ref, nega_ref, dtb_ref,
                     qkv_ref, z_ref, dq_ref, dk_ref, dv_ref, gb_ref):
    xn = _rms(x_ref[...], g_ref[...]).astype(BF16)
    proj = jnp.dot(xn, w_ref[...], preferred_element_type=F32)
    off = 3 * C_WIDTH
    qkv_ref[...] = proj[:, :off]
    z_ref[...] = proj[:, off:off + C_WIDTH]
    off += C_WIDTH
    dq_ref[...] = (proj[:, off:off + D_WIDTH] * (D_HEAD_DIM ** -0.5)).astype(BF16)
    off += D_WIDTH
    dk_ref[...] = proj[:, off:off + D_WIDTH].astype(BF16)
    off += D_WIDTH
    dv_ref[...] = proj[:, off:off + D_WIDTH].astype(BF16)
    off += D_WIDTH
    t = proj[:, off:off + LANES]
    lane = lax.broadcasted_iota(jnp.int32, t.shape, 1)
    gate = jnp.where(lane < 2 * C_HEADS, _sigmoid(t), nega_ref[...] * _softplus(t + dtb_ref[...]))
    gb_ref[...] = gate[:, :4 * C_HEADS]


def _odd_proj(x, gain, w, nega, dtb, tm=512):
    M = x.shape[0]
    row = lambda i: (i, 0)
    const = lambda i: (0, 0)
    widths = (3 * C_WIDTH, C_WIDTH, D_WIDTH, D_WIDTH, D_WIDTH, 4 * C_HEADS)
    dtypes = (F32, F32, BF16, BF16, BF16, F32)
    return pl.pallas_call(
        _odd_proj_kernel,
        grid=(M // tm,),
        in_specs=[pl.BlockSpec((tm, D_MODEL), row), pl.BlockSpec((1, D_MODEL), const),
                  pl.BlockSpec((D_MODEL, w.shape[1]), const),
                  pl.BlockSpec((1, LANES), const), pl.BlockSpec((1, LANES), const)],
        out_specs=[pl.BlockSpec((tm, n), row) for n in widths],
        out_shape=[jax.ShapeDtypeStruct((M, n), dt) for n, dt in zip(widths, dtypes)],
        compiler_params=_cparams("parallel"),
        name="odd_proj",
    )(x, gain, w, nega, dtb)


def _gdn_prep_kernel(p_ref, c_ref, n_ref, cw_ref, q_ref, k_ref, v_ref, *, nblk):
    t = pl.program_id(1)
    y = _silu(_conv4(p_ref[...], c_ref[...], n_ref[...], cw_ref[...], t == 0, t == nblk - 1))
    for h in range(C_HEADS):
        for part, ref, scale in ((0, q_ref, C_HEAD_DIM ** -0.5), (1, k_ref, 1.0)):
            a = y[:, part * C_WIDTH + h * LANES:part * C_WIDTH + (h + 1) * LANES]
            a = a * lax.rsqrt(jnp.sum(a * a, axis=-1, keepdims=True) + EPS)
            ref[:, h * LANES:(h + 1) * LANES] = a * scale
    v_ref[...] = y[:, 2 * C_WIDTH:]


def _gdn_prep(qkv, cw, B, S, tm=256):
    M = qkv.shape[0]
    nblk = S // tm
    r = tm // HALO
    nrb = M // HALO
    W = 3 * C_WIDTH
    blk = lambda b, t: b * nblk + t
    return pl.pallas_call(
        functools.partial(_gdn_prep_kernel, nblk=nblk),
        grid=(B, nblk),
        in_specs=[pl.BlockSpec((HALO, W), lambda b, t: (jnp.maximum(blk(b, t) * r - 1, 0), 0)),
                  pl.BlockSpec((tm, W), lambda b, t: (blk(b, t), 0)),
                  pl.BlockSpec((HALO, W), lambda b, t: (jnp.minimum((blk(b, t) + 1) * r, nrb - 1), 0)),
                  pl.BlockSpec((4, W), lambda b, t: (0, 0))],
        out_specs=[pl.BlockSpec((tm, C_WIDTH), lambda b, t: (blk(b, t), 0))] * 3,
        out_shape=[jax.ShapeDtypeStruct((M, C_WIDTH), F32)] * 3,
        compiler_params=_cparams("parallel", "parallel"),
        name="gdn_prep",
    )(qkv, qkv, qkv, cw)


def _split3(x):
    hi = x.astype(BF16)
    r = x - hi.astype(F32)
    mid = r.astype(BF16)
    lo = (r - mid.astype(F32)).astype(BF16)
    return hi, mid, lo


def _gdn_kernel(q_ref, k_ref, v_ref, gb_ref, o_ref, state, *, reverse, tm):
    C = C_CHUNK
    nc = tm // C
    t = pl.program_id(1)

    @pl.when(t == 0)
    def _():
        state[...] = jnp.zeros_like(state)

    ri = lax.broadcasted_iota(jnp.int32, (tm, tm), 0)
    ci = lax.broadcasted_iota(jnp.int32, (tm, tm), 1)
    same = (ri // C) == (ci // C)
    diff = (ci - ri) if reverse else (ri - ci)
    incl = same & (diff >= 0)
    strict = same & (diff > 0)
    tri = jnp.where(incl, 1.0, 0.0).astype(BF16)
    eye = jnp.where(ri == ci, 1.0, 0.0)
    hi_i, lo_i = (ci, ri) if reverse else (ri, ci)
    level_masks = []
    s = 1
    while s < C:
        level_masks.append((((hi_i // s) % 2) == 1) & ((lo_i // s) == (hi_i // s) - 1))
        s *= 2
    dcol = 4 if reverse else 0
    gb = gb_ref[...]
    for h in range(C_HEADS):
        sl = slice(h * LANES, (h + 1) * LANES)
        q = q_ref[:, sl]
        k = k_ref[:, sl]
        v = v_ref[:, sl]
        beta = jnp.broadcast_to(gb[:, dcol + h:dcol + h + 1], (tm, LANES))
        g = jnp.broadcast_to(gb[:, 2 * C_HEADS + dcol + h:2 * C_HEADS + dcol + h + 1], (tm, LANES))
        gc = sum(jnp.dot(tri, part, preferred_element_type=F32) for part in _split3(g))
        gct = jnp.broadcast_to(jnp.transpose(gc)[0:1, :], (tm, tm))
        gc2 =jnp.concatenate([gc] * (tm // LANES), axis=1)
        dec = jnp.exp(jnp.where(incl, gc2 - gct, NEG_INF))
        kbf = k.astype(BF16)
        kb = k * beta
        kk = lax.dot_general(kb.astype(BF16), kbf, (((1,), (1,)), ((), ())), preferred_element_type=F32)
        lmat = jnp.where(strict, kk * dec, 0.0)
        tinv = eye - jnp.where(level_masks[0], lmat, 0.0)
        for mask in level_masks[1:]:
            xb = tinv.astype(BF16)
            cx = jnp.dot(jnp.where(mask, lmat, 0.0).astype(BF16), xb, preferred_element_type=F32)
            tinv = tinv - jnp.dot(xb, cx.astype(BF16), preferred_element_type=F32)
        eg = jnp.exp(gc)
        rhs = jnp.concatenate([v * beta, kb * eg], axis=1).astype(BF16)
        sol = jnp.dot(tinv.astype(BF16), rhs, preferred_element_type=F32)
        u = sol[:, :LANES]
        w = sol[:, LANES:].astype(BF16)
        qk = lax.dot_general(q.astype(BF16), kbf, (((1,), (1,)), ((), ())), preferred_element_type=F32)
        amat = jnp.where(incl, qk * dec, 0.0).astype(BF16)
        qe = (q * eg).astype(BF16)
        st = state[h]
        vnew = [None] * nc
        o_st = [None] * nc
        for c in (range(nc - 1, -1, -1) if reverse else range(nc)):
            rs = slice(c * C, (c + 1) * C)
            stb = st.astype(BF16)
            vn = u[rs] - jnp.dot(w[rs], stb, preferred_element_type=F32)
            o_st[c] = jnp.dot(qe[rs], stb, preferred_element_type=F32)
            last = c * C if reverse else (c + 1) * C - 1
            gl = gc[last:last + 1, :]
            kd = (k[rs] * jnp.exp(gl - gc[rs])).astype(BF16)
            st = st * jnp.exp(gl) + lax.dot_general(kd, vn.astype(BF16), (((0,), (0,)), ((), ())),
                                                     preferred_element_type=F32)
            vnew[c] = vn
        state[h] = st
        vn_all = jnp.concatenate(vnew, axis=0).astype(BF16)
        o_ref[:, sl] = jnp.concatenate(o_st, axis=0) + jnp.dot(amat, vn_all, preferred_element_type=F32)


def _gdn(q, k, v, gb, B, S, reverse, tm=256):
    M = q.shape[0]
    nblk = S // tm
    if reverse:
        blk = lambda b, t: (b * nblk + (nblk - 1 - t), 0)
    else:
        blk = lambda b, t: (b * nblk + t, 0)
    return pl.pallas_call(
        functools.partial(_gdn_kernel, reverse=reverse, tm=tm),
        grid=(B, nblk),
        in_specs=[pl.BlockSpec((tm, C_WIDTH), blk)] * 3 + [pl.BlockSpec((tm, 4 * C_HEADS), blk)],
        out_specs=pl.BlockSpec((tm, C_WIDTH), blk),
        out_shape=jax.ShapeDtypeStruct((M, C_WIDTH), F32),
        scratch_shapes=[pltpu.VMEM((C_HEADS, C_HEAD_DIM, C_HEAD_DIM), F32)],
        compiler_params=_cparams("arbitrary", "arbitrary"),
        name="gdn_rev" if reverse else "gdn_fwd",
    )(q, k, v, gb)


def _band_kernel(q_ref, kp_ref, kc_ref, kn_ref, vp_ref, vc_ref, vn_ref, bias_ref, o_ref, lse_ref, *, tq, m_len):
    t = pl.program_id(1)
    q = q_ref[...]
    kwin = jnp.concatenate([kp_ref[...], kc_ref[...], kn_ref[...]], axis=0)
    vwin = jnp.concatenate([vp_ref[...], vc_ref[...], vn_ref[...]], axis=0)
    nkeys = tq + 2 * D_STEPS
    kpos = t * tq - D_STEPS + lax.broadcasted_iota(jnp.int32, (tq, nkeys), 1)
    valid = (kpos >= 0) & (kpos < m_len)
    lane = lax.broadcasted_iota(jnp.int32, (tq, D_GW), 1)
    o = jnp.zeros((tq, D_GW), F32)
    lse_full = jnp.zeros((tq, D_GW), F32)
    for h in range(D_HEADS_PER_GROUP):
        mine = (lane // D_HEAD_DIM) == h
        qh = jnp.where(mine, q, jnp.zeros_like(q))
        s = lax.dot_general(qh, kwin, (((1,), (1,)), ((), ())), preferred_element_type=F32) + bias_ref[h]
        s = jnp.where(valid, s, NEG_INF)
        m = jnp.max(s, axis=-1, keepdims=True)
        lse = m + jnp.log(jnp.sum(jnp.exp(s - m), axis=-1, keepdims=True))
        p = jnp.exp(s - lse).astype(BF16)
        pv = jnp.dot(p, vwin, preferred_element_type=F32)
        o = jnp.where(mine, pv, o)
        lse_full = jnp.where(mine, lse, lse_full)
    o_ref[...] = o
    lse_ref[...] = lse_full


def _band(q, k, v, bias, Z, m_len, tq=128):
    M = q.shape[0]
    nq = m_len // tq
    r = tq // D_STEPS
    nhb = M // D_STEPS
    cur = lambda z, t: (z * nq + t, 0)
    prev = lambda z, t: (jnp.maximum((z * nq + t) * r - 1, 0), 0)
    nxt = lambda z, t: (jnp.minimum((z * nq + t + 1) * r, nhb - 1), 0)
    halo = lambda f: pl.BlockSpec((D_STEPS, D_GW), f)
    full = pl.BlockSpec((tq, D_GW), cur)
    return pl.pallas_call(
        functools.partial(_band_kernel, tq=tq, m_len=m_len),
        grid=(Z, nq),
        in_specs=[full, halo(prev), full, halo(nxt), halo(prev), full, halo(nxt),
                  pl.BlockSpec((D_HEADS_PER_GROUP, tq, tq + 2 * D_STEPS), lambda z, t: (0, 0, 0))],
        out_specs=[full, full],
        out_shape=[jax.ShapeDtypeStruct((M, D_GW), F32)] * 2,
        compiler_params=_cparams("parallel", "parallel"),
        name="band_attn",
    )(q, k, k, k, v, v, v, bias)


def _odd_out_kernel(x_ref, of_ref, or_ref, z_ref, cg_ref, o0_ref, o1_ref, o2_ref, l0_ref, l1_ref, l2_ref,
                    wc_ref, wd_ref, out_ref):
    oc = of_ref[...] + or_ref[...]
    z = z_ref[...]
    parts = []
    for h in range(C_HEADS):
        sl = slice(h * LANES, (h + 1) * LANES)
        parts.append(_rms(oc[:, sl], cg_ref[...]) * _silu(z[:, sl]))
    yc = jnp.concatenate(parts, axis=1).astype(BF16)
    l0, l1, l2 = l0_ref[...], l1_ref[...], l2_ref[...]
    m = jnp.maximum(jnp.maximum(l0, l1), l2)
    e0, e1, e2 = jnp.exp(l0 - m), jnp.exp(l1 - m), jnp.exp(l2 - m)
    den = e0 + e1 + e2
    yd = ((e0 / den) * o0_ref[...] + (e1 / den) * o1_ref[...] + (e2 / den) * o2_ref[...]).astype(BF16)
    y = jnp.dot(yc, wc_ref[...], preferred_element_type=F32) + jnp.dot(yd, wd_ref[...], preferred_element_type=F32)
    out_ref[...] = x_ref[...] + y


def _odd_out(x, o_f, o_r, z, cg, os_, ls_, wc, wd, tm=512):
    M = x.shape[0]
    row = lambda i: (i, 0)
    const = lambda i: (0, 0)
    wide = pl.BlockSpec((tm, C_WIDTH), row)
    grp = pl.BlockSpec((tm, D_GW), row)
    return pl.pallas_call(
        _odd_out_kernel,
        grid=(M // tm,),
        in_specs=[pl.BlockSpec((tm, D_MODEL), row), wide, wide, wide, pl.BlockSpec((1, LANES), const),
                  grp, grp, grp, grp, grp, grp,
                  pl.BlockSpec((C_WIDTH, D_MODEL), const), pl.BlockSpec((D_GW, D_MODEL), const)],
        out_specs=pl.BlockSpec((tm, D_MODEL), row),
        out_shape=jax.ShapeDtypeStruct((M, D_MODEL), F32),
        compiler_params=_cparams("parallel"),
        name="odd_out",
    )(x, o_f, o_r, z, cg, *os_, *ls_, wc, wd)


def _t5_bucket(rel):
    nb = N_BUCKETS // 2
    max_exact = nb // 2
    n = np.abs(rel)
    large = max_exact + (np.log(np.maximum(n, 1) / max_exact) / math.log(MAX_DISTANCE / max_exact)
                         * (nb - max_exact)).astype(np.int64)
    large = np.minimum(large, nb - 1)
    return (np.where(rel > 0, nb, 0) + np.where(n < max_exact, n, large)).astype(np.int32)


def _band_bias(rel_bias, tq=128):
    i = np.arange(tq)[:, None]
    j = np.arange(tq + 2 * D_STEPS)[None, :]
    delta = j - D_STEPS - i
    in_band = np.abs(delta) <= D_STEPS
    idx = np.clip(delta + D_STEPS, 0, 2 * D_STEPS)
    tiles = []
    for gi, (window, dil) in enumerate(D_GROUPS):
        steps = window // (2 * dil)
        assert steps == D_STEPS
        buckets = _t5_bucket(np.arange(-steps, steps + 1) * dil)
        b = rel_bias.astype(F32)[jnp.asarray(buckets)][:, gi * D_HEADS_PER_GROUP:(gi + 1) * D_HEADS_PER_GROUP].T
        tiles.append(jnp.where(jnp.asarray(in_band)[None], b[:, jnp.asarray(idx)], NEG_INF))
    return jnp.stack(tiles)


def _prep_odd(w_in, w_out, conv_w, a_log, dt_bias, o_gain, rel_bias):
    n_c = 4 * C_WIDTH
    n_ba = 4 * C_HEADS
    w = jnp.concatenate([w_in[:, :n_c], w_in[:, n_c + n_ba:], w_in[:, n_c:n_c + n_ba],
                         jnp.zeros((D_MODEL, LANES - n_ba), w_in.dtype)], axis=1).astype(BF16)
    pad = jnp.zeros((LANES - n_ba,), F32)
    nega = jnp.concatenate([jnp.zeros((2 * C_HEADS,), F32), -jnp.exp(a_log.astype(F32)).reshape(-1), pad])[None]
    dtb = jnp.concatenate([jnp.zeros((2 * C_HEADS,), F32), dt_bias.astype(F32).reshape(-1), pad])[None]
    return dict(w=w, nega=nega, dtb=dtb, cw=conv_w.astype(F32), cg=o_gain.astype(F32)[None],
                bias=_band_bias(rel_bias), wc=w_out[:C_WIDTH].astype(BF16), wd=w_out[C_WIDTH:].astype(BF16))


def _odd_layer(x, gain, p, B, S):
    qkv, z, dq, dk, dv, gb = _odd_proj(x, gain, p['w'], p['nega'], p['dtb'])
    cq, ck, cv = _gdn_prep(qkv, p['cw'], B, S)
    o_f = _gdn(cq, ck, cv, gb, B, S, False)
    o_r = _gdn(cq, ck, cv, gb, B, S, True)
    outs, lses = [], []
    for gi, (_, dil) in enumerate(D_GROUPS):
        m_len = S // dil

        def to_sub(a):
            a = a[:, gi * D_GW:(gi + 1) * D_GW]
            if dil == 1:
                return a
            return a.reshape(B, m_len, dil, D_GW).transpose(0, 2, 1, 3).reshape(B * S, D_GW)

        def from_sub(a):
            if dil == 1:
                return a
            return a.reshape(B, dil, m_len, D_GW).transpose(0, 2, 1, 3).reshape(B * S, D_GW)

        o, lse = _band(to_sub(dq), to_sub(dk), to_sub(dv), p['bias'][gi], B * dil, m_len)
        outs.append(from_sub(o))
        lses.append(from_sub(lse))
    return _odd_out(x, o_f, o_r, z, p['cg'], outs, lses, p['wc'], p['wd'])


def _trunk(x3, evens, odds, norm_mix, norm_ff, norm_final, w1, w2):
    B, S, _ = x3.shape
    x = x3.reshape(B * S, D_MODEL)
    cos, sin = _rope_tables(S)
    for layer in range(DEPTH):
        gain = norm_mix[layer][None]
        if layer % 2 == 0:
            x = _even_layer(x, gain, evens[layer // 2], cos, sin, B, S)
        else:
            x = _odd_layer(x, gain, odds[layer // 2], B, S)
        x = _mlp(x, norm_ff[layer][None], w1[layer], w2[layer])
    return _final_norm(x, norm_final[None]).reshape(B, S, D_MODEL)


def kernel(x_prompt, x_sample, rel_bias, norm_mix, norm_ff, norm_final, w_ff1, w_ff2, w_in_e, w_out_e,
           a_qnorm, a_knorm, b_conv_w, b_conv_b, b_wr, b_br, b_wi, b_bi, b_lambda, w_in_o, w_out_o,
           c_conv_w, c_a_log, c_dt_bias, c_norm):
    evens = [_prep_even(w_in_e[j], w_out_e[j], a_qnorm[j], a_knorm[j], b_conv_w[j], b_conv_b[j],
                        b_wr[j], b_br[j], b_wi[j], b_bi[j], b_lambda[j]) for j in range((DEPTH + 1) // 2)]
    odds = [_prep_odd(w_in_o[j], w_out_o[j], c_conv_w[j], c_a_log[j], c_dt_bias[j], c_norm[j], rel_bias)
            for j in range(DEPTH // 2)]
    w1 = w_ff1.astype(BF16)
    w2 = w_ff2.astype(BF16)
    nm = norm_mix.astype(F32)
    nf = norm_ff.astype(F32)
    ng = norm_final.astype(F32)
    y_prompt = _trunk(x_prompt, evens, odds, nm, nf, ng, w1, w2)
    y_sample = _trunk(x_sample, evens, odds, nm, nf, ng, w1, w2)
    return (y_prompt, y_sample)
```

```python
import functools
import math

import numpy as np
import jax
import jax.numpy as jnp
from jax import lax
from jax.experimental import pallas as pl
from jax.experimental.pallas import tpu as pltpu

F32 = jnp.float32
BF16 = jnp.bfloat16

D_MODEL = 1024
D_FF = 4 * D_MODEL
DEPTH = 4
EPS = 1e-6
NEG_INF = -1e30
GRID_W = 64
LANES = 128

A_HEADS = 8
A_KV_HEADS = 2
A_HEAD_DIM = 64
A_Q = A_HEADS * A_HEAD_DIM
A_KV = A_KV_HEADS * A_HEAD_DIM
ROPE_THETA = 10000.0
B_WIDTH = 512
B_BLOCKS = 8
B_BLOCK_DIM = B_WIDTH // B_BLOCKS
RG_C = 8.0
C_HEADS = 4
C_HEAD_DIM = 128
C_WIDTH = C_HEADS * C_HEAD_DIM
C_CHUNK = 64
D_GROUPS = ((128, 1), (512, 4), (2048, 16))
D_HEADS_PER_GROUP = 4
D_HEAD_DIM = 64
D_NHEADS = len(D_GROUPS) * D_HEADS_PER_GROUP
D_WIDTH = D_NHEADS * D_HEAD_DIM
D_GW = D_HEADS_PER_GROUP * D_HEAD_DIM
D_STEPS = 64
N_BUCKETS = 32
MAX_DISTANCE = 1024
HALO = 8
Q_SCALE = A_HEAD_DIM ** -0.5 * math.log2(math.e)
VT_PAD = 16
VT_ROWS = A_KV_HEADS * (A_HEAD_DIM + VT_PAD)

VMEM_LIMIT = 56 * 1024 * 1024


def _cparams(*sem):
    return pltpu.CompilerParams(dimension_semantics=sem, vmem_limit_bytes=VMEM_LIMIT)


def _rms(x, gain):
    return x * lax.rsqrt(jnp.mean(x * x, axis=-1, keepdims=True) + EPS) * gain


def _softplus(x):
    return jnp.maximum(x, 0.0) + jnp.log1p(jnp.exp(-jnp.abs(x)))


def _sigmoid(x):
    return 1.0 / (1.0 + jnp.exp(-x))


def _silu(x):
    return x * _sigmoid(x)


def _gelu_tanh(x):
    return 0.5 * x * (1.0 + jnp.tanh(math.sqrt(2.0 / math.pi) * (x + 0.044715 * (x * x * x))))


def _conv4(prev, cur, nxt, w, first, last):
    rows = cur.shape[0]
    prev = jnp.where(first, 0.0, prev)
    nxt = jnp.where(last, 0.0, nxt)
    full = jnp.concatenate([prev, cur, nxt], axis=0)
    y = full[HALO - 2:HALO - 2 + rows] * w[0:1]
    for j in range(1, 4):
        y = y + full[HALO - 2 + j:HALO - 2 + j + rows] * w[j:j + 1]
    return y


def _even_proj_kernel(x_ref, g_ref, w_ref, wvt_ref, qg_ref, kg_ref, cos_ref, sin_ref,
                      q_ref, k_ref, vt_ref, xr_ref, gr_ref):
    xn = _rms(x_ref[...], g_ref[...]).astype(BF16)
    proj = jnp.dot(xn, w_ref[...], preferred_element_type=F32)
    vt = lax.dot_general(wvt_ref[...], xn, (((1,), (1,)), ((), ())), preferred_element_type=F32).astype(BF16)
    ones = jnp.ones((VT_PAD, LANES), BF16)
    for c in range(vt_ref.shape[0]):
        cols = slice(c * LANES, (c + 1) * LANES)
        vt_ref[c] = jnp.concatenate([vt[:A_HEAD_DIM, cols], ones, vt[A_HEAD_DIM:, cols], ones], axis=0)
    cos = cos_ref[...]
    sin = sin_ref[...]
    lane = lax.broadcasted_iota(jnp.int32, cos.shape, 1)
    first_half = (lane % 32) < 16
    lo = lane < A_HEAD_DIM

    def rope(t):
        swapped = jnp.where(first_half, pltpu.roll(t, LANES - 16, 1), pltpu.roll(t, 16, 1))
        return t * cos + swapped * sin

    for h in range(A_HEADS):
        t = proj[:, h * LANES:(h + 1) * LANES]
        r = lax.rsqrt(jnp.sum(t * t, axis=-1, keepdims=True) * (1.0 / A_HEAD_DIM) + EPS)
        t = rope(t * r * qg_ref[:, h * LANES:(h + 1) * LANES])
        q_ref[:, h * LANES:(h + 1) * LANES] = (t * Q_SCALE).astype(BF16)
    off = A_HEADS * LANES
    t = proj[:, off:off + LANES]
    t2 = t * t
    s_lo = jnp.sum(jnp.where(lo, t2, 0.0), axis=-1, keepdims=True)
    s_hi = jnp.sum(jnp.where(lo, 0.0, t2), axis=-1, keepdims=True)
    r = lax.rsqrt(jnp.where(lo, s_lo, s_hi) * (1.0 / A_HEAD_DIM) + EPS)
    k_ref[...] = rope(t * r * kg_ref[...]).astype(BF16)
    off += LANES
    xr_ref[...] = proj[:, off:off + B_WIDTH]
    off += B_WIDTH
    gr_ref[...] = proj[:, off:off + B_WIDTH]


def _even_proj(x, gain, w, wvt, qg, kg, cos, sin, S, tm=512):
    M = x.shape[0]
    nS = S // tm
    n_in = w.shape[1]
    row = lambda i: (i, 0)
    const = lambda i: (0, 0)
    return pl.pallas_call(
        _even_proj_kernel,
        grid=(M // tm,),
        in_specs=[pl.BlockSpec((tm, D_MODEL), row), pl.BlockSpec((1, D_MODEL), const),
                  pl.BlockSpec((D_MODEL, n_in), const), pl.BlockSpec((A_KV, D_MODEL), const),
                  pl.BlockSpec((1, A_HEADS * LANES), const), pl.BlockSpec((1, LANES), const),
                  pl.BlockSpec((tm, LANES), lambda i: (i % nS, 0)),
                  pl.BlockSpec((tm, LANES), lambda i: (i % nS, 0))],
        out_specs=[pl.BlockSpec((tm, A_HEADS * LANES), row), pl.BlockSpec((tm, LANES), row),
                   pl.BlockSpec((tm // LANES, VT_ROWS, LANES), lambda i: (i, 0, 0)),
                   pl.BlockSpec((tm, B_WIDTH), row), pl.BlockSpec((tm, B_WIDTH), row)],
        out_shape=[jax.ShapeDtypeStruct((M, A_HEADS * LANES), BF16),
                   jax.ShapeDtypeStruct((M, LANES), BF16),
                   jax.ShapeDtypeStruct((M // LANES, VT_ROWS, LANES), BF16),
                   jax.ShapeDtypeStruct((M, B_WIDTH), F32), jax.ShapeDtypeStruct((M, B_WIDTH), F32)],
        compiler_params=_cparams("parallel"),
        name="even_proj",
    )(x, gain, w, wvt, qg, kg, cos, sin)


def _gqa_kernel(q_ref, k_ref, vt_ref, o_ref, m_ref, acc_ref, *, tq, nslab, unroll, lookahead):
    ntile = A_HEADS // 2
    qt = [jnp.concatenate([q_ref[:, (2 * p) * LANES:(2 * p + 1) * LANES],
                           q_ref[:, (2 * p + 1) * LANES:(2 * p + 2) * LANES]], axis=0) for p in range(ntile)]
    m_ref[...] = jnp.full(m_ref.shape, NEG_INF, F32)
    acc_ref[...] = jnp.zeros(acc_ref.shape, F32)
    vrows = VT_ROWS // A_KV_HEADS

    def scores(j, u, p):
        ks = k_ref[pl.ds(pl.multiple_of((j * unroll + u) * LANES, LANES), LANES), :]
        return lax.dot_general(ks, qt[p], (((1,), (1,)), ((), ())), preferred_element_type=F32)

    def update(j, u, p, st):
        g = (2 * p) // (A_HEADS // A_KV_HEADS)
        m_old = m_ref[p]
        m_new = jnp.maximum(m_old, jnp.max(st, axis=0, keepdims=True))
        alpha = jnp.exp2(m_old - m_new)
        pt = jnp.exp2(st - m_new).astype(BF16)
        pv = jnp.dot(vt_ref[j * unroll + u, g * vrows:(g + 1) * vrows, :], pt, preferred_element_type=F32)
        acc_ref[p] = alpha * acc_ref[p] + pv
        m_ref[p] = m_new

    pairs = [(u, p) for u in range(unroll) for p in range(ntile)]
    niter = nslab // unroll

    def body(j, pending):
        pending = list(pending)
        j_next = jnp.minimum(j + 1, niter - 1)
        for idx in range(len(pairs)):
            ahead = idx + lookahead
            if ahead < len(pairs):
                pending.append(scores(j, *pairs[ahead]))
            else:
                pending.append(scores(j_next, *pairs[ahead - len(pairs)]))
            update(j, *pairs[idx], pending.pop(0))
        return tuple(pending)

    lax.fori_loop(0, niter, body, tuple(scores(0, *pairs[i]) for i in range(lookahead)))
    for p in range(ntile):
        acc = acc_ref[p]
        o = acc[:A_HEAD_DIM] / acc[A_HEAD_DIM:A_HEAD_DIM + 1]
        both = jnp.concatenate([o[:, :tq], o[:, tq:]], axis=0)
        o_ref[:, p * LANES:(p + 1) * LANES] = jnp.transpose(both).astype(o_ref.dtype)


def _gqa(q, k, vt, B, S, tq=128, unroll=8, lookahead=6):
    M = q.shape[0]
    nq = S // tq
    nslab = S // LANES
    kern = functools.partial(_gqa_kernel, tq=tq, nslab=nslab, unroll=unroll, lookahead=lookahead)
    return pl.pallas_call(
        kern,
        grid=(B, nq),
        in_specs=[pl.BlockSpec((tq, A_HEADS * LANES), lambda b, i: (b * nq + i, 0)),
                  pl.BlockSpec((S, LANES), lambda b, i: (b, 0)),
                  pl.BlockSpec((nslab, VT_ROWS, LANES), lambda b, i: (b, 0, 0))],
        out_specs=pl.BlockSpec((tq, A_Q), lambda b, i: (b * nq + i, 0)),
        out_shape=jax.ShapeDtypeStruct((M, A_Q), BF16),
        scratch_shapes=[pltpu.VMEM((A_HEADS // 2, 1, 2 * tq), F32),
                        pltpu.VMEM((A_HEADS // 2, VT_ROWS // A_KV_HEADS, 2 * tq), F32)],
        compiler_params=_cparams("parallel", "parallel"),
        name="gqa",
    )(q, k, vt)


def _rglru_kernel(xp_ref, xc_ref, xn_ref, yp_ref, yc_ref, yn_ref, cw_ref, cb_ref, wg_ref, bg_ref, lam_ref,
                  hf_ref, hr_ref, a_s, u_s, carry_s, *, tm, nblk):
    t = pl.program_id(1)

    @pl.when(t == 0)
    def _():
        carry_s[...] = jnp.zeros_like(carry_s)

    row8 = lax.broadcasted_iota(jnp.int32, (tm, B_WIDTH), 0) % 8
    cw = cw_ref[...]
    for d, (p_ref, c_ref, n_ref, out_ref) in enumerate(((xp_ref, xc_ref, xn_ref, hf_ref),
                                                       (yp_ref, yc_ref, yn_ref, hr_ref))):
        tt = t if d == 0 else nblk - 1 - t
        xc = _conv4(p_ref[...], c_ref[...], n_ref[...], cw, tt == 0, tt == nblk - 1) + cb_ref[...]
        xb = xc.astype(BF16)
        r = _sigmoid(jnp.dot(xb, wg_ref[2 * d], preferred_element_type=F32) + bg_ref[2 * d:2 * d + 1])
        i = _sigmoid(jnp.dot(xb, wg_ref[2 * d + 1], preferred_element_type=F32) + bg_ref[2 * d + 1:2 * d + 2])
        log_a = (-RG_C) * r * _softplus(-lam_ref[d:d + 1])
        a = jnp.exp(log_a)
        u = jnp.sqrt(1.0 - jnp.exp(2.0 * log_a)) * (i * xc)
        for sh in (1, 2, 4):
            if d == 0:
                keep = row8 >= sh
                a_sh = pltpu.roll(a, sh, 0)
                u_sh = pltpu.roll(u, sh, 0)
            else:
                keep = row8 < 8 - sh
                a_sh = pltpu.roll(a, tm - sh, 0)
                u_sh = pltpu.roll(u, tm - sh, 0)
            u = u + a * jnp.where(keep, u_sh, 0.0)
            a = a * jnp.where(keep, a_sh, 1.0)
        a_s[...] = a
        u_s[...] = u
        ngrp = tm // 8

        def body(gidx, h):
            gg = gidx if d == 0 else ngrp - 1 - gidx
            start = pl.multiple_of(gg * 8, 8)
            hg = u_s[pl.ds(start, 8), :] + a_s[pl.ds(start, 8), :] * h
            out_ref[pl.ds(start, 8), :] = hg
            edge = hg[7:8, :] if d == 0 else hg[0:1, :]
            return jnp.broadcast_to(edge, (8, B_WIDTH))

        carry_s[d] = lax.fori_loop(0, ngrp, body, carry_s[d], unroll=4)


def _rglru(xr, cw, cb, wg, bg, lam, B, S, tm=256):
    M = xr.shape[0]
    nblk = S // tm
    r = tm // HALO
    nrb = M // HALO
    fwd = lambda b, t: b * nblk + t
    rev = lambda b, t: b * nblk + (nblk - 1 - t)
    specs = []
    for blk in (fwd, rev):
        specs += [pl.BlockSpec((HALO, B_WIDTH), lambda b, t, blk=blk: (jnp.maximum(blk(b, t) * r - 1, 0), 0)),
                  pl.BlockSpec((tm, B_WIDTH), lambda b, t, blk=blk: (blk(b, t), 0)),
                  pl.BlockSpec((HALO, B_WIDTH), lambda b, t, blk=blk: (jnp.minimum((blk(b, t) + 1) * r, nrb - 1), 0))]
    const2 = lambda b, t: (0, 0)
    const3 = lambda b, t: (0, 0, 0)
    kern = functools.partial(_rglru_kernel, tm=tm, nblk=nblk)
    return pl.pallas_call(
        kern,
        grid=(B, nblk),
        in_specs=specs + [pl.BlockSpec((4, B_WIDTH), const2), pl.BlockSpec((1, B_WIDTH), const2),
                          pl.BlockSpec((4, B_WIDTH, B_WIDTH), const3), pl.BlockSpec((4, B_WIDTH), const2),
                          pl.BlockSpec((2, B_WIDTH), const2)],
        out_specs=[pl.BlockSpec((tm, B_WIDTH), lambda b, t: (fwd(b, t), 0)),
                   pl.BlockSpec((tm, B_WIDTH), lambda b, t: (rev(b, t), 0))],
        out_shape=[jax.ShapeDtypeStruct((M, B_WIDTH), F32), jax.ShapeDtypeStruct((M, B_WIDTH), F32)],
        scratch_shapes=[pltpu.VMEM((tm, B_WIDTH), F32), pltpu.VMEM((tm, B_WIDTH), F32),
                        pltpu.VMEM((2, 8, B_WIDTH), F32)],
        compiler_params=_cparams("arbitrary", "arbitrary"),
        name="rglru",
    )(xr, xr, xr, xr, xr, xr, cw, cb, wg, bg, lam)


def _even_out_kernel(x_ref, ya_ref, hf_ref, hr_ref, gr_ref, wa_ref, wb_ref, o_ref):
    yb = ((hf_ref[...] + hr_ref[...]) * _gelu_tanh(gr_ref[...])).astype(BF16)
    y = jnp.dot(ya_ref[...], wa_ref[...], preferred_element_type=F32)
    y = y + jnp.dot(yb, wb_ref[...], preferred_element_type=F32)
    o_ref[...] = x_ref[...] + y


def _even_out(x, ya, hf, hr, gr, wa, wb, tm=512):
    M = x.shape[0]
    row = lambda i: (i, 0)
    const = lambda i: (0, 0)
    return pl.pallas_call(
        _even_out_kernel,
        grid=(M // tm,),
        in_specs=[pl.BlockSpec((tm, D_MODEL), row), pl.BlockSpec((tm, A_Q), row),
                  pl.BlockSpec((tm, B_WIDTH), row), pl.BlockSpec((tm, B_WIDTH), row),
                  pl.BlockSpec((tm, B_WIDTH), row),
                  pl.BlockSpec((A_Q, D_MODEL), const), pl.BlockSpec((B_WIDTH, D_MODEL), const)],
        out_specs=pl.BlockSpec((tm, D_MODEL), row),
        out_shape=jax.ShapeDtypeStruct((M, D_MODEL), F32),
        compiler_params=_cparams("parallel"),
        name="even_out",
    )(x, ya, hf, hr, gr, wa, wb)


def _mlp_kernel(x_ref, g_ref, w1_ref, w2_ref, o_ref, *, tf):
    x = x_ref[...]
    xn = _rms(x, g_ref[...]).astype(BF16)
    acc = x
    for c in range(D_FF // tf):
        h = jnp.dot(xn, w1_ref[:, c * tf:(c + 1) * tf], preferred_element_type=F32)
        h = jnp.square(jnp.maximum(h, 0.0)).astype(BF16)
        acc = acc + jnp.dot(h, w2_ref[c * tf:(c + 1) * tf, :], preferred_element_type=F32)
    o_ref[...] = acc


def _mlp(x, gain, w1, w2, tm=512, tf=1024):
    M = x.shape[0]
    row = lambda i: (i, 0)
    const = lambda i: (0, 0)
    return pl.pallas_call(
        functools.partial(_mlp_kernel, tf=tf),
        grid=(M // tm,),
        in_specs=[pl.BlockSpec((tm, D_MODEL), row), pl.BlockSpec((1, D_MODEL), const),
                  pl.BlockSpec((D_MODEL, D_FF), const), pl.BlockSpec((D_FF, D_MODEL), const)],
        out_specs=pl.BlockSpec((tm, D_MODEL), row),
        out_shape=jax.ShapeDtypeStruct((M, D_MODEL), F32),
        compiler_params=_cparams("parallel"),
        name="mlp",
    )(x, gain, w1, w2)


def _final_norm_kernel(x_ref, g_ref, o_ref):
    o_ref[...] = _rms(x_ref[...], g_ref[...])


def _final_norm(x, gain, tm=1024):
    M = x.shape[0]
    return pl.pallas_call(
        _final_norm_kernel,
        grid=(M // tm,),
        in_specs=[pl.BlockSpec((tm, D_MODEL), lambda i: (i, 0)), pl.BlockSpec((1, D_MODEL), lambda i: (0, 0))],
        out_specs=pl.BlockSpec((tm, D_MODEL), lambda i: (i, 0)),
        out_shape=jax.ShapeDtypeStruct((M, D_MODEL), F32),
        compiler_params=_cparams("parallel"),
        name="final_norm",
    )(x, gain)


def _rope_tables(S):
    rows = S // GRID_W
    row = jnp.repeat(jnp.arange(rows, dtype=F32), GRID_W)
    col = jnp.tile(jnp.arange(GRID_W, dtype=F32), rows)
    n_freq = A_HEAD_DIM // 4
    inv = ROPE_THETA ** (-jnp.arange(n_freq, dtype=F32) / n_freq)
    ang_r = row[:, None] * inv
    ang_c = col[:, None] * inv
    cos = jnp.concatenate([jnp.cos(ang_r)] * 2 + [jnp.cos(ang_c)] * 2, axis=1)
    sin = jnp.concatenate([-jnp.sin(ang_r), jnp.sin(ang_r), -jnp.sin(ang_c), jnp.sin(ang_c)], axis=1)
    return jnp.tile(cos, (1, 2)), jnp.tile(sin, (1, 2))


def _prep_even(w_in, w_out, qn, kn, conv_w, conv_b, wr, br, wi, bi, lam):
    G = A_HEADS // A_KV_HEADS
    wq = w_in[:, :A_Q].reshape(D_MODEL, A_HEADS, A_HEAD_DIM)
    zeros = jnp.zeros_like(wq)
    half = (jnp.arange(A_HEADS) // G)[None, :, None]
    wq_pad = jnp.concatenate([jnp.where(half == 0, wq, zeros), jnp.where(half == 1, wq, zeros)], axis=-1)
    w = jnp.concatenate([wq_pad.reshape(D_MODEL, A_HEADS * LANES), w_in[:, A_Q:A_Q + A_KV],
                         w_in[:, A_Q + 2 * A_KV:]], axis=1).astype(BF16)
    wvt = w_in[:, A_Q + A_KV:A_Q + 2 * A_KV].T.astype(BF16)
    qg =jnp.tile(qn.astype(F32), 2 * A_HEADS)[None]
    kg = jnp.tile(kn.astype(F32), 2)[None]

    def dense(blocks):
        eye = jnp.eye(B_BLOCKS, dtype=blocks.dtype)
        return jnp.einsum('nde,nm->ndme', blocks, eye).reshape(B_WIDTH, B_WIDTH)

    wg = jnp.stack([dense(wr[0]), dense(wi[0]), dense(wr[1]), dense(wi[1])]).astype(BF16)
    bg = jnp.stack([br[0], bi[0], br[1], bi[1]]).astype(F32)
    return dict(w=w, wvt=wvt, qg=qg, kg=kg, cw=conv_w.astype(F32), cb=conv_b.astype(F32)[None], wg=wg, bg=bg,
                lam=lam.astype(F32), wa=w_out[:A_Q].astype(BF16), wb=w_out[A_Q:].astype(BF16))


def _even_layer(x, gain, p, cos, sin, B, S):
    q, k, vt, xr, gr = _even_proj(x, gain, p['w'], p['wvt'], p['qg'], p['kg'], cos, sin, S)
    ya = _gqa(q, k, vt, B, S)
    hf, hr = _rglru(xr, p['cw'], p['cb'], p['wg'], p['bg'], p['lam'], B, S)
    return _even_out(x, ya, hf, hr, gr, p['wa'], p['wb'])


def _odd_proj_kernel(x_ref, g_ref, w_ref, nega_ref, dtb_ref,
                     qkv_ref, z_ref, dq_ref, dk_ref, dv_ref, gb_ref):
    xn = _rms(x_ref[...], g_ref[...]).astype(BF16)
    proj = jnp.dot(xn, w_ref[...], preferred_element_type=F32)
    off = 3 * C_WIDTH
    qkv_ref[...] = proj[:, :off]
    z_ref[...] = proj[:, off:off + C_WIDTH]
    off += C_WIDTH
    dq_ref[...] = (proj[:, off:off + D_WIDTH] * (D_HEAD_DIM ** -0.5)).astype(BF16)
    off += D_WIDTH
    dk_ref[...] = proj[:, off:off + D_WIDTH].astype(BF16)
    off += D_WIDTH
    dv_ref[...] = proj[:, off:off + D_WIDTH].astype(BF16)
    off += D_WIDTH
    t = proj[:, off:off + LANES]
    lane = lax.broadcasted_iota(jnp.int32, t.shape, 1)
    gate = jnp.where(lane < 2 * C_HEADS, _sigmoid(t), nega_ref[...] * _softplus(t + dtb_ref[...]))
    gb_ref[...] = gate[:, :4 * C_HEADS]


def _odd_proj(x, gain, w, nega, dtb, tm=512):
    M = x.shape[0]
    row = lambda i: (i, 0)
    const = lambda i: (0, 0)
    widths = (3 * C_WIDTH, C_WIDTH, D_WIDTH, D_WIDTH, D_WIDTH, 4 * C_HEADS)
    dtypes = (F32, F32, BF16, BF16, BF16, F32)
    return pl.pallas_call(
        _odd_proj_kernel,
        grid=(M // tm,),
        in_specs=[pl.BlockSpec((tm, D_MODEL), row), pl.BlockSpec((1, D_MODEL), const),
                  pl.BlockSpec((D_MODEL, w.shape[1]), const),
                  pl.BlockSpec((1, LANES), const), pl.BlockSpec((1, LANES), const)],
        out_specs=[pl.BlockSpec((tm, n), row) for n in widths],
        out_shape=[jax.ShapeDtypeStruct((M, n), dt) for n, dt in zip(widths, dtypes)],
        compiler_params=_cparams("parallel"),
        name="odd_proj",
    )(x, gain, w, nega, dtb)


def _gdn_prep_kernel(p_ref, c_ref, n_ref, cw_ref, q_ref, k_ref, v_ref, *, nblk):
    t = pl.program_id(1)
    y = _silu(_conv4(p_ref[...], c_ref[...], n_ref[...], cw_ref[...], t == 0, t == nblk - 1))
    for h in range(C_HEADS):
        for part, ref, scale in ((0, q_ref, C_HEAD_DIM ** -0.5), (1, k_ref, 1.0)):
            a = y[:, part * C_WIDTH + h * LANES:part * C_WIDTH + (h + 1) * LANES]
            a = a * lax.rsqrt(jnp.sum(a * a, axis=-1, keepdims=True) + EPS)
            ref[:, h * LANES:(h + 1) * LANES] = a * scale
    v_ref[...] = y[:, 2 * C_WIDTH:]


def _gdn_prep(qkv, cw, B, S, tm=256):
    M = qkv.shape[0]
    nblk = S // tm
    r = tm // HALO
    nrb = M // HALO
    W = 3 * C_WIDTH
    blk = lambda b, t: b * nblk + t
    return pl.pallas_call(
        functools.partial(_gdn_prep_kernel, nblk=nblk),
        grid=(B, nblk),
        in_specs=[pl.BlockSpec((HALO, W), lambda b, t: (jnp.maximum(blk(b, t) * r - 1, 0), 0)),
                  pl.BlockSpec((tm, W), lambda b, t: (blk(b, t), 0)),
                  pl.BlockSpec((HALO, W), lambda b, t: (jnp.minimum((blk(b, t) + 1) * r, nrb - 1), 0)),
                  pl.BlockSpec((4, W), lambda b, t: (0, 0))],
        out_specs=[pl.BlockSpec((tm, C_WIDTH), lambda b, t: (blk(b, t), 0))] * 3,
        out_shape=[jax.ShapeDtypeStruct((M, C_WIDTH), F32)] * 3,
        compiler_params=_cparams("parallel", "parallel"),
        name="gdn_prep",
    )(qkv, qkv, qkv, cw)


def _split3(x):
    hi = x.astype(BF16)
    r = x - hi.astype(F32)
    mid = r.astype(BF16)
    lo = (r - mid.astype(F32)).astype(BF16)
    return hi, mid, lo


def _gdn_kernel(q_ref, k_ref, v_ref, gb_ref, o_ref, state, *, reverse, tm):
    C = C_CHUNK
    nc = tm // C
    t = pl.program_id(1)

    @pl.when(t == 0)
    def _():
        state[...] = jnp.zeros_like(state)

    ri = lax.broadcasted_iota(jnp.int32, (tm, tm), 0)
    ci = lax.broadcasted_iota(jnp.int32, (tm, tm), 1)
    same = (ri // C) == (ci // C)
    diff = (ci - ri) if reverse else (ri - ci)
    incl = same & (diff >= 0)
    strict = same & (diff > 0)
    tri = jnp.where(incl, 1.0, 0.0).astype(BF16)
    eye = jnp.where(ri == ci, 1.0, 0.0)
    hi_i, lo_i = (ci, ri) if reverse else (ri, ci)
    level_masks = []
    s = 1
    while s < C:
        level_masks.append((((hi_i // s) % 2) == 1) & ((lo_i // s) == (hi_i // s) - 1))
        s *= 2
    dcol = 4 if reverse else 0
    gb = gb_ref[...]
    for h in range(C_HEADS):
        sl = slice(h * LANES, (h + 1) * LANES)
        q = q_ref[:, sl]
        k = k_ref[:, sl]
        v = v_ref[:, sl]
        beta = jnp.broadcast_to(gb[:, dcol + h:dcol + h + 1], (tm, LANES))
        g = jnp.broadcast_to(gb[:, 2 * C_HEADS + dcol + h:2 * C_HEADS + dcol + h + 1], (tm, LANES))
        gc = sum(jnp.dot(tri, part, preferred_element_type=F32) for part in _split3(g))
        gct = jnp.broadcast_to(jnp.transpose(gc)[0:1, :], (tm, tm))
        gc2 =jnp.concatenate([gc] * (tm // LANES), axis=1)
        dec = jnp.exp(jnp.where(incl, gc2 - gct, NEG_INF))
        kbf = k.astype(BF16)
        kb = k * beta
        kk = lax.dot_general(kb.astype(BF16), kbf, (((1,), (1,)), ((), ())), preferred_element_type=F32)
        lmat = jnp.where(strict, kk * dec, 0.0)
        tinv = eye - jnp.where(level_masks[0], lmat, 0.0)
        for mask in level_masks[1:]:
            xb = tinv.astype(BF16)
            cx = jnp.dot(jnp.where(mask, lmat, 0.0).astype(BF16), xb, preferred_element_type=F32)
            tinv = tinv - jnp.dot(xb, cx.astype(BF16), preferred_element_type=F32)
        eg = jnp.exp(gc)
        rhs = jnp.concatenate([v * beta, kb * eg], axis=1).astype(BF16)
        sol = jnp.dot(tinv.astype(BF16), rhs, preferred_element_type=F32)
        u = sol[:, :LANES]
        w = sol[:, LANES:].astype(BF16)
        qk = lax.dot_general(q.astype(BF16), kbf, (((1,), (1,)), ((), ())), preferred_element_type=F32)
        amat = jnp.where(incl, qk * dec, 0.0).astype(BF16)
        qe = (q * eg).astype(BF16)
        st = state[h]
        vnew = [None] * nc
        o_st = [None] * nc
        for c in (range(nc - 1, -1, -1) if reverse else range(nc)):
            rs = slice(c * C, (c + 1) * C)
            stb = st.astype(BF16)
            vn = u[rs] - jnp.dot(w[rs], stb, preferred_element_type=F32)
            o_st[c] = jnp.dot(qe[rs], stb, preferred_element_type=F32)
            last = c * C if reverse else (c + 1) * C - 1
            gl = gc[last:last + 1, :]
            kd = (k[rs] * jnp.exp(gl - gc[rs])).astype(BF16)
            st = st * jnp.exp(gl) + lax.dot_general(kd, vn.astype(BF16), (((0,), (0,)), ((), ())),
                                                     preferred_element_type=F32)
            vnew[c] = vn
        state[h] = st
        vn_all = jnp.concatenate(vnew, axis=0).astype(BF16)
        o_ref[:, sl] = jnp.concatenate(o_st, axis=0) + jnp.dot(amat, vn_all, preferred_element_type=F32)


def _gdn(q, k, v, gb, B, S, reverse, tm=256):
    M = q.shape[0]
    nblk = S // tm
    if reverse:
        blk = lambda b, t: (b * nblk + (nblk - 1 - t), 0)
    else:
        blk = lambda b, t: (b * nblk + t, 0)
    return pl.pallas_call(
        functools.partial(_gdn_kernel, reverse=reverse, tm=tm),
        grid=(B, nblk),
        in_specs=[pl.BlockSpec((tm, C_WIDTH), blk)] * 3 + [pl.BlockSpec((tm, 4 * C_HEADS), blk)],
        out_specs=pl.BlockSpec((tm, C_WIDTH), blk),
        out_shape=jax.ShapeDtypeStruct((M, C_WIDTH), F32),
        scratch_shapes=[pltpu.VMEM((C_HEADS, C_HEAD_DIM, C_HEAD_DIM), F32)],
        compiler_params=_cparams("arbitrary", "arbitrary"),
        name="gdn_rev" if reverse else "gdn_fwd",
    )(q, k, v, gb)


def _band_kernel(q_ref, kp_ref, kc_ref, kn_ref, vp_ref, vc_ref, vn_ref, bias_ref, o_ref, lse_ref, *, tq, m_len):
    t = pl.program_id(1)
    q = q_ref[...]
    kwin = jnp.concatenate([kp_ref[...], kc_ref[...], kn_ref[...]], axis=0)
    vwin = jnp.concatenate([vp_ref[...], vc_ref[...], vn_ref[...]], axis=0)
    nkeys = tq + 2 * D_STEPS
    kpos = t * tq - D_STEPS + lax.broadcasted_iota(jnp.int32, (tq, nkeys), 1)
    valid = (kpos >= 0) & (kpos < m_len)
    lane = lax.broadcasted_iota(jnp.int32, (tq, D_GW), 1)
    o = jnp.zeros((tq, D_GW), F32)
    lse_full = jnp.zeros((tq, D_GW), F32)
    for h in range(D_HEADS_PER_GROUP):
        mine = (lane // D_HEAD_DIM) == h
        qh = jnp.where(mine, q, jnp.zeros_like(q))
        s = lax.dot_general(qh, kwin, (((1,), (1,)), ((), ())), preferred_element_type=F32) + bias_ref[h]
        s = jnp.where(valid, s, NEG_INF)
        m = jnp.max(s, axis=-1, keepdims=True)
        lse = m + jnp.log(jnp.sum(jnp.exp(s - m), axis=-1, keepdims=True))
        p = jnp.exp(s - lse).astype(BF16)
        pv = jnp.dot(p, vwin, preferred_element_type=F32)
        o = jnp.where(mine, pv, o)
        lse_full = jnp.where(mine, lse, lse_full)
    o_ref[...] = o
    lse_ref[...] = lse_full


def _band(q, k, v, bias, Z, m_len, tq=128):
    M = q.shape[0]
    nq = m_len // tq
    r = tq // D_STEPS
    nhb = M // D_STEPS
    cur = lambda z, t: (z * nq + t, 0)
    prev = lambda z, t: (jnp.maximum((z * nq + t) * r - 1, 0), 0)
    nxt = lambda z, t: (jnp.minimum((z * nq + t + 1) * r, nhb - 1), 0)
    halo = lambda f: pl.BlockSpec((D_STEPS, D_GW), f)
    full = pl.BlockSpec((tq, D_GW), cur)
    return pl.pallas_call(
        functools.partial(_band_kernel, tq=tq, m_len=m_len),
        grid=(Z, nq),
        in_specs=[full, halo(prev), full, halo(nxt), halo(prev), full, halo(nxt),
                  pl.BlockSpec((D_HEADS_PER_GROUP, tq, tq + 2 * D_STEPS), lambda z, t: (0, 0, 0))],
        out_specs=[full, full],
        out_shape=[jax.ShapeDtypeStruct((M, D_GW), F32)] * 2,
        compiler_params=_cparams("parallel", "parallel"),
        name="band_attn",
    )(q, k, k, k, v, v, v, bias)


def _odd_out_kernel(x_ref, of_ref, or_ref, z_ref, cg_ref, o0_ref, o1_ref, o2_ref, l0_ref, l1_ref, l2_ref,
                    wc_ref, wd_ref, out_ref):
    oc = of_ref[...] + or_ref[...]
    z = z_ref[...]
    parts = []
    for h in range(C_HEADS):
        sl = slice(h * LANES, (h + 1) * LANES)
        parts.append(_rms(oc[:, sl], cg_ref[...]) * _silu(z[:, sl]))
    yc = jnp.concatenate(parts, axis=1).astype(BF16)
    l0, l1, l2 = l0_ref[...], l1_ref[...], l2_ref[...]
    m = jnp.maximum(jnp.maximum(l0, l1), l2)
    e0, e1, e2 = jnp.exp(l0 - m), jnp.exp(l1 - m), jnp.exp(l2 - m)
    den = e0 + e1 + e2
    yd = ((e0 / den) * o0_ref[...] + (e1 / den) * o1_ref[...] + (e2 / den) * o2_ref[...]).astype(BF16)
    y = jnp.dot(yc, wc_ref[...], preferred_element_type=F32) + jnp.dot(yd, wd_ref[...], preferred_element_type=F32)
    out_ref[...] = x_ref[...] + y


def _odd_out(x, o_f, o_r, z, cg, os_, ls_, wc, wd, tm=512):
    M = x.shape[0]
    row = lambda i: (i, 0)
    const = lambda i: (0, 0)
    wide = pl.BlockSpec((tm, C_WIDTH), row)
    grp = pl.BlockSpec((tm, D_GW), row)
    return pl.pallas_call(
        _odd_out_kernel,
        grid=(M // tm,),
        in_specs=[pl.BlockSpec((tm, D_MODEL), row), wide, wide, wide, pl.BlockSpec((1, LANES), const),
                  grp, grp, grp, grp, grp, grp,
                  pl.BlockSpec((C_WIDTH, D_MODEL), const), pl.BlockSpec((D_GW, D_MODEL), const)],
        out_specs=pl.BlockSpec((tm, D_MODEL), row),
        out_shape=jax.ShapeDtypeStruct((M, D_MODEL), F32),
        compiler_params=_cparams("parallel"),
        name="odd_out",
    )(x, o_f, o_r, z, cg, *os_, *ls_, wc, wd)


def _t5_bucket(rel):
    nb = N_BUCKETS // 2
    max_exact = nb // 2
    n = np.abs(rel)
    large = max_exact + (np.log(np.maximum(n, 1) / max_exact) / math.log(MAX_DISTANCE / max_exact)
                         * (nb - max_exact)).astype(np.int64)
    large = np.minimum(large, nb - 1)
    return (np.where(rel > 0, nb, 0) + np.where(n < max_exact, n, large)).astype(np.int32)


def _band_bias(rel_bias, tq=128):
    i = np.arange(tq)[:, None]
    j = np.arange(tq + 2 * D_STEPS)[None, :]
    delta = j - D_STEPS - i
    in_band = np.abs(delta) <= D_STEPS
    idx = np.clip(delta + D_STEPS, 0, 2 * D_STEPS)
    tiles = []
    for gi, (window, dil) in enumerate(D_GROUPS):
        steps = window // (2 * dil)
        assert steps == D_STEPS
        buckets = _t5_bucket(np.arange(-steps, steps + 1) * dil)
        b = rel_bias.astype(F32)[jnp.asarray(buckets)][:, gi * D_HEADS_PER_GROUP:(gi + 1) * D_HEADS_PER_GROUP].T
        tiles.append(jnp.where(jnp.asarray(in_band)[None], b[:, jnp.asarray(idx)], NEG_INF))
    return jnp.stack(tiles)


def _prep_odd(w_in, w_out, conv_w, a_log, dt_bias, o_gain, rel_bias):
    n_c = 4 * C_WIDTH
    n_ba = 4 * C_HEADS
    w = jnp.concatenate([w_in[:, :n_c], w_in[:, n_c + n_ba:], w_in[:, n_c:n_c + n_ba],
                         jnp.zeros((D_MODEL, LANES - n_ba), w_in.dtype)], axis=1).astype(BF16)
    pad = jnp.zeros((LANES - n_ba,), F32)
    nega = jnp.concatenate([jnp.zeros((2 * C_HEADS,), F32), -jnp.exp(a_log.astype(F32)).reshape(-1), pad])[None]
    dtb = jnp.concatenate([jnp.zeros((2 * C_HEADS,), F32), dt_bias.astype(F32).reshape(-1), pad])[None]
    return dict(w=w, nega=nega, dtb=dtb, cw=conv_w.astype(F32), cg=o_gain.astype(F32)[None],
                bias=_band_bias(rel_bias), wc=w_out[:C_WIDTH].astype(BF16), wd=w_out[C_WIDTH:].astype(BF16))


def _odd_layer(x, gain, p, B, S):
    qkv, z, dq, dk, dv, gb = _odd_proj(x, gain, p['w'], p['nega'], p['dtb'])
    cq, ck, cv = _gdn_prep(qkv, p['cw'], B, S)
    o_f = _gdn(cq, ck, cv, gb, B, S, False)
    o_r = _gdn(cq, ck, cv, gb, B, S, True)
    outs, lses = [], []
    for gi, (_, dil) in enumerate(D_GROUPS):
        m_len = S // dil

        def to_sub(a):
            a = a[:, gi * D_GW:(gi + 1) * D_GW]
            if dil == 1:
                return a
            return a.reshape(B, m_len, dil, D_GW).transpose(0, 2, 1, 3).reshape(B * S, D_GW)

        def from_sub(a):
            if dil == 1:
                return a
            return a.reshape(B, dil, m_len, D_GW).transpose(0, 2, 1, 3).reshape(B * S, D_GW)

        o, lse = _band(to_sub(dq), to_sub(dk), to_sub(dv), p['bias'][gi], B * dil, m_len)
        outs.append(from_sub(o))
        lses.append(from_sub(lse))
    return _odd_out(x, o_f, o_r, z, p['cg'], outs, lses, p['wc'], p['wd'])


def _trunk(x3, evens, odds, norm_mix, norm_ff, norm_final, w1, w2):
    B, S, _ = x3.shape
    x = x3.reshape(B * S, D_MODEL)
    cos, sin = _rope_tables(S)
    for layer in range(DEPTH):
        gain = norm_mix[layer][None]
        if layer % 2 == 0:
            x = _even_layer(x, gain, evens[layer // 2], cos, sin, B, S)
        else:
            x = _odd_layer(x, gain, odds[layer // 2], B, S)
        x = _mlp(x, norm_ff[layer][None], w1[layer], w2[layer])
    return _final_norm(x, norm_final[None]).reshape(B, S, D_MODEL)


def kernel(x_prompt, x_sample, rel_bias, norm_mix, norm_ff, norm_final, w_ff1, w_ff2, w_in_e, w_out_e,
           a_qnorm, a_knorm, b_conv_w, b_conv_b, b_wr, b_br, b_wi, b_bi, b_lambda, w_in_o, w_out_o,
           c_conv_w, c_a_log, c_dt_bias, c_norm):
    evens = [_prep_even(w_in_e[j], w_out_e[j], a_qnorm[j], a_knorm[j], b_conv_w[j], b_conv_b[j],
                        b_wr[j], b_br[j], b_wi[j], b_bi[j], b_lambda[j]) for j in range((DEPTH + 1) // 2)]
    odds = [_prep_odd(w_in_o[j], w_out_o[j], c_conv_w[j], c_a_log[j], c_dt_bias[j], c_norm[j], rel_bias)
            for j in range(DEPTH // 2)]
    w1 = w_ff1.astype(BF16)
    w2 = w_ff2.astype(BF16)
    nm = norm_mix.astype(F32)
    nf = norm_ff.astype(F32)
    ng = norm_final.astype(F32)
    y_prompt = _trunk(x_prompt, evens, odds, nm, nf, ng, w1, w2)
    y_sample = _trunk(x_sample, evens, odds, nm, nf, ng, w1, w2)
    return (y_prompt, y_sample)
```

```python
import functools
import math

import numpy as np
import jax
import jax.numpy as jnp
from jax import lax
from jax.experimental import pallas as pl
from jax.experimental.pallas import tpu as pltpu

F32 = jnp.float32
BF16 = jnp.bfloat16

D_MODEL = 1024
D_FF = 4 * D_MODEL
DEPTH = 4
EPS = 1e-6
NEG_INF = -1e30
GRID_W = 64
LANES = 128

A_HEADS = 8
A_KV_HEADS = 2
A_HEAD_DIM = 64
A_Q = A_HEADS * A_HEAD_DIM
A_KV = A_KV_HEADS * A_HEAD_DIM
ROPE_THETA = 10000.0
B_WIDTH = 512
B_BLOCKS = 8
B_BLOCK_DIM = B_WIDTH // B_BLOCKS
RG_C = 8.0
C_HEADS = 4
C_HEAD_DIM = 128
C_WIDTH = C_HEADS * C_HEAD_DIM
C_CHUNK = 64
D_GROUPS = ((128, 1), (512, 4), (2048, 16))
D_HEADS_PER_GROUP = 4
D_HEAD_DIM = 64
D_NHEADS = len(D_GROUPS) * D_HEADS_PER_GROUP
D_WIDTH = D_NHEADS * D_HEAD_DIM
D_GW = D_HEADS_PER_GROUP * D_HEAD_DIM
D_STEPS = 64
N_BUCKETS = 32
MAX_DISTANCE = 1024
HALO = 8
Q_SCALE = A_HEAD_DIM ** -0.5 * math.log2(math.e)
BAND_QT = 128
BAND_TQ = 512
VT_PAD = 16
VT_ROWS = A_KV_HEADS * (A_HEAD_DIM + VT_PAD)

VMEM_LIMIT = 56 * 1024 * 1024


def _cparams(*sem):
    return pltpu.CompilerParams(dimension_semantics=sem, vmem_limit_bytes=VMEM_LIMIT)


def _rms(x, gain):
    return x * lax.rsqrt(jnp.mean(x * x, axis=-1, keepdims=True) + EPS) * gain


def _softplus(x):
    return jnp.maximum(x, 0.0) + jnp.log1p(jnp.exp(-jnp.abs(x)))


def _sigmoid(x):
    return 1.0 / (1.0 + jnp.exp(-x))


def _silu(x):
    return x * _sigmoid(x)


def _gelu_tanh(x):
    return 0.5 * x * (1.0 + jnp.tanh(math.sqrt(2.0 / math.pi) * (x + 0.044715 * (x * x * x))))


def _conv4(prev, cur, nxt, w, first, last):
    rows = cur.shape[0]
    prev = jnp.where(first, 0.0, prev)
    nxt = jnp.where(last, 0.0, nxt)
    full = jnp.concatenate([prev, cur, nxt], axis=0)
    y = full[HALO - 2:HALO - 2 + rows] * w[0:1]
    for j in range(1, 4):
        y = y + full[HALO - 2 + j:HALO - 2 + j + rows] * w[j:j + 1]
    return y


def _even_proj_kernel(x_ref, g_ref, w_ref, wvt_ref, qg_ref, kg_ref, cos_ref, sin_ref,
                      q_ref, k_ref, vt_ref, xr_ref, gr_ref):
    xn = _rms(x_ref[...], g_ref[...]).astype(BF16)
    proj = jnp.dot(xn, w_ref[...], preferred_element_type=F32)
    vt = lax.dot_general(wvt_ref[...], xn, (((1,), (1,)), ((), ())), preferred_element_type=F32).astype(BF16)
    ones = jnp.ones((VT_PAD, LANES), BF16)
    for c in range(vt_ref.shape[0]):
        cols = slice(c * LANES, (c + 1) * LANES)
        vt_ref[c] = jnp.concatenate([vt[:A_HEAD_DIM, cols], ones, vt[A_HEAD_DIM:, cols], ones], axis=0)
    cos = cos_ref[...]
    sin = sin_ref[...]
    lane = lax.broadcasted_iota(jnp.int32, cos.shape, 1)
    first_half = (lane % 32) < 16
    lo = lane < A_HEAD_DIM

    def rope(t):
        swapped = jnp.where(first_half, pltpu.roll(t, LANES - 16, 1), pltpu.roll(t, 16, 1))
        return t * cos + swapped * sin

    for h in range(A_HEADS):
        t = proj[:, h * LANES:(h + 1) * LANES]
        r = lax.rsqrt(jnp.sum(t * t, axis=-1, keepdims=True) * (1.0 / A_HEAD_DIM) + EPS)
        t = rope(t * r * qg_ref[:, h * LANES:(h + 1) * LANES])
        q_ref[:, h * LANES:(h + 1) * LANES] = (t * Q_SCALE).astype(BF16)
    off = A_HEADS * LANES
    t = proj[:, off:off + LANES]
    t2 = t * t
    s_lo = jnp.sum(jnp.where(lo, t2, 0.0), axis=-1, keepdims=True)
    s_hi = jnp.sum(jnp.where(lo, 0.0, t2), axis=-1, keepdims=True)
    r = lax.rsqrt(jnp.where(lo, s_lo, s_hi) * (1.0 / A_HEAD_DIM) + EPS)
    k_ref[...] = rope(t * r * kg_ref[...]).astype(BF16)
    off += LANES
    xr_ref[...] = proj[:, off:off + B_WIDTH]
    off += B_WIDTH
    gr_ref[...] = proj[:, off:off + B_WIDTH]


def _even_proj(x, gain, w, wvt, qg, kg, cos, sin, S, tm=512):
    M = x.shape[0]
    nS = S // tm
    n_in = w.shape[1]
    row = lambda i: (i, 0)
    const = lambda i: (0, 0)
    return pl.pallas_call(
        _even_proj_kernel,
        grid=(M // tm,),
        in_specs=[pl.BlockSpec((tm, D_MODEL), row), pl.BlockSpec((1, D_MODEL), const),
                  pl.BlockSpec((D_MODEL, n_in), const), pl.BlockSpec((A_KV, D_MODEL), const),
                  pl.BlockSpec((1, A_HEADS * LANES), const), pl.BlockSpec((1, LANES), const),
                  pl.BlockSpec((tm, LANES), lambda i: (i % nS, 0)),
                  pl.BlockSpec((tm, LANES), lambda i: (i % nS, 0))],
        out_specs=[pl.BlockSpec((tm, A_HEADS * LANES), row), pl.BlockSpec((tm, LANES), row),
                   pl.BlockSpec((tm // LANES, VT_ROWS, LANES), lambda i: (i, 0, 0)),
                   pl.BlockSpec((tm, B_WIDTH), row), pl.BlockSpec((tm, B_WIDTH), row)],
        out_shape=[jax.ShapeDtypeStruct((M, A_HEADS * LANES), BF16),
                   jax.ShapeDtypeStruct((M, LANES), BF16),
                   jax.ShapeDtypeStruct((M // LANES, VT_ROWS, LANES), BF16),
                   jax.ShapeDtypeStruct((M, B_WIDTH), F32), jax.ShapeDtypeStruct((M, B_WIDTH), F32)],
        compiler_params=_cparams("parallel"),
        name="even_proj",
    )(x, gain, w, wvt, qg, kg, cos, sin)


def _gqa_kernel(q_ref, k_ref, vt_ref, o_ref, m_ref, acc_ref, *, tq, nslab, unroll, lookahead):
    ntile = A_HEADS // 2
    qt = [jnp.concatenate([q_ref[:, (2 * p) * LANES:(2 * p + 1) * LANES],
                           q_ref[:, (2 * p + 1) * LANES:(2 * p + 2) * LANES]], axis=0) for p in range(ntile)]
    m_ref[...] = jnp.full(m_ref.shape, NEG_INF, F32)
    acc_ref[...] = jnp.zeros(acc_ref.shape, F32)
    vrows = VT_ROWS // A_KV_HEADS

    def scores(j, u, p):
        ks = k_ref[pl.ds(pl.multiple_of((j * unroll + u) * LANES, LANES), LANES), :]
        return lax.dot_general(ks, qt[p], (((1,), (1,)), ((), ())), preferred_element_type=F32)

    def update(j, u, p, st):
        g = (2 * p) // (A_HEADS // A_KV_HEADS)
        m_old = m_ref[p]
        m_new = jnp.maximum(m_old, jnp.max(st, axis=0, keepdims=True))
        alpha = jnp.exp2(m_old - m_new)
        pt = jnp.exp2(st - m_new).astype(BF16)
        pv = jnp.dot(vt_ref[j * unroll + u, g * vrows:(g + 1) * vrows, :], pt, preferred_element_type=F32)
        acc_ref[p] = alpha * acc_ref[p] + pv
        m_ref[p] = m_new

    pairs = [(u, p) for u in range(unroll) for p in range(ntile)]
    niter = nslab // unroll

    def body(j, pending):
        pending = list(pending)
        j_next = jnp.minimum(j + 1, niter - 1)
        for idx in range(len(pairs)):
            ahead = idx + lookahead
            if ahead < len(pairs):
                pending.append(scores(j, *pairs[ahead]))
            else:
                pending.append(scores(j_next, *pairs[ahead - len(pairs)]))
            update(j, *pairs[idx], pending.pop(0))
        return tuple(pending)

    lax.fori_loop(0, niter, body, tuple(scores(0, *pairs[i]) for i in range(lookahead)))
    for p in range(ntile):
        acc = acc_ref[p]
        o = acc[:A_HEAD_DIM] / acc[A_HEAD_DIM:A_HEAD_DIM + 1]
        both = jnp.concatenate([o[:, :tq], o[:, tq:]], axis=0)
        o_ref[:, p * LANES:(p + 1) * LANES] = jnp.transpose(both).astype(o_ref.dtype)


def _gqa(q, k, vt, B, S, tq=128, unroll=8, lookahead=6):
    M = q.shape[0]
    nq = S // tq
    nslab = S // LANES
    kern = functools.partial(_gqa_kernel, tq=tq, nslab=nslab, unroll=unroll, lookahead=lookahead)
    return pl.pallas_call(
        kern,
        grid=(B, nq),
        in_specs=[pl.BlockSpec((tq, A_HEADS * LANES), lambda b, i: (b * nq + i, 0)),
                  pl.BlockSpec((S, LANES), lambda b, i: (b, 0)),
                  pl.BlockSpec((nslab, VT_ROWS, LANES), lambda b, i: (b, 0, 0))],
        out_specs=pl.BlockSpec((tq, A_Q), lambda b, i: (b * nq + i, 0)),
        out_shape=jax.ShapeDtypeStruct((M, A_Q), BF16),
        scratch_shapes=[pltpu.VMEM((A_HEADS // 2, 1, 2 * tq), F32),
                        pltpu.VMEM((A_HEADS // 2, VT_ROWS // A_KV_HEADS, 2 * tq), F32)],
        compiler_params=_cparams("parallel", "parallel"),
        name="gqa",
    )(q, k, vt)


def _rglru_kernel(xp_ref, xc_ref, xn_ref, yp_ref, yc_ref, yn_ref, cw_ref, cb_ref, wg_ref, bg_ref, lam_ref,
                  hf_ref, hr_ref, a_s, u_s, carry_s, *, tm, nblk):
    t = pl.program_id(1)

    @pl.when(t == 0)
    def _():
        carry_s[...] = jnp.zeros_like(carry_s)

    row8 = lax.broadcasted_iota(jnp.int32, (tm, B_WIDTH), 0) % 8
    cw = cw_ref[...]
    for d, (p_ref, c_ref, n_ref, out_ref) in enumerate(((xp_ref, xc_ref, xn_ref, hf_ref),
                                                       (yp_ref, yc_ref, yn_ref, hr_ref))):
        tt = t if d == 0 else nblk - 1 - t
        xc = _conv4(p_ref[...], c_ref[...], n_ref[...], cw, tt == 0, tt == nblk - 1) + cb_ref[...]
        xb = xc.astype(BF16)
        r = _sigmoid(jnp.dot(xb, wg_ref[2 * d], preferred_element_type=F32) + bg_ref[2 * d:2 * d + 1])
        i = _sigmoid(jnp.dot(xb, wg_ref[2 * d + 1], preferred_element_type=F32) + bg_ref[2 * d + 1:2 * d + 2])
        log_a = (-RG_C) * r * _softplus(-lam_ref[d:d + 1])
        a = jnp.exp(log_a)
        u = jnp.sqrt(1.0 - jnp.exp(2.0 * log_a)) * (i * xc)
        for sh in (1, 2, 4):
            if d == 0:
                keep = row8 >= sh
                a_sh = pltpu.roll(a, sh, 0)
                u_sh = pltpu.roll(u, sh, 0)
            else:
                keep = row8 < 8 - sh
                a_sh = pltpu.roll(a, tm - sh, 0)
                u_sh = pltpu.roll(u, tm - sh, 0)
            u = u + a * jnp.where(keep, u_sh, 0.0)
            a = a * jnp.where(keep, a_sh, 1.0)
        a_s[...] = a
        u_s[...] = u
        ngrp = tm // 8

        def body(gidx, h):
            gg = gidx if d == 0 else ngrp - 1 - gidx
            start = pl.multiple_of(gg * 8, 8)
            hg = u_s[pl.ds(start, 8), :] + a_s[pl.ds(start, 8), :] * h
            out_ref[pl.ds(start, 8), :] = hg
            edge = hg[7:8, :] if d == 0 else hg[0:1, :]
            return jnp.broadcast_to(edge, (8, B_WIDTH))

        carry_s[d] = lax.fori_loop(0, ngrp, body, carry_s[d], unroll=4)


def _rglru(xr, cw, cb, wg, bg, lam, B, S, tm=256):
    M = xr.shape[0]
    nblk = S // tm
    r = tm // HALO
    nrb = M // HALO
    fwd = lambda b, t: b * nblk + t
    rev = lambda b, t: b * nblk + (nblk - 1 - t)
    specs = []
    for blk in (fwd, rev):
        specs += [pl.BlockSpec((HALO, B_WIDTH), lambda b, t, blk=blk: (jnp.maximum(blk(b, t) * r - 1, 0), 0)),
                  pl.BlockSpec((tm, B_WIDTH), lambda b, t, blk=blk: (blk(b, t), 0)),
                  pl.BlockSpec((HALO, B_WIDTH), lambda b, t, blk=blk: (jnp.minimum((blk(b, t) + 1) * r, nrb - 1), 0))]
    const2 = lambda b, t: (0, 0)
    const3 = lambda b, t: (0, 0, 0)
    kern = functools.partial(_rglru_kernel, tm=tm, nblk=nblk)
    return pl.pallas_call(
        kern,
        grid=(B, nblk),
        in_specs=specs + [pl.BlockSpec((4, B_WIDTH), const2), pl.BlockSpec((1, B_WIDTH), const2),
                          pl.BlockSpec((4, B_WIDTH, B_WIDTH), const3), pl.BlockSpec((4, B_WIDTH), const2),
                          pl.BlockSpec((2, B_WIDTH), const2)],
        out_specs=[pl.BlockSpec((tm, B_WIDTH), lambda b, t: (fwd(b, t), 0)),
                   pl.BlockSpec((tm, B_WIDTH), lambda b, t: (rev(b, t), 0))],
        out_shape=[jax.ShapeDtypeStruct((M, B_WIDTH), F32), jax.ShapeDtypeStruct((M, B_WIDTH), F32)],
        scratch_shapes=[pltpu.VMEM((tm, B_WIDTH), F32), pltpu.VMEM((tm, B_WIDTH), F32),
                        pltpu.VMEM((2, 8, B_WIDTH), F32)],
        compiler_params=_cparams("arbitrary", "arbitrary"),
        name="rglru",
    )(xr, xr, xr, xr, xr, xr, cw, cb, wg, bg, lam)


def _even_out_kernel(x_ref, ya_ref, hf_ref, hr_ref, gr_ref, wa_ref, wb_ref, o_ref):
    yb = ((hf_ref[...] + hr_ref[...]) * _gelu_tanh(gr_ref[...])).astype(BF16)
    y = jnp.dot(ya_ref[...], wa_ref[...], preferred_element_type=F32)
    y = y + jnp.dot(yb, wb_ref[...], preferred_element_type=F32)
    o_ref[...] = x_ref[...] + y


def _even_out(x, ya, hf, hr, gr, wa, wb, tm=512):
    M = x.shape[0]
    row = lambda i: (i, 0)
    const = lambda i: (0, 0)
    return pl.pallas_call(
        _even_out_kernel,
        grid=(M // tm,),
        in_specs=[pl.BlockSpec((tm, D_MODEL), row), pl.BlockSpec((tm, A_Q), row),
                  pl.BlockSpec((tm, B_WIDTH), row), pl.BlockSpec((tm, B_WIDTH), row),
                  pl.BlockSpec((tm, B_WIDTH), row),
                  pl.BlockSpec((A_Q, D_MODEL), const), pl.BlockSpec((B_WIDTH, D_MODEL), const)],
        out_specs=pl.BlockSpec((tm, D_MODEL), row),
        out_shape=jax.ShapeDtypeStruct((M, D_MODEL), F32),
        compiler_params=_cparams("parallel"),
        name="even_out",
    )(x, ya, hf, hr, gr, wa, wb)


def _mlp_kernel(x_ref, g_ref, w1_ref, w2_ref, o_ref, *, tf):
    x = x_ref[...]
    xn = _rms(x, g_ref[...]).astype(BF16)
    acc = x
    for c in range(D_FF // tf):
        h = jnp.dot(xn, w1_ref[:, c * tf:(c + 1) * tf], preferred_element_type=F32)
        h = jnp.square(jnp.maximum(h, 0.0)).astype(BF16)
        acc = acc + jnp.dot(h, w2_ref[c * tf:(c + 1) * tf, :], preferred_element_type=F32)
    o_ref[...] = acc


def _mlp(x, gain, w1, w2, tm=512, tf=1024):
    M = x.shape[0]
    row = lambda i: (i, 0)
    const = lambda i: (0, 0)
    return pl.pallas_call(
        functools.partial(_mlp_kernel, tf=tf),
        grid=(M // tm,),
        in_specs=[pl.BlockSpec((tm, D_MODEL), row), pl.BlockSpec((1, D_MODEL), const),
                  pl.BlockSpec((D_MODEL, D_FF), const), pl.BlockSpec((D_FF, D_MODEL), const)],
        out_specs=pl.BlockSpec((tm, D_MODEL), row),
        out_shape=jax.ShapeDtypeStruct((M, D_MODEL), F32),
        compiler_params=_cparams("parallel"),
        name="mlp",
    )(x, gain, w1, w2)


def _final_norm_kernel(x_ref, g_ref, o_ref):
    o_ref[...] = _rms(x_ref[...], g_ref[...])


def _final_norm(x, gain, tm=1024):
    M = x.shape[0]
    return pl.pallas_call(
        _final_norm_kernel,
        grid=(M // tm,),
        in_specs=[pl.BlockSpec((tm, D_MODEL), lambda i: (i, 0)), pl.BlockSpec((1, D_MODEL), lambda i: (0, 0))],
        out_specs=pl.BlockSpec((tm, D_MODEL), lambda i: (i, 0)),
        out_shape=jax.ShapeDtypeStruct((M, D_MODEL), F32),
        compiler_params=_cparams("parallel"),
        name="final_norm",
    )(x, gain)


def _rope_tables(S):
    rows = S // GRID_W
    row = jnp.repeat(jnp.arange(rows, dtype=F32), GRID_W)
    col = jnp.tile(jnp.arange(GRID_W, dtype=F32), rows)
    n_freq = A_HEAD_DIM // 4
    inv = ROPE_THETA ** (-jnp.arange(n_freq, dtype=F32) / n_freq)
    ang_r = row[:, None] * inv
    ang_c = col[:, None] * inv
    cos = jnp.concatenate([jnp.cos(ang_r)] * 2 + [jnp.cos(ang_c)] * 2, axis=1)
    sin = jnp.concatenate([-jnp.sin(ang_r), jnp.sin(ang_r), -jnp.sin(ang_c), jnp.sin(ang_c)], axis=1)
    return jnp.tile(cos, (1, 2)), jnp.tile(sin, (1, 2))


def _prep_even(w_in, w_out, qn, kn, conv_w, conv_b, wr, br, wi, bi, lam):
    G = A_HEADS // A_KV_HEADS
    wq = w_in[:, :A_Q].reshape(D_MODEL, A_HEADS, A_HEAD_DIM)
    zeros = jnp.zeros_like(wq)
    half = (jnp.arange(A_HEADS) // G)[None, :, None]
    wq_pad = jnp.concatenate([jnp.where(half == 0, wq, zeros), jnp.where(half == 1, wq, zeros)], axis=-1)
    w = jnp.concatenate([wq_pad.reshape(D_MODEL, A_HEADS * LANES), w_in[:, A_Q:A_Q + A_KV],
                         w_in[:, A_Q + 2 * A_KV:]], axis=1).astype(BF16)
    wvt = w_in[:, A_Q + A_KV:A_Q + 2 * A_KV].T.astype(BF16)
    qg =jnp.tile(qn.astype(F32), 2 * A_HEADS)[None]
    kg = jnp.tile(kn.astype(F32), 2)[None]

    def dense(blocks):
        eye = jnp.eye(B_BLOCKS, dtype=blocks.dtype)
        return jnp.einsum('nde,nm->ndme', blocks, eye).reshape(B_WIDTH, B_WIDTH)

    wg = jnp.stack([dense(wr[0]), dense(wi[0]), dense(wr[1]), dense(wi[1])]).astype(BF16)
    bg = jnp.stack([br[0], bi[0], br[1], bi[1]]).astype(F32)
    return dict(w=w, wvt=wvt, qg=qg, kg=kg, cw=conv_w.astype(F32), cb=conv_b.astype(F32)[None], wg=wg, bg=bg,
                lam=lam.astype(F32), wa=w_out[:A_Q].astype(BF16), wb=w_out[A_Q:].astype(BF16))


def _even_layer(x, gain, p, cos, sin, B, S):
    q, k, vt, xr, gr = _even_proj(x, gain, p['w'], p['wvt'], p['qg'], p['kg'], cos, sin, S)
    ya = _gqa(q, k, vt, B, S)
    hf, hr = _rglru(xr, p['cw'], p['cb'], p['wg'], p['bg'], p['lam'], B, S)
    return _even_out(x, ya, hf, hr, gr, p['wa'], p['wb'])


def _odd_proj_kernel(x_ref, g_ref, w_ref, nega_ref, dtb_ref,
                     qkv_ref, z_ref, dq_ref, dk_ref, dv_ref, gb_ref):
    xn = _rms(x_ref[...], g_ref[...]).astype(BF16)
    proj = jnp.dot(xn, w_ref[...], preferred_element_type=F32)
    off = 3 * C_WIDTH
    qkv_ref[...] = proj[:, :off]
    z_ref[...] = proj[:, off:off + C_WIDTH]
    off += C_WIDTH
    dq_ref[...] = (proj[:, off:off + D_WIDTH] * (D_HEAD_DIM ** -0.5)).astype(BF16)
    off += D_WIDTH
    dk_ref[...] = proj[:, off:off + D_WIDTH].astype(BF16)
    off += D_WIDTH
    dv_ref[...] = proj[:, off:off + D_WIDTH].astype(BF16)
    off += D_WIDTH
    t = proj[:, off:off + LANES]
    lane = lax.broadcasted_iota(jnp.int32, t.shape, 1)
    gate = jnp.where(lane < 2 * C_HEADS, _sigmoid(t), nega_ref[...] * _softplus(t + dtb_ref[...]))
    gb_ref[...] = jnp.transpose(gate)[:4 * C_HEADS]


def _odd_proj(x, gain, w, nega, dtb, tm=512):
    M = x.shape[0]
    row = lambda i: (i, 0)
    const = lambda i: (0, 0)
    widths = (3 * C_WIDTH, C_WIDTH, D_WIDTH, D_WIDTH, D_WIDTH)
    dtypes = (F32, F32, BF16, BF16, BF16)
    return pl.pallas_call(
        _odd_proj_kernel,
        grid=(M // tm,),
        in_specs=[pl.BlockSpec((tm, D_MODEL), row), pl.BlockSpec((1, D_MODEL), const),
                  pl.BlockSpec((D_MODEL, w.shape[1]), const),
                  pl.BlockSpec((1, LANES), const), pl.BlockSpec((1, LANES), const)],
        out_specs=[pl.BlockSpec((tm, n), row) for n in widths]
        + [pl.BlockSpec((4 * C_HEADS, tm), lambda i: (0, i))],
        out_shape=[jax.ShapeDtypeStruct((M, n), dt) for n, dt in zip(widths, dtypes)]
        + [jax.ShapeDtypeStruct((4 * C_HEADS, M), F32)],
        compiler_params=_cparams("parallel"),
        name="odd_proj",
    )(x, gain, w, nega, dtb)


def _gdn_prep_kernel(p_ref, c_ref, n_ref, cw_ref, q_ref, k_ref, v_ref, *, nblk):
    t = pl.program_id(1)
    y = _silu(_conv4(p_ref[...], c_ref[...], n_ref[...], cw_ref[...], t == 0, t == nblk - 1))
    for h in range(C_HEADS):
        for part, ref, scale in ((0, q_ref, C_HEAD_DIM ** -0.5), (1, k_ref, 1.0)):
            a = y[:, part * C_WIDTH + h * LANES:part * C_WIDTH + (h + 1) * LANES]
            a = a * lax.rsqrt(jnp.sum(a * a, axis=-1, keepdims=True) + EPS)
            ref[:, h * LANES:(h + 1) * LANES] = a * scale
    v_ref[...] = y[:, 2 * C_WIDTH:]


def _gdn_prep(qkv, cw, B, S, tm=256):
    M = qkv.shape[0]
    nblk = S // tm
    r = tm // HALO
    nrb = M // HALO
    W = 3 * C_WIDTH
    blk = lambda b, t: b * nblk + t
    return pl.pallas_call(
        functools.partial(_gdn_prep_kernel, nblk=nblk),
        grid=(B, nblk),
        in_specs=[pl.BlockSpec((HALO, W), lambda b, t: (jnp.maximum(blk(b, t) * r - 1, 0), 0)),
                  pl.BlockSpec((tm, W), lambda b, t: (blk(b, t), 0)),
                  pl.BlockSpec((HALO, W), lambda b, t: (jnp.minimum((blk(b, t) + 1) * r, nrb - 1), 0)),
                  pl.BlockSpec((4, W), lambda b, t: (0, 0))],
        out_specs=[pl.BlockSpec((tm, C_WIDTH), lambda b, t: (blk(b, t), 0))] * 3,
        out_shape=[jax.ShapeDtypeStruct((M, C_WIDTH), F32)] * 3,
        compiler_params=_cparams("parallel", "parallel"),
        name="gdn_prep",
    )(qkv, qkv, qkv, cw)


def _nt_dot(a, b):
    return lax.dot_general(a, b, (((1,), (1,)), ((), ())), preferred_element_type=F32)


def _gdn_kernel(qf_ref, kf_ref, vf_ref, gf_ref, qr_ref, kr_ref, vr_ref, gr_ref, of_ref, or_ref, state, *, tm):
    C = C_CHUNK
    SB = LANES
    nsb = tm // SB
    nc = tm // C
    t = pl.program_id(1)

    @pl.when(t == 0)
    def _():
        state[...] = jnp.zeros_like(state)

    ri = lax.broadcasted_iota(jnp.int32, (SB, SB), 0)
    ci = lax.broadcasted_iota(jnp.int32, (SB, SB), 1)
    same = (ri // C) == (ci // C)
    eye = jnp.where(ri == ci, 1.0, 0.0)
    incl, strict, levels = [], [], []
    for d in range(2):
        hi_i, lo_i = (ci, ri) if d else (ri, ci)
        incl.append(same & (hi_i >= lo_i))
        strict.append(same & (hi_i > lo_i))
        lv = []
        s = 1
        while s < C:
            lv.append((((hi_i // s) % 2) == 1) & ((lo_i // s) == (hi_i // s) - 1))
            s *= 2
        levels.append(lv)

    pos = lax.broadcasted_iota(jnp.int32, (4 * C_HEADS, tm), 1) % C
    rows, cols = [], []
    for d, g_ref in enumerate((gf_ref, gr_ref)):
        gt = g_ref[...]
        x = gt
        s = 1
        while s < C:
            if d == 0:
                x = x + jnp.where(pos >= s, pltpu.roll(x, s, 1), 0.0)
            else:
                x = x + jnp.where(pos < C - s, pltpu.roll(x, tm - s, 1), 0.0)
            s *= 2
        lo = d * C_HEADS
        rw = jnp.concatenate([gt[lo:lo + C_HEADS], x[2 * C_HEADS + lo:2 * C_HEADS + lo + C_HEADS]], axis=0)
        rows.append(rw)
        cols.append(jnp.transpose(jnp.concatenate([rw, jnp.zeros((LANES - 2 * C_HEADS, tm), F32)], axis=0)))

    refs = ((qf_ref, kf_ref, vf_ref, of_ref), (qr_ref, kr_ref, vr_ref, or_ref))
    chains = []
    for d in range(2):
        for h in range(C_HEADS):
            for sb in range(nsb):
                rs = slice(sb * SB, (sb + 1) * SB)
                sl = slice(h * LANES, (h + 1) * LANES)
                q = refs[d][0][rs, sl]
                k = refs[d][1][rs, sl]
                v = refs[d][2][rs, sl]
                beta = jnp.broadcast_to(cols[d][rs, h:h + 1], (SB, LANES))
                gcc = jnp.broadcast_to(cols[d][rs, C_HEADS + h:C_HEADS + h + 1], (SB, LANES))
                gcr = jnp.broadcast_to(rows[d][C_HEADS + h:C_HEADS + h + 1, rs], (SB, SB))
                dec = jnp.exp(jnp.where(incl[d], gcc - gcr, NEG_INF))
                eg = jnp.exp(gcc)
                kbf = k.astype(BF16)
                kb = k * beta
                lmat = jnp.where(strict[d], _nt_dot(kb.astype(BF16), kbf) * dec, 0.0)
                amat = jnp.where(incl[d], _nt_dot(q.astype(BF16), kbf) * dec, 0.0).astype(BF16)
                chains.append(dict(d=d, h=h, sb=sb, k=k, gcc=gcc, lmat=lmat, amat=amat,
                                   rhs=jnp.concatenate([v * beta, kb * eg], axis=1).astype(BF16),
                                   qe=(q * eg).astype(BF16)))

    for ch in chains:
        ch['x'] = eye - jnp.where(levels[ch['d']][0], ch['lmat'], 0.0)
    for li in range(1, len(levels[0])):
        for ch in chains:
            ch['xb'] = ch['x'].astype(BF16)
            cmat = jnp.where(levels[ch['d']][li], ch['lmat'], 0.0).astype(BF16)
            ch['cx'] = jnp.dot(cmat, ch['xb'], preferred_element_type=F32).astype(BF16)
        for ch in chains:
            ch['x'] = ch['x'] - jnp.dot(ch['xb'], ch['cx'], preferred_element_type=F32)
    for ch in chains:
        sol = jnp.dot(ch['x'].astype(BF16), ch['rhs'], preferred_element_type=F32)
        ch['u'] = sol[:, :LANES]
        ch['w'] = sol[:, LANES:].astype(BF16)
        ch['vn'] = [None] * (SB // C)
        ch['ost'] = [None] * (SB // C)

    by_key = {(ch['d'], ch['h'], ch['sb']): ch for ch in chains}
    st = {(d, h): state[d, h] for d in range(2) for h in range(C_HEADS)}
    for step in range(nc):
        work = []
        for (d, h), s_val in st.items():
            c = nc - 1 - step if d else step
            ch = by_key[(d, h, c // (SB // C))]
            lc = c % (SB // C)
            rs = slice(lc * C, (lc + 1) * C)
            stb = s_val.astype(BF16)
            vn = ch['u'][rs] - jnp.dot(ch['w'][rs], stb, preferred_element_type=F32)
            ch['ost'][lc] = jnp.dot(ch['qe'][rs], stb, preferred_element_type=F32)
            ch['vn'][lc] = vn
            last = lc * C if d else (lc + 1) * C - 1
            gl = ch['gcc'][last:last + 1, :]
            kd = (ch['k'][rs] * jnp.exp(gl - ch['gcc'][rs])).astype(BF16)
            work.append(((d, h), s_val * jnp.exp(gl), kd, vn.astype(BF16)))
        for key, decayed, kd, vnb in work:
            st[key] = decayed + lax.dot_general(kd, vnb, (((0,), (0,)), ((), ())), preferred_element_type=F32)
    for (d, h), s_val in st.items():
        state[d, h] = s_val
    for ch in chains:
        rs = slice(ch['sb'] * SB, (ch['sb'] + 1) * SB)
        sl = slice(ch['h'] * LANES, (ch['h'] + 1) * LANES)
        vn_all = jnp.concatenate(ch['vn'], axis=0).astype(BF16)
        refs[ch['d']][3][rs, sl] = (jnp.concatenate(ch['ost'], axis=0)
                                   + jnp.dot(ch['amat'], vn_all, preferred_element_type=F32))


def _gdn(q, k, v, gbt, B, S, tm=256):
    M = q.shape[0]
    nblk = S // tm
    fwd = lambda b, t: b * nblk + t
    rev = lambda b, t: b * nblk + (nblk - 1 - t)
    specs = []
    for blk in (fwd, rev):
        specs += [pl.BlockSpec((tm, C_WIDTH), lambda b, t, blk=blk: (blk(b, t), 0))] * 3
        specs += [pl.BlockSpec((4 * C_HEADS, tm), lambda b, t, blk=blk: (0, blk(b, t)))]
    return pl.pallas_call(
        functools.partial(_gdn_kernel, tm=tm),
        grid=(B, nblk),
        in_specs=specs,
        out_specs=[pl.BlockSpec((tm, C_WIDTH), lambda b, t: (fwd(b, t), 0)),
                   pl.BlockSpec((tm, C_WIDTH), lambda b, t: (rev(b, t), 0))],
        out_shape=[jax.ShapeDtypeStruct((M, C_WIDTH), F32)] * 2,
        scratch_shapes=[pltpu.VMEM((2, C_HEADS, C_HEAD_DIM, C_HEAD_DIM), F32)],
        compiler_params=_cparams("arbitrary", "arbitrary"),
        name="gdn",
    )(q, k, v, gbt, q, k, v, gbt)


def _band_kernel(q_ref, kp_ref, kc_ref, kn_ref, vp_ref, vc_ref, vn_ref, bias_ref, o_ref, lse_ref, *, tq, m_len):
    t = pl.program_id(1)
    QT = BAND_QT
    nkeys = QT + 2 * D_STEPS
    kwin = jnp.concatenate([kp_ref[...], kc_ref[...], kn_ref[...]], axis=0)
    vwin = jnp.concatenate([vp_ref[...], vc_ref[...], vn_ref[...]], axis=0)
    lane = lax.broadcasted_iota(jnp.int32, (QT, D_GW), 1)
    mine = [(lane // D_HEAD_DIM) == h for h in range(D_HEADS_PER_GROUP)]
    col = lax.broadcasted_iota(jnp.int32, (QT, nkeys), 1)
    tiles = []
    for i in range(tq // QT):
        q = q_ref[i * QT:(i + 1) * QT, :]
        kpos = t * tq + i * QT - D_STEPS + col
        valid = (kpos >= 0) & (kpos < m_len)
        kw = kwin[i * QT:i * QT + nkeys]
        for h in range(D_HEADS_PER_GROUP):
            qh = jnp.where(mine[h], q, jnp.zeros_like(q))
            s = lax.dot_general(qh, kw, (((1,), (1,)), ((), ())), preferred_element_type=F32) + bias_ref[h]
            tiles.append([i, h, jnp.where(valid, s, NEG_INF)])
    for tile in tiles:
        s = tile[2]
        m = jnp.max(s, axis=-1, keepdims=True)
        e = jnp.exp(s - m)
        l = jnp.sum(e, axis=-1, keepdims=True)
        tile[2] = (e / l).astype(BF16)
        tile.append(m + jnp.log(l))
    for i in range(tq // QT):
        o = jnp.zeros((QT, D_GW), F32)
        lse_full = jnp.zeros((QT, D_GW), F32)
        vw = vwin[i * QT:i * QT + nkeys]
        for _, h, p, lse in tiles[i * D_HEADS_PER_GROUP:(i + 1) * D_HEADS_PER_GROUP]:
            o = jnp.where(mine[h], jnp.dot(p, vw, preferred_element_type=F32), o)
            lse_full = jnp.where(mine[h], lse, lse_full)
        o_ref[i * QT:(i + 1) * QT, :] = o
        lse_ref[i * QT:(i + 1) * QT, :] = lse_full


def _band(q, k, v, bias, Z, m_len):
    M = q.shape[0]
    tq = min(BAND_TQ, m_len)
    nq = m_len // tq
    r = tq // D_STEPS
    nhb = M // D_STEPS
    cur = lambda z, t: (z * nq + t, 0)
    prev = lambda z, t: (jnp.maximum((z * nq + t) * r - 1, 0), 0)
    nxt = lambda z, t: (jnp.minimum((z * nq + t + 1) * r, nhb - 1), 0)
    halo = lambda f: pl.BlockSpec((D_STEPS, D_GW), f)
    full = pl.BlockSpec((tq, D_GW), cur)
    return pl.pallas_call(
        functools.partial(_band_kernel, tq=tq, m_len=m_len),
        grid=(Z, nq),
        in_specs=[full, halo(prev), full, halo(nxt), halo(prev), full, halo(nxt),
                  pl.BlockSpec((D_HEADS_PER_GROUP, BAND_QT, BAND_QT + 2 * D_STEPS), lambda z, t: (0, 0, 0))],
        out_specs=[full, full],
        out_shape=[jax.ShapeDtypeStruct((M, D_GW), F32)] * 2,
        compiler_params=_cparams("parallel", "parallel"),
        name="band_attn",
    )(q, k, k, k, v, v, v, bias)


def _odd_out_kernel(x_ref, of_ref, or_ref, z_ref, cg_ref, o0_ref, o1_ref, o2_ref, l0_ref, l1_ref, l2_ref,
                    wc_ref, wd_ref, out_ref):
    oc = of_ref[...] + or_ref[...]
    z = z_ref[...]
    parts = []
    for h in range(C_HEADS):
        sl = slice(h * LANES, (h + 1) * LANES)
        parts.append(_rms(oc[:, sl], cg_ref[...]) * _silu(z[:, sl]))
    yc = jnp.concatenate(parts, axis=1).astype(BF16)
    l0, l1, l2 = l0_ref[...], l1_ref[...], l2_ref[...]
    m = jnp.maximum(jnp.maximum(l0, l1), l2)
    e0, e1, e2 = jnp.exp(l0 - m), jnp.exp(l1 - m), jnp.exp(l2 - m)
    den = e0 + e1 + e2
    yd = ((e0 / den) * o0_ref[...] + (e1 / den) * o1_ref[...] + (e2 / den) * o2_ref[...]).astype(BF16)
    y = jnp.dot(yc, wc_ref[...], preferred_element_type=F32) + jnp.dot(yd, wd_ref[...], preferred_element_type=F32)
    out_ref[...] = x_ref[...] + y


def _odd_out(x, o_f, o_r, z, cg, os_, ls_, wc, wd, tm=512):
    M = x.shape[0]
    row = lambda i: (i, 0)
    const = lambda i: (0, 0)
    wide = pl.BlockSpec((tm, C_WIDTH), row)
    grp = pl.BlockSpec((tm, D_GW), row)
    return pl.pallas_call(
        _odd_out_kernel,
        grid=(M // tm,),
        in_specs=[pl.BlockSpec((tm, D_MODEL), row), wide, wide, wide, pl.BlockSpec((1, LANES), const),
                  grp, grp, grp, grp, grp, grp,
                  pl.BlockSpec((C_WIDTH, D_MODEL), const), pl.BlockSpec((D_GW, D_MODEL), const)],
        out_specs=pl.BlockSpec((tm, D_MODEL), row),
        out_shape=jax.ShapeDtypeStruct((M, D_MODEL), F32),
        compiler_params=_cparams("parallel"),
        name="odd_out",
    )(x, o_f, o_r, z, cg, *os_, *ls_, wc, wd)


def _t5_bucket(rel):
    nb = N_BUCKETS // 2
    max_exact = nb // 2
    n = np.abs(rel)
    large = max_exact + (np.log(np.maximum(n, 1) / max_exact) / math.log(MAX_DISTANCE / max_exact)
                         * (nb - max_exact)).astype(np.int64)
    large = np.minimum(large, nb - 1)
    return (np.where(rel > 0, nb, 0) + np.where(n < max_exact, n, large)).astype(np.int32)


def _band_bias(rel_bias, tq=128):
    i = np.arange(tq)[:, None]
    j = np.arange(tq + 2 * D_STEPS)[None, :]
    delta = j - D_STEPS - i
    in_band = np.abs(delta) <= D_STEPS
    idx = np.clip(delta + D_STEPS, 0, 2 * D_STEPS)
    tiles = []
    for gi, (window, dil) in enumerate(D_GROUPS):
        steps = window // (2 * dil)
        assert steps == D_STEPS
        buckets = _t5_bucket(np.arange(-steps, steps + 1) * dil)
        b = rel_bias.astype(F32)[jnp.asarray(buckets)][:, gi * D_HEADS_PER_GROUP:(gi + 1) * D_HEADS_PER_GROUP].T
        tiles.append(jnp.where(jnp.asarray(in_band)[None], b[:, jnp.asarray(idx)], NEG_INF))
    return jnp.stack(tiles)


def _prep_odd(w_in, w_out, conv_w, a_log, dt_bias, o_gain, rel_bias):
    n_c = 4 * C_WIDTH
    n_ba = 4 * C_HEADS
    w = jnp.concatenate([w_in[:, :n_c], w_in[:, n_c + n_ba:], w_in[:, n_c:n_c + n_ba],
                         jnp.zeros((D_MODEL, LANES - n_ba), w_in.dtype)], axis=1).astype(BF16)
    pad = jnp.zeros((LANES - n_ba,), F32)
    nega = jnp.concatenate([jnp.zeros((2 * C_HEADS,), F32), -jnp.exp(a_log.astype(F32)).reshape(-1), pad])[None]
    dtb = jnp.concatenate([jnp.zeros((2 * C_HEADS,), F32), dt_bias.astype(F32).reshape(-1), pad])[None]
    return dict(w=w, nega=nega, dtb=dtb, cw=conv_w.astype(F32), cg=o_gain.astype(F32)[None],
                bias=_band_bias(rel_bias), wc=w_out[:C_WIDTH].astype(BF16), wd=w_out[C_WIDTH:].astype(BF16))


def _odd_layer(x, gain, p, B, S):
    qkv, z, dq, dk, dv, gb = _odd_proj(x, gain, p['w'], p['nega'], p['dtb'])
    cq, ck, cv = _gdn_prep(qkv, p['cw'], B, S)
    o_f, o_r = _gdn(cq, ck, cv, gb, B, S)
    outs, lses = [], []
    for gi, (_, dil) in enumerate(D_GROUPS):
        m_len = S // dil

        def to_sub(a):
            a = a[:, gi * D_GW:(gi + 1) * D_GW]
            if dil == 1:
                return a
            return a.reshape(B, m_len, dil, D_GW).transpose(0, 2, 1, 3).reshape(B * S, D_GW)

        def from_sub(a):
            if dil == 1:
                return a
            return a.reshape(B, dil, m_len, D_GW).transpose(0, 2, 1, 3).reshape(B * S, D_GW)

        o, lse = _band(to_sub(dq), to_sub(dk), to_sub(dv), p['bias'][gi], B * dil, m_len)
        outs.append(from_sub(o))
        lses.append(from_sub(lse))
    return _odd_out(x, o_f, o_r, z, p['cg'], outs, lses, p['wc'], p['wd'])


def _trunk(x3, evens, odds, norm_mix, norm_ff, norm_final, w1, w2):
    B, S, _ = x3.shape
    x = x3.reshape(B * S, D_MODEL)
    cos, sin = _rope_tables(S)
    for layer in range(DEPTH):
        gain = norm_mix[layer][None]
        if layer % 2 == 0:
            x = _even_layer(x, gain, evens[layer // 2], cos, sin, B, S)
        else:
            x = _odd_layer(x, gain, odds[layer // 2], B, S)
        x = _mlp(x, norm_ff[layer][None], w1[layer], w2[layer])
    return _final_norm(x, norm_final[None]).reshape(B, S, D_MODEL)


def kernel(x_prompt, x_sample, rel_bias, norm_mix, norm_ff, norm_final, w_ff1, w_ff2, w_in_e, w_out_e,
           a_qnorm, a_knorm, b_conv_w, b_conv_b, b_wr, b_br, b_wi, b_bi, b_lambda, w_in_o, w_out_o,
           c_conv_w, c_a_log, c_dt_bias, c_norm):
    evens = [_prep_even(w_in_e[j], w_out_e[j], a_qnorm[j], a_knorm[j], b_conv_w[j], b_conv_b[j],
                        b_wr[j], b_br[j], b_wi[j], b_bi[j], b_lambda[j]) for j in range((DEPTH + 1) // 2)]
    odds = [_prep_odd(w_in_o[j], w_out_o[j], c_conv_w[j], c_a_log[j], c_dt_bias[j], c_norm[j], rel_bias)
            for j in range(DEPTH // 2)]
    w1 = w_ff1.astype(BF16)
    w2 = w_ff2.astype(BF16)
    nm = norm_mix.astype(F32)
    nf = norm_ff.astype(F32)
    ng = norm_final.astype(F32)
    y_prompt = _trunk(x_prompt, evens, odds, nm, nf, ng, w1, w2)
    y_sample = _trunk(x_sample, evens, odds, nm, nf, ng, w1, w2)
    return (y_prompt, y_sample)
```

```python
import functools
import math

import numpy as np
import jax
import jax.numpy as jnp
from jax import lax
from jax.experimental import pallas as pl
from jax.experimental.pallas import tpu as pltpu

F32 = jnp.float32
BF16 = jnp.bfloat16

D_MODEL = 1024
D_FF = 4 * D_MODEL
DEPTH = 4
EPS = 1e-6
NEG_INF = -1e30
GRID_W = 64
LANES = 128

A_HEADS = 8
A_KV_HEADS = 2
A_HEAD_DIM = 64
A_Q = A_HEADS * A_HEAD_DIM
A_KV = A_KV_HEADS * A_HEAD_DIM
ROPE_THETA = 10000.0
B_WIDTH = 512
B_BLOCKS = 8
B_BLOCK_DIM = B_WIDTH // B_BLOCKS
RG_C = 8.0
C_HEADS = 4
C_HEAD_DIM = 128
C_WIDTH = C_HEADS * C_HEAD_DIM
C_CHUNK = 64
D_GROUPS = ((128, 1), (512, 4), (2048, 16))
D_HEADS_PER_GROUP = 4
D_HEAD_DIM = 64
D_NHEADS = len(D_GROUPS) * D_HEADS_PER_GROUP
D_WIDTH = D_NHEADS * D_HEAD_DIM
D_GW = D_HEADS_PER_GROUP * D_HEAD_DIM
D_STEPS = 64
N_BUCKETS = 32
MAX_DISTANCE = 1024
HALO = 8
Q_SCALE = A_HEAD_DIM ** -0.5 * math.log2(math.e)
MLP_TF = 1024
BAND_QT = 128
BAND_TQ = 512
VT_PAD = 16
VT_ROWS = A_KV_HEADS * (A_HEAD_DIM + VT_PAD)

VMEM_LIMIT = 56 * 1024 * 1024


def _cparams(*sem):
    return pltpu.CompilerParams(dimension_semantics=sem, vmem_limit_bytes=VMEM_LIMIT)


def _rms(x, gain):
    return x * lax.rsqrt(jnp.mean(x * x, axis=-1, keepdims=True) + EPS) * gain


def _softplus(x):
    return jnp.maximum(x, 0.0) + jnp.log1p(jnp.exp(-jnp.abs(x)))


def _sigmoid(x):
    return 0.5 * jnp.tanh(0.5 * x) + 0.5


def _silu(x):
    return x * _sigmoid(x)


def _gelu_tanh(x):
    return 0.5 * x * (1.0 + jnp.tanh(math.sqrt(2.0 / math.pi) * (x + 0.044715 * (x * x * x))))


def _conv4(prev, cur, nxt, w, first, last):
    rows = cur.shape[0]
    prev = jnp.where(first, 0.0, prev)
    nxt = jnp.where(last, 0.0, nxt)
    full = jnp.concatenate([prev, cur, nxt], axis=0)
    y = full[HALO - 2:HALO - 2 + rows] * w[0:1]
    for j in range(1, 4):
        y = y + full[HALO - 2 + j:HALO - 2 + j + rows] * w[j:j + 1]
    return y


def _even_proj_kernel(x_ref, g_ref, w_ref, wvt_ref, qg_ref, kg_ref, cos_ref, sin_ref,
                      q_ref, k_ref, vt_ref, xr_ref, gr_ref):
    xn = _rms(x_ref[...], g_ref[...]).astype(BF16)
    proj = jnp.dot(xn, w_ref[...], preferred_element_type=F32)
    vt = lax.dot_general(wvt_ref[...], xn, (((1,), (1,)), ((), ())), preferred_element_type=F32).astype(BF16)
    ones = jnp.ones((VT_PAD, LANES), BF16)
    for c in range(vt_ref.shape[0]):
        cols = slice(c * LANES, (c + 1) * LANES)
        vt_ref[c] = jnp.concatenate([vt[:A_HEAD_DIM, cols], ones, vt[A_HEAD_DIM:, cols], ones], axis=0)
    cos = cos_ref[...]
    sin = sin_ref[...]
    lane = lax.broadcasted_iota(jnp.int32, cos.shape, 1)
    first_half = (lane % 32) < 16
    lo = lane < A_HEAD_DIM

    def rope(t):
        swapped = jnp.where(first_half, pltpu.roll(t, LANES - 16, 1), pltpu.roll(t, 16, 1))
        return t * cos + swapped * sin

    for h in range(A_HEADS):
        t = proj[:, h * LANES:(h + 1) * LANES]
        r = lax.rsqrt(jnp.sum(t * t, axis=-1, keepdims=True) * (1.0 / A_HEAD_DIM) + EPS)
        t = rope(t * r * qg_ref[:, h * LANES:(h + 1) * LANES])
        q_ref[:, h * LANES:(h + 1) * LANES] = (t * Q_SCALE).astype(BF16)
    off = A_HEADS * LANES
    t = proj[:, off:off + LANES]
    t2 = t * t
    s_lo = jnp.sum(jnp.where(lo, t2, 0.0), axis=-1, keepdims=True)
    s_hi = jnp.sum(jnp.where(lo, 0.0, t2), axis=-1, keepdims=True)
    r = lax.rsqrt(jnp.where(lo, s_lo, s_hi) * (1.0 / A_HEAD_DIM) + EPS)
    k_ref[...] = rope(t * r * kg_ref[...]).astype(BF16)
    off += LANES
    xr_ref[...] = proj[:, off:off + B_WIDTH]
    off += B_WIDTH
    gr_ref[...] = proj[:, off:off + B_WIDTH]


def _even_proj(x, gain, w, wvt, qg, kg, cos, sin, S, tm=512):
    M = x.shape[0]
    nS = S // tm
    n_in = w.shape[1]
    row = lambda i: (i, 0)
    const = lambda i: (0, 0)
    return pl.pallas_call(
        _even_proj_kernel,
        grid=(M // tm,),
        in_specs=[pl.BlockSpec((tm, D_MODEL), row), pl.BlockSpec((1, D_MODEL), const),
                  pl.BlockSpec((D_MODEL, n_in), const), pl.BlockSpec((A_KV, D_MODEL), const),
                  pl.BlockSpec((1, A_HEADS * LANES), const), pl.BlockSpec((1, LANES), const),
                  pl.BlockSpec((tm, LANES), lambda i: (i % nS, 0)),
                  pl.BlockSpec((tm, LANES), lambda i: (i % nS, 0))],
        out_specs=[pl.BlockSpec((tm, A_HEADS * LANES), row), pl.BlockSpec((tm, LANES), row),
                   pl.BlockSpec((tm // LANES, VT_ROWS, LANES), lambda i: (i, 0, 0)),
                   pl.BlockSpec((tm, B_WIDTH), row), pl.BlockSpec((tm, B_WIDTH), row)],
        out_shape=[jax.ShapeDtypeStruct((M, A_HEADS * LANES), BF16),
                   jax.ShapeDtypeStruct((M, LANES), BF16),
                   jax.ShapeDtypeStruct((M // LANES, VT_ROWS, LANES), BF16),
                   jax.ShapeDtypeStruct((M, B_WIDTH), F32), jax.ShapeDtypeStruct((M, B_WIDTH), F32)],
        compiler_params=_cparams("parallel"),
        name="even_proj",
    )(x, gain, w, wvt, qg, kg, cos, sin)


def _gqa_kernel(q_ref, k_ref, vt_ref, o_ref, m_ref, acc_ref, *, tq, nslab, unroll, lookahead):
    ntile = A_HEADS // 2
    qt = [jnp.concatenate([q_ref[:, (2 * p) * LANES:(2 * p + 1) * LANES],
                           q_ref[:, (2 * p + 1) * LANES:(2 * p + 2) * LANES]], axis=0) for p in range(ntile)]
    m_ref[...] = jnp.full(m_ref.shape, NEG_INF, F32)
    acc_ref[...] = jnp.zeros(acc_ref.shape, F32)
    vrows = VT_ROWS // A_KV_HEADS

    def scores(j, u, p):
        ks = k_ref[pl.ds(pl.multiple_of((j * unroll + u) * LANES, LANES), LANES), :]
        return lax.dot_general(ks, qt[p], (((1,), (1,)), ((), ())), preferred_element_type=F32)

    def update(j, u, p, st):
        g = (2 * p) // (A_HEADS // A_KV_HEADS)
        m_old = m_ref[p]
        m_new = jnp.maximum(m_old, jnp.max(st, axis=0, keepdims=True))
        alpha = jnp.exp2(m_old - m_new)
        pt = jnp.exp2(st - m_new).astype(BF16)
        pv = jnp.dot(vt_ref[j * unroll + u, g * vrows:(g + 1) * vrows, :], pt, preferred_element_type=F32)
        acc_ref[p] = alpha * acc_ref[p] + pv
        m_ref[p] = m_new

    pairs = [(u, p) for u in range(unroll) for p in range(ntile)]
    niter = nslab // unroll

    def body(j, pending):
        pending = list(pending)
        j_next = jnp.minimum(j + 1, niter - 1)
        for idx in range(len(pairs)):
            ahead = idx + lookahead
            if ahead < len(pairs):
                pending.append(scores(j, *pairs[ahead]))
            else:
                pending.append(scores(j_next, *pairs[ahead - len(pairs)]))
            update(j, *pairs[idx], pending.pop(0))
        return tuple(pending)

    lax.fori_loop(0, niter, body, tuple(scores(0, *pairs[i]) for i in range(lookahead)))
    for p in range(ntile):
        acc = acc_ref[p]
        o = acc[:A_HEAD_DIM] / acc[A_HEAD_DIM:A_HEAD_DIM + 1]
        both = jnp.concatenate([o[:, :tq], o[:, tq:]], axis=0)
        o_ref[:, p * LANES:(p + 1) * LANES] = jnp.transpose(both).astype(o_ref.dtype)


def _gqa(q, k, vt, B, S, tq=128, unroll=16, lookahead=6):
    M = q.shape[0]
    nq = S // tq
    nslab = S // LANES
    unroll = min(unroll, nslab)
    kern = functools.partial(_gqa_kernel, tq=tq, nslab=nslab, unroll=unroll, lookahead=lookahead)
    return pl.pallas_call(
        kern,
        grid=(B, nq),
        in_specs=[pl.BlockSpec((tq, A_HEADS * LANES), lambda b, i: (b * nq + i, 0)),
                  pl.BlockSpec((S, LANES), lambda b, i: (b, 0)),
                  pl.BlockSpec((nslab, VT_ROWS, LANES), lambda b, i: (b, 0, 0))],
        out_specs=pl.BlockSpec((tq, A_Q), lambda b, i: (b * nq + i, 0)),
        out_shape=jax.ShapeDtypeStruct((M, A_Q), BF16),
        scratch_shapes=[pltpu.VMEM((A_HEADS // 2, 1, 2 * tq), F32),
                        pltpu.VMEM((A_HEADS // 2, VT_ROWS // A_KV_HEADS, 2 * tq), F32)],
        compiler_params=_cparams("parallel", "parallel"),
        name="gqa",
    )(q, k, vt)


def _rglru_kernel(xp_ref, xc_ref, xn_ref, yp_ref, yc_ref, yn_ref, cw_ref, cb_ref, wg_ref, bg_ref, lam_ref,
                  hf_ref, hr_ref, a_s, u_s, carry_s, *, tm, nblk):
    t = pl.program_id(1)

    @pl.when(t == 0)
    def _():
        carry_s[...] = jnp.zeros_like(carry_s)

    cw = cw_ref[...]
    ngrp = tm // 8
    row8 = lax.broadcasted_iota(jnp.int32, (ngrp, 8, B_WIDTH), 1)
    for d, (p_ref, c_ref, n_ref) in enumerate(((xp_ref, xc_ref, xn_ref), (yp_ref, yc_ref, yn_ref))):
        tt = t if d == 0 else nblk - 1 - t
        xc = _conv4(p_ref[...], c_ref[...], n_ref[...], cw, tt == 0, tt == nblk - 1) + cb_ref[...]
        xb = xc.astype(BF16)
        r = _sigmoid(jnp.dot(xb, wg_ref[2 * d], preferred_element_type=F32) + bg_ref[2 * d:2 * d + 1])
        i = _sigmoid(jnp.dot(xb, wg_ref[2 * d + 1], preferred_element_type=F32) + bg_ref[2 * d + 1:2 * d + 2])
        log_a = (-RG_C) * r * _softplus(-lam_ref[d:d + 1])
        a = jnp.exp(log_a).reshape(ngrp, 8, B_WIDTH)
        u = (jnp.sqrt(1.0 - jnp.exp(2.0 * log_a)) * (i * xc)).reshape(ngrp, 8, B_WIDTH)
        for sh in (1, 2, 4):
            keep = (row8 >= sh) if d == 0 else (row8 < 8 - sh)
            shift = sh if d == 0 else 8 - sh
            u = u + a * jnp.where(keep, pltpu.roll(u, shift, 1), 0.0)
            a = a * jnp.where(keep, pltpu.roll(a, shift, 1), 1.0)
        a_s[d] = a.reshape(tm, B_WIDTH)
        u_s[d] = u.reshape(tm, B_WIDTH)

    def body(gidx, carry):
        out = []
        for d, out_ref in enumerate((hf_ref, hr_ref)):
            gg = gidx if d == 0 else ngrp - 1 - gidx
            start = pl.multiple_of(gg * 8, 8)
            hg = u_s[d, pl.ds(start, 8), :] + a_s[d, pl.ds(start, 8), :] * carry[d]
            out_ref[pl.ds(start, 8), :] = hg
            edge = hg[7:8, :] if d == 0 else hg[0:1, :]
            out.append(jnp.broadcast_to(edge, (8, B_WIDTH)))
        return tuple(out)

    ends = lax.fori_loop(0, ngrp, body, (carry_s[0], carry_s[1]), unroll=4)
    carry_s[0] = ends[0]
    carry_s[1] = ends[1]


def _rglru(xr, cw, cb, wg, bg, lam, B, S, tm=256):
    M = xr.shape[0]
    nblk = S // tm
    r = tm // HALO
    nrb = M // HALO
    fwd = lambda b, t: b * nblk + t
    rev = lambda b, t: b * nblk + (nblk - 1 - t)
    specs = []
    for blk in (fwd, rev):
        specs += [pl.BlockSpec((HALO, B_WIDTH), lambda b, t, blk=blk: (jnp.maximum(blk(b, t) * r - 1, 0), 0)),
                  pl.BlockSpec((tm, B_WIDTH), lambda b, t, blk=blk: (blk(b, t), 0)),
                  pl.BlockSpec((HALO, B_WIDTH), lambda b, t, blk=blk: (jnp.minimum((blk(b, t) + 1) * r, nrb - 1), 0))]
    const2 = lambda b, t: (0, 0)
    const3 = lambda b, t: (0, 0, 0)
    kern = functools.partial(_rglru_kernel, tm=tm, nblk=nblk)
    return pl.pallas_call(
        kern,
        grid=(B, nblk),
        in_specs=specs + [pl.BlockSpec((4, B_WIDTH), const2), pl.BlockSpec((1, B_WIDTH), const2),
                          pl.BlockSpec((4, B_WIDTH, B_WIDTH), const3), pl.BlockSpec((4, B_WIDTH), const2),
                          pl.BlockSpec((2, B_WIDTH), const2)],
        out_specs=[pl.BlockSpec((tm, B_WIDTH), lambda b, t: (fwd(b, t), 0)),
                   pl.BlockSpec((tm, B_WIDTH), lambda b, t: (rev(b, t), 0))],
        out_shape=[jax.ShapeDtypeStruct((M, B_WIDTH), F32), jax.ShapeDtypeStruct((M, B_WIDTH), F32)],
        scratch_shapes=[pltpu.VMEM((2, tm, B_WIDTH), F32), pltpu.VMEM((2, tm, B_WIDTH), F32),
                        pltpu.VMEM((2, 8, B_WIDTH), F32)],
        compiler_params=_cparams("arbitrary", "arbitrary"),
        name="rglru",
    )(xr, xr, xr, xr, xr, xr, cw, cb, wg, bg, lam)


def _mlp_tail(x, ff, final):
    gff_ref, w1_ref, w2_ref, gfin_ref = ff
    xn = _rms(x, gff_ref[...]).astype(BF16)
    acc = x
    for c in range(D_FF // MLP_TF):
        h = jnp.dot(xn, w1_ref[:, c * MLP_TF:(c + 1) * MLP_TF], preferred_element_type=F32)
        h = jnp.square(jnp.maximum(h, 0.0)).astype(BF16)
        acc = acc + jnp.dot(h, w2_ref[c * MLP_TF:(c + 1) * MLP_TF, :], preferred_element_type=F32)
    return _rms(acc, gfin_ref[...]) if final else acc


def _resident(shape):
    return pl.BlockSpec(shape, lambda *_: (0,) * len(shape), pipeline_mode=pl.Buffered(1))


def _ff_specs():
    return [_resident((1, D_MODEL)), _resident((D_MODEL, D_FF)), _resident((D_FF, D_MODEL)),
            _resident((1, D_MODEL))]


def _even_out_kernel(x_ref, ya_ref, hf_ref, hr_ref, gr_ref, wa_ref, wb_ref, *rest, final):
    *ff, o_ref = rest
    yb = ((hf_ref[...] + hr_ref[...]) * _gelu_tanh(gr_ref[...])).astype(BF16)
    y = jnp.dot(ya_ref[...], wa_ref[...], preferred_element_type=F32)
    y = y + jnp.dot(yb, wb_ref[...], preferred_element_type=F32)
    o_ref[...] = _mlp_tail(x_ref[...] + y, ff, final)


def _even_out(x, ya, hf, hr, gr, wa, wb, ff, final, tm=512):
    M = x.shape[0]
    row = lambda i: (i, 0)
    return pl.pallas_call(
        functools.partial(_even_out_kernel, final=final),
        grid=(M // tm,),
        in_specs=[pl.BlockSpec((tm, D_MODEL), row), pl.BlockSpec((tm, A_Q), row),
                  pl.BlockSpec((tm, B_WIDTH), row), pl.BlockSpec((tm, B_WIDTH), row),
                  pl.BlockSpec((tm, B_WIDTH), row),
                  _resident((A_Q, D_MODEL)), _resident((B_WIDTH, D_MODEL))] + _ff_specs(),
        out_specs=pl.BlockSpec((tm, D_MODEL), row),
        out_shape=jax.ShapeDtypeStruct((M, D_MODEL), F32),
        compiler_params=_cparams("parallel"),
        name="even_out_mlp",
    )(x, ya, hf, hr, gr, wa, wb, *ff)


def _rope_tables(S):
    rows = S // GRID_W
    row = jnp.repeat(jnp.arange(rows, dtype=F32), GRID_W)
    col = jnp.tile(jnp.arange(GRID_W, dtype=F32), rows)
    n_freq = A_HEAD_DIM // 4
    inv = ROPE_THETA ** (-jnp.arange(n_freq, dtype=F32) / n_freq)
    ang_r = row[:, None] * inv
    ang_c = col[:, None] * inv
    cos = jnp.concatenate([jnp.cos(ang_r)] * 2 + [jnp.cos(ang_c)] * 2, axis=1)
    sin = jnp.concatenate([-jnp.sin(ang_r), jnp.sin(ang_r), -jnp.sin(ang_c), jnp.sin(ang_c)], axis=1)
    return jnp.tile(cos, (1, 2)), jnp.tile(sin, (1, 2))


def _prep_even(w_in, w_out, qn, kn, conv_w, conv_b, wr, br, wi, bi, lam):
    G = A_HEADS // A_KV_HEADS
    wq = w_in[:, :A_Q].reshape(D_MODEL, A_HEADS, A_HEAD_DIM)
    zeros = jnp.zeros_like(wq)
    half = (jnp.arange(A_HEADS) // G)[None, :, None]
    wq_pad = jnp.concatenate([jnp.where(half == 0, wq, zeros), jnp.where(half == 1, wq, zeros)], axis=-1)
    w = jnp.concatenate([wq_pad.reshape(D_MODEL, A_HEADS * LANES), w_in[:, A_Q:A_Q + A_KV],
                         w_in[:, A_Q + 2 * A_KV:]], axis=1).astype(BF16)
    wvt = w_in[:, A_Q + A_KV:A_Q + 2 * A_KV].T.astype(BF16)
    qg =jnp.tile(qn.astype(F32), 2 * A_HEADS)[None]
    kg = jnp.tile(kn.astype(F32), 2)[None]

    def dense(blocks):
        eye = jnp.eye(B_BLOCKS, dtype=blocks.dtype)
        return jnp.einsum('nde,nm->ndme', blocks, eye).reshape(B_WIDTH, B_WIDTH)

    wg = jnp.stack([dense(wr[0]), dense(wi[0]), dense(wr[1]), dense(wi[1])]).astype(BF16)
    bg = jnp.stack([br[0], bi[0], br[1], bi[1]]).astype(F32)
    return dict(w=w, wvt=wvt, qg=qg, kg=kg, cw=conv_w.astype(F32), cb=conv_b.astype(F32)[None], wg=wg, bg=bg,
                lam=lam.astype(F32), wa=w_out[:A_Q].astype(BF16), wb=w_out[A_Q:].astype(BF16))


def _even_layer(x, gain, p, ff, final, cos, sin, B, S):
    q, k, vt, xr, gr = _even_proj(x, gain, p['w'], p['wvt'], p['qg'], p['kg'], cos, sin, S)
    ya = _gqa(q, k, vt, B, S)
    hf, hr = _rglru(xr, p['cw'], p['cb'], p['wg'], p['bg'], p['lam'], B, S)
    return _even_out(x, ya, hf, hr, gr, p['wa'], p['wb'], ff, final)


def _odd_proj_kernel(x_ref, g_ref, w_ref, nega_ref, dtb_ref,
                     qkv_ref, z_ref, dq_ref, dk_ref, dv_ref, gb_ref):
    xn = _rms(x_ref[...], g_ref[...]).astype(BF16)
    proj = jnp.dot(xn, w_ref[...], preferred_element_type=F32)
    off = 3 * C_WIDTH
    qkv_ref[...] = proj[:, :off]
    z_ref[...] = proj[:, off:off + C_WIDTH]
    off += C_WIDTH
    dq_ref[...] = (proj[:, off:off + D_WIDTH] * (D_HEAD_DIM ** -0.5)).astype(BF16)
    off += D_WIDTH
    dk_ref[...] = proj[:, off:off + D_WIDTH].astype(BF16)
    off += D_WIDTH
    dv_ref[...] = proj[:, off:off + D_WIDTH].astype(BF16)
    off += D_WIDTH
    t = proj[:, off:off + LANES]
    lane = lax.broadcasted_iota(jnp.int32, t.shape, 1)
    gate = jnp.where(lane < 2 * C_HEADS, _sigmoid(t), nega_ref[...] * _softplus(t + dtb_ref[...]))
    gb_ref[...] = jnp.transpose(gate)[:4 * C_HEADS]


def _odd_proj(x, gain, w, nega, dtb, tm=512):
    M = x.shape[0]
    row = lambda i: (i, 0)
    const = lambda i: (0, 0)
    widths = (3 * C_WIDTH, C_WIDTH, D_WIDTH, D_WIDTH, D_WIDTH)
    dtypes = (F32, F32, BF16, BF16, BF16)
    return pl.pallas_call(
        _odd_proj_kernel,
        grid=(M // tm,),
        in_specs=[pl.BlockSpec((tm, D_MODEL), row), pl.BlockSpec((1, D_MODEL), const),
                  pl.BlockSpec((D_MODEL, w.shape[1]), const),
                  pl.BlockSpec((1, LANES), const), pl.BlockSpec((1, LANES), const)],
        out_specs=[pl.BlockSpec((tm, n), row) for n in widths]
        + [pl.BlockSpec((4 * C_HEADS, tm), lambda i: (0, i))],
        out_shape=[jax.ShapeDtypeStruct((M, n), dt) for n, dt in zip(widths, dtypes)]
        + [jax.ShapeDtypeStruct((4 * C_HEADS, M), F32)],
        compiler_params=_cparams("parallel"),
        name="odd_proj",
    )(x, gain, w, nega, dtb)


def _gdn_prep_kernel(p_ref, c_ref, n_ref, cw_ref, q_ref, k_ref, v_ref, *, nblk):
    t = pl.program_id(1)
    y = _silu(_conv4(p_ref[...], c_ref[...], n_ref[...], cw_ref[...], t == 0, t == nblk - 1))
    for h in range(C_HEADS):
        for part, ref, scale in ((0, q_ref, C_HEAD_DIM ** -0.5), (1, k_ref, 1.0)):
            a = y[:, part * C_WIDTH + h * LANES:part * C_WIDTH + (h + 1) * LANES]
            a = a * lax.rsqrt(jnp.sum(a * a, axis=-1, keepdims=True) + EPS)
            ref[:, h * LANES:(h + 1) * LANES] = a * scale
    v_ref[...] = y[:, 2 * C_WIDTH:]


def _gdn_prep(qkv, cw, B, S, tm=256):
    M = qkv.shape[0]
    nblk = S // tm
    r = tm // HALO
    nrb = M // HALO
    W = 3 * C_WIDTH
    blk = lambda b, t: b * nblk + t
    return pl.pallas_call(
        functools.partial(_gdn_prep_kernel, nblk=nblk),
        grid=(B, nblk),
        in_specs=[pl.BlockSpec((HALO, W), lambda b, t: (jnp.maximum(blk(b, t) * r - 1, 0), 0)),
                  pl.BlockSpec((tm, W), lambda b, t: (blk(b, t), 0)),
                  pl.BlockSpec((HALO, W), lambda b, t: (jnp.minimum((blk(b, t) + 1) * r, nrb - 1), 0)),
                  pl.BlockSpec((4, W), lambda b, t: (0, 0))],
        out_specs=[pl.BlockSpec((tm, C_WIDTH), lambda b, t: (blk(b, t), 0))] * 3,
        out_shape=[jax.ShapeDtypeStruct((M, C_WIDTH), F32)] * 3,
        compiler_params=_cparams("parallel", "parallel"),
        name="gdn_prep",
    )(qkv, qkv, qkv, cw)


def _nt_dot(a, b):
    return lax.dot_general(a, b, (((1,), (1,)), ((), ())), preferred_element_type=F32)


def _gdn_kernel(qf_ref, kf_ref, vf_ref, gf_ref, qr_ref, kr_ref, vr_ref, gr_ref, of_ref, or_ref, state, *, tm):
    C = C_CHUNK
    SB = LANES
    nsb = tm // SB
    nc = tm // C
    t = pl.program_id(1)

    @pl.when(t == 0)
    def _():
        state[...] = jnp.zeros_like(state)

    ri = lax.broadcasted_iota(jnp.int32, (SB, SB), 0)
    ci = lax.broadcasted_iota(jnp.int32, (SB, SB), 1)
    same = (ri // C) == (ci // C)
    eye = jnp.where(ri == ci, 1.0, 0.0)
    incl, strict, levels = [], [], []
    for d in range(2):
        hi_i, lo_i = (ci, ri) if d else (ri, ci)
        incl.append(same & (hi_i >= lo_i))
        strict.append(same & (hi_i > lo_i))
        lv = []
        s = 1
        while s < C:
            lv.append((((hi_i // s) % 2) == 1) & ((lo_i // s) == (hi_i // s) - 1))
            s *= 2
        levels.append(lv)

    pos = lax.broadcasted_iota(jnp.int32, (4 * C_HEADS, tm), 1) % C
    rows, cols = [], []
    for d, g_ref in enumerate((gf_ref, gr_ref)):
        gt = g_ref[...]
        x = gt
        s = 1
        while s < C:
            if d == 0:
                x = x + jnp.where(pos >= s, pltpu.roll(x, s, 1), 0.0)
            else:
                x = x + jnp.where(pos < C - s, pltpu.roll(x, tm - s, 1), 0.0)
            s *= 2
        lo = d * C_HEADS
        rw = jnp.concatenate([gt[lo:lo + C_HEADS], x[2 * C_HEADS + lo:2 * C_HEADS + lo + C_HEADS]], axis=0)
        rows.append(rw)
        cols.append(jnp.transpose(jnp.concatenate([rw, jnp.zeros((LANES - 2 * C_HEADS, tm), F32)], axis=0)))

    refs = ((qf_ref, kf_ref, vf_ref, of_ref), (qr_ref, kr_ref, vr_ref, or_ref))
    chains = []
    for d in range(2):
        for h in range(C_HEADS):
            for sb in range(nsb):
                rs = slice(sb * SB, (sb + 1) * SB)
                sl = slice(h * LANES, (h + 1) * LANES)
                q = refs[d][0][rs, sl]
                k = refs[d][1][rs, sl]
                v = refs[d][2][rs, sl]
                beta = jnp.broadcast_to(cols[d][rs, h:h + 1], (SB, LANES))
                gcc = jnp.broadcast_to(cols[d][rs, C_HEADS + h:C_HEADS + h + 1], (SB, LANES))
                gcr = jnp.broadcast_to(rows[d][C_HEADS + h:C_HEADS + h + 1, rs], (SB, SB))
                dec = jnp.exp(jnp.where(incl[d], gcc - gcr, NEG_INF))
                eg = jnp.exp(gcc)
                kbf = k.astype(BF16)
                kb = k * beta
                lmat = jnp.where(strict[d], _nt_dot(kb.astype(BF16), kbf) * dec, 0.0)
                amat = jnp.where(incl[d], _nt_dot(q.astype(BF16), kbf) * dec, 0.0).astype(BF16)
                chains.append(dict(d=d, h=h, sb=sb, k=k, gcc=gcc, lmat=lmat, amat=amat,
                                   rhs=jnp.concatenate([v * beta, kb * eg], axis=1).astype(BF16),
                                   qe=(q * eg).astype(BF16)))

    for ch in chains:
        ch['x'] = eye - jnp.where(levels[ch['d']][0], ch['lmat'], 0.0)
    for li in range(1, len(levels[0])):
        for ch in chains:
            ch['xb'] = ch['x'].astype(BF16)
            cmat = jnp.where(levels[ch['d']][li], ch['lmat'], 0.0).astype(BF16)
            ch['cx'] = jnp.dot(cmat, ch['xb'], preferred_element_type=F32).astype(BF16)
        for ch in chains:
            ch['x'] = ch['x'] - jnp.dot(ch['xb'], ch['cx'], preferred_element_type=F32)
    for ch in chains:
        sol = jnp.dot(ch['x'].astype(BF16), ch['rhs'], preferred_element_type=F32)
        ch['u'] = sol[:, :LANES]
        ch['w'] = sol[:, LANES:].astype(BF16)
        ch['vn'] = [None] * (SB // C)
        ch['ost'] = [None] * (SB // C)

    by_key = {(ch['d'], ch['h'], ch['sb']): ch for ch in chains}
    st = {(d, h): state[d, h] for d in range(2) for h in range(C_HEADS)}
    for step in range(nc):
        work = []
        for (d, h), s_val in st.items():
            c = nc - 1 - step if d else step
            ch = by_key[(d, h, c // (SB // C))]
            lc = c % (SB // C)
            rs = slice(lc * C, (lc + 1) * C)
            stb = s_val.astype(BF16)
            vn = ch['u'][rs] - jnp.dot(ch['w'][rs], stb, preferred_element_type=F32)
            ch['ost'][lc] = jnp.dot(ch['qe'][rs], stb, preferred_element_type=F32)
            ch['vn'][lc] = vn
            last = lc * C if d else (lc + 1) * C - 1
            gl = ch['gcc'][last:last + 1, :]
            kd = (ch['k'][rs] * jnp.exp(gl - ch['gcc'][rs])).astype(BF16)
            work.append(((d, h), s_val * jnp.exp(gl), kd, vn.astype(BF16)))
        for key, decayed, kd, vnb in work:
            st[key] = decayed + lax.dot_general(kd, vnb, (((0,), (0,)), ((), ())), preferred_element_type=F32)
    for (d, h), s_val in st.items():
        state[d, h] = s_val
    for ch in chains:
        rs = slice(ch['sb'] * SB, (ch['sb'] + 1) * SB)
        sl = slice(ch['h'] * LANES, (ch['h'] + 1) * LANES)
        vn_all = jnp.concatenate(ch['vn'], axis=0).astype(BF16)
        refs[ch['d']][3][rs, sl] = (jnp.concatenate(ch['ost'], axis=0)
                                   + jnp.dot(ch['amat'], vn_all, preferred_element_type=F32))


def _gdn(q, k, v, gbt, B, S, tm=256):
    M = q.shape[0]
    nblk = S // tm
    fwd = lambda b, t: b * nblk + t
    rev = lambda b, t: b * nblk + (nblk - 1 - t)
    specs = []
    for blk in (fwd, rev):
        specs += [pl.BlockSpec((tm, C_WIDTH), lambda b, t, blk=blk: (blk(b, t), 0))] * 3
        specs += [pl.BlockSpec((4 * C_HEADS, tm), lambda b, t, blk=blk: (0, blk(b, t)))]
    return pl.pallas_call(
        functools.partial(_gdn_kernel, tm=tm),
        grid=(B, nblk),
        in_specs=specs,
        out_specs=[pl.BlockSpec((tm, C_WIDTH), lambda b, t: (fwd(b, t), 0)),
                   pl.BlockSpec((tm, C_WIDTH), lambda b, t: (rev(b, t), 0))],
        out_shape=[jax.ShapeDtypeStruct((M, C_WIDTH), F32)] * 2,
        scratch_shapes=[pltpu.VMEM((2, C_HEADS, C_HEAD_DIM, C_HEAD_DIM), F32)],
        compiler_params=_cparams("arbitrary", "arbitrary"),
        name="gdn",
    )(q, k, v, gbt, q, k, v, gbt)


def _band_kernel(q_ref, kp_ref, kc_ref, kn_ref, vp_ref, vc_ref, vn_ref, bias_ref, o_ref, lse_ref, *, tq, m_len):
    t = pl.program_id(2)
    QT = BAND_QT
    nkeys = QT + 2 * D_STEPS
    kwin = jnp.concatenate([kp_ref[...], kc_ref[...], kn_ref[...]], axis=0)
    vwin = jnp.concatenate([vp_ref[...], vc_ref[...], vn_ref[...]], axis=0)
    lane = lax.broadcasted_iota(jnp.int32, (QT, D_GW), 1)
    mine = [(lane // D_HEAD_DIM) == h for h in range(D_HEADS_PER_GROUP)]
    col = lax.broadcasted_iota(jnp.int32, (QT, nkeys), 1)
    tiles = []
    for i in range(tq // QT):
        q = q_ref[i * QT:(i + 1) * QT, :]
        kpos = t * tq + i * QT - D_STEPS + col
        valid = (kpos >= 0) & (kpos < m_len)
        kw = kwin[i * QT:i * QT + nkeys]
        for h in range(D_HEADS_PER_GROUP):
            qh = jnp.where(mine[h], q, jnp.zeros_like(q))
            s = lax.dot_general(qh, kw, (((1,), (1,)), ((), ())), preferred_element_type=F32) + bias_ref[h]
            tiles.append([i, h, jnp.where(valid, s, NEG_INF)])
    for tile in tiles:
        s = tile[2]
        m = jnp.max(s, axis=-1, keepdims=True)
        e = jnp.exp(s - m)
        l = jnp.sum(e, axis=-1, keepdims=True)
        tile[2] = (e / l).astype(BF16)
        tile.append(m + jnp.log(l))
    for i in range(tq // QT):
        o = jnp.zeros((QT, D_GW), F32)
        lse_full = jnp.zeros((QT, D_GW), F32)
        vw = vwin[i * QT:i * QT + nkeys]
        for _, h, p, lse in tiles[i * D_HEADS_PER_GROUP:(i + 1) * D_HEADS_PER_GROUP]:
            o = jnp.where(mine[h], jnp.dot(p, vw, preferred_element_type=F32), o)
            lse_full = jnp.where(mine[h], lse, lse_full)
        o_ref[i * QT:(i + 1) * QT, :] = o
        lse_ref[i * QT:(i + 1) * QT, :] = lse_full


def _band(dq, dk, dv, bias, gi, dil, B, S):
    M = B * S
    m_len = S // dil
    tq = min(BAND_TQ, m_len)
    nq = m_len // tq
    ngrp = len(D_GROUPS)
    hr = tq // D_STEPS
    nhb = (B * m_len) // D_STEPS
    view = lambda a: a.reshape(B * m_len, dil * ngrp * D_GW)
    cur = lambda b, r, t: (b * nq + t, r * ngrp + gi)
    prev = lambda b, r, t: (jnp.maximum((b * nq + t) * hr - 1, 0), r * ngrp + gi)
    nxt = lambda b, r, t: (jnp.minimum((b * nq + t + 1) * hr, nhb - 1), r * ngrp + gi)
    halo = lambda f: pl.BlockSpec((D_STEPS, D_GW), f)
    full = pl.BlockSpec((tq, D_GW), cur)
    out = pl.BlockSpec((tq, D_GW), lambda b, r, t: (b * nq + t, r))
    q2, k2, v2 = view(dq), view(dk), view(dv)
    o, lse = pl.pallas_call(
        functools.partial(_band_kernel, tq=tq, m_len=m_len),
        grid=(B, dil, nq),
        in_specs=[full, halo(prev), full, halo(nxt), halo(prev), full, halo(nxt),
                  pl.BlockSpec((D_HEADS_PER_GROUP, BAND_QT, BAND_QT + 2 * D_STEPS), lambda b, r, t: (0, 0, 0))],
        out_specs=[out, out],
        out_shape=[jax.ShapeDtypeStruct((B * m_len, dil * D_GW), F32)] * 2,
        compiler_params=_cparams("parallel", "parallel", "parallel"),
        name="band_attn",
    )(q2, k2, k2, k2, v2, v2, v2, bias)
    return o.reshape(M, D_GW), lse.reshape(M, D_GW)


def _odd_out_kernel(x_ref, of_ref, or_ref, z_ref, cg_ref, o0_ref, o1_ref, o2_ref, l0_ref, l1_ref, l2_ref,
                    wc_ref, wd_ref, *rest, final):
    *ff, out_ref = rest
    oc = of_ref[...] + or_ref[...]
    z = z_ref[...]
    parts = []
    for h in range(C_HEADS):
        sl = slice(h * LANES, (h + 1) * LANES)
        parts.append(_rms(oc[:, sl], cg_ref[...]) * _silu(z[:, sl]))
    yc = jnp.concatenate(parts, axis=1).astype(BF16)
    l0, l1, l2 = l0_ref[...], l1_ref[...], l2_ref[...]
    m = jnp.maximum(jnp.maximum(l0, l1), l2)
    e0, e1, e2 = jnp.exp(l0 - m), jnp.exp(l1 - m), jnp.exp(l2 - m)
    den = e0 + e1 + e2
    yd = ((e0 / den) * o0_ref[...] + (e1 / den) * o1_ref[...] + (e2 / den) * o2_ref[...]).astype(BF16)
    y = jnp.dot(yc, wc_ref[...], preferred_element_type=F32) + jnp.dot(yd, wd_ref[...], preferred_element_type=F32)
    out_ref[...] = _mlp_tail(x_ref[...] + y, ff, final)


def _odd_out(x, o_f, o_r, z, cg, os_, ls_, wc, wd, ff, final, tm=512):
    M = x.shape[0]
    row = lambda i: (i, 0)
    wide = pl.BlockSpec((tm, C_WIDTH), row)
    grp = pl.BlockSpec((tm, D_GW), row)
    return pl.pallas_call(
        functools.partial(_odd_out_kernel, final=final),
        grid=(M // tm,),
        in_specs=[pl.BlockSpec((tm, D_MODEL), row), wide, wide, wide, _resident((1, LANES)),
                  grp, grp, grp, grp, grp, grp,
                  _resident((C_WIDTH, D_MODEL)), _resident((D_GW, D_MODEL))] + _ff_specs(),
        out_specs=pl.BlockSpec((tm, D_MODEL), row),
        out_shape=jax.ShapeDtypeStruct((M, D_MODEL), F32),
        compiler_params=_cparams("parallel"),
        name="odd_out_mlp",
    )(x, o_f, o_r, z, cg, *os_, *ls_, wc, wd, *ff)


def _t5_bucket(rel):
    nb = N_BUCKETS // 2
    max_exact = nb // 2
    n = np.abs(rel)
    large = max_exact + (np.log(np.maximum(n, 1) / max_exact) / math.log(MAX_DISTANCE / max_exact)
                         * (nb - max_exact)).astype(np.int64)
    large = np.minimum(large, nb - 1)
    return (np.where(rel > 0, nb, 0) + np.where(n < max_exact, n, large)).astype(np.int32)


def _band_bias(rel_bias, tq=128):
    i = np.arange(tq)[:, None]
    j = np.arange(tq + 2 * D_STEPS)[None, :]
    delta = j - D_STEPS - i
    in_band = np.abs(delta) <= D_STEPS
    idx = np.clip(delta + D_STEPS, 0, 2 * D_STEPS)
    tiles = []
    for gi, (window, dil) in enumerate(D_GROUPS):
        steps = window // (2 * dil)
        assert steps == D_STEPS
        buckets = _t5_bucket(np.arange(-steps, steps + 1) * dil)
        b = rel_bias.astype(F32)[jnp.asarray(buckets)][:, gi * D_HEADS_PER_GROUP:(gi + 1) * D_HEADS_PER_GROUP].T
        tiles.append(jnp.where(jnp.asarray(in_band)[None], b[:, jnp.asarray(idx)], NEG_INF))
    return jnp.stack(tiles)


def _prep_odd(w_in, w_out, conv_w, a_log, dt_bias, o_gain, rel_bias):
    n_c = 4 * C_WIDTH
    n_ba = 4 * C_HEADS
    w = jnp.concatenate([w_in[:, :n_c], w_in[:, n_c + n_ba:], w_in[:, n_c:n_c + n_ba],
                         jnp.zeros((D_MODEL, LANES - n_ba), w_in.dtype)], axis=1).astype(BF16)
    pad = jnp.zeros((LANES - n_ba,), F32)
    nega = jnp.concatenate([jnp.zeros((2 * C_HEADS,), F32), -jnp.exp(a_log.astype(F32)).reshape(-1), pad])[None]
    dtb = jnp.concatenate([jnp.zeros((2 * C_HEADS,), F32), dt_bias.astype(F32).reshape(-1), pad])[None]
    return dict(w=w, nega=nega, dtb=dtb, cw=conv_w.astype(F32), cg=o_gain.astype(F32)[None],
                bias=_band_bias(rel_bias), wc=w_out[:C_WIDTH].astype(BF16), wd=w_out[C_WIDTH:].astype(BF16))


def _odd_layer(x, gain, p, ff, final, B, S):
    qkv, z, dq, dk, dv, gb = _odd_proj(x, gain, p['w'], p['nega'], p['dtb'])
    cq, ck, cv = _gdn_prep(qkv, p['cw'], B, S)
    o_f, o_r = _gdn(cq, ck, cv, gb, B, S)
    outs, lses = [], []
    for gi, (_, dil) in enumerate(D_GROUPS):
        o, lse = _band(dq, dk, dv, p['bias'][gi], gi, dil, B, S)
        outs.append(o)
        lses.append(lse)
    return _odd_out(x, o_f, o_r, z, p['cg'], outs, lses, p['wc'], p['wd'], ff, final)


def _trunk(x3, evens, odds, norm_mix, norm_ff, norm_final, w1, w2):
    B, S, _ = x3.shape
    x = x3.reshape(B * S, D_MODEL)
    cos, sin = _rope_tables(S)
    for layer in range(DEPTH):
        gain = norm_mix[layer][None]
        ff = (norm_ff[layer][None], w1[layer], w2[layer], norm_final[None])
        final = layer == DEPTH - 1
        if layer % 2 == 0:
            x = _even_layer(x, gain, evens[layer // 2], ff, final, cos, sin, B, S)
        else:
            x = _odd_layer(x, gain, odds[layer // 2], ff, final, B, S)
    return x.reshape(B, S, D_MODEL)


def kernel(x_prompt, x_sample, rel_bias, norm_mix, norm_ff, norm_final, w_ff1, w_ff2, w_in_e, w_out_e,
           a_qnorm, a_knorm, b_conv_w, b_conv_b, b_wr, b_br, b_wi, b_bi, b_lambda, w_in_o, w_out_o,
           c_conv_w, c_a_log, c_dt_bias, c_norm):
    evens = [_prep_even(w_in_e[j], w_out_e[j], a_qnorm[j], a_knorm[j], b_conv_w[j], b_conv_b[j],
                        b_wr[j], b_br[j], b_wi[j], b_bi[j], b_lambda[j]) for j in range((DEPTH + 1) // 2)]
    odds = [_prep_odd(w_in_o[j], w_out_o[j], c_conv_w[j], c_a_log[j], c_dt_bias[j], c_norm[j], rel_bias)
            for j in range(DEPTH // 2)]
    w1 = w_ff1.astype(BF16)
    w2 = w_ff2.astype(BF16)
    nm = norm_mix.astype(F32)
    nf = norm_ff.astype(F32)
    ng = norm_final.astype(F32)
    y_prompt = _trunk(x_prompt, evens, odds, nm, nf, ng, w1, w2)
    y_sample = _trunk(x_sample, evens, odds, nm, nf, ng, w1, w2)
    return (y_prompt, y_sample)
```

```python
import functools
import math

import numpy as np
import jax
import jax.numpy as jnp
from jax import lax
from jax.experimental import pallas as pl
from jax.experimental.pallas import tpu as pltpu

F32 = jnp.float32
BF16 = jnp.bfloat16

D_MODEL = 1024
D_FF = 4 * D_MODEL
DEPTH = 4
EPS = 1e-6
NEG_INF = -1e30
GRID_W = 64
LANES = 128

A_HEADS = 8
A_KV_HEADS = 2
A_HEAD_DIM = 64
A_Q = A_HEADS * A_HEAD_DIM
A_KV = A_KV_HEADS * A_HEAD_DIM
ROPE_THETA = 10000.0
B_WIDTH = 512
B_BLOCKS = 8
B_BLOCK_DIM = B_WIDTH // B_BLOCKS
RG_C = 8.0
C_HEADS = 4
C_HEAD_DIM = 128
C_WIDTH = C_HEADS * C_HEAD_DIM
C_CHUNK = 64
D_GROUPS = ((128, 1), (512, 4), (2048, 16))
D_HEADS_PER_GROUP = 4
D_HEAD_DIM = 64
D_NHEADS = len(D_GROUPS) * D_HEADS_PER_GROUP
D_WIDTH = D_NHEADS * D_HEAD_DIM
D_GW = D_HEADS_PER_GROUP * D_HEAD_DIM
D_STEPS = 64
N_BUCKETS = 32
MAX_DISTANCE = 1024
HALO = 8
Q_SCALE = A_HEAD_DIM ** -0.5 * math.log2(math.e)
MLP_TF = 1024
PERM_TM = 512
BAND_QT = 128
BAND_TQ = 512
VT_PAD = 16
VT_ROWS = A_KV_HEADS * (A_HEAD_DIM + VT_PAD)

VMEM_LIMIT = 56 * 1024 * 1024


def _cparams(*sem):
    return pltpu.CompilerParams(dimension_semantics=sem, vmem_limit_bytes=VMEM_LIMIT)


def _rms(x, gain):
    return x * lax.rsqrt(jnp.mean(x * x, axis=-1, keepdims=True) + EPS) * gain


def _softplus(x):
    return jnp.maximum(x, 0.0) + jnp.log1p(jnp.exp(-jnp.abs(x)))


def _sigmoid(x):
    return 0.5 * jnp.tanh(0.5 * x) + 0.5


def _silu(x):
    return x * _sigmoid(x)


def _gelu_tanh(x):
    return 0.5 * x * (1.0 + jnp.tanh(math.sqrt(2.0 / math.pi) * (x + 0.044715 * (x * x * x))))


def _conv4(prev, cur, nxt, w, first, last):
    rows = cur.shape[0]
    prev = jnp.where(first, 0.0, prev)
    nxt = jnp.where(last, 0.0, nxt)
    full = jnp.concatenate([prev, cur, nxt], axis=0)
    y = full[HALO - 2:HALO - 2 + rows] * w[0:1]
    for j in range(1, 4):
        y = y + full[HALO - 2 + j:HALO - 2 + j + rows] * w[j:j + 1]
    return y


def _even_proj_kernel(x_ref, g_ref, w_ref, wvt_ref, qg_ref, kg_ref, cos_ref, sin_ref,
                      q_ref, k_ref, vt_ref, xr_ref, gr_ref):
    xn = _rms(x_ref[...], g_ref[...]).astype(BF16)
    proj = jnp.dot(xn, w_ref[...], preferred_element_type=F32)
    vt = lax.dot_general(wvt_ref[...], xn, (((1,), (1,)), ((), ())), preferred_element_type=F32).astype(BF16)
    ones = jnp.ones((VT_PAD, LANES), BF16)
    for c in range(vt_ref.shape[0]):
        cols = slice(c * LANES, (c + 1) * LANES)
        vt_ref[c] = jnp.concatenate([vt[:A_HEAD_DIM, cols], ones, vt[A_HEAD_DIM:, cols], ones], axis=0)
    cos = cos_ref[...]
    sin = sin_ref[...]
    lane = lax.broadcasted_iota(jnp.int32, cos.shape, 1)
    first_half = (lane % 32) < 16
    lo = lane < A_HEAD_DIM

    def rope(t):
        swapped = jnp.where(first_half, pltpu.roll(t, LANES - 16, 1), pltpu.roll(t, 16, 1))
        return t * cos + swapped * sin

    for h in range(A_HEADS):
        t = proj[:, h * LANES:(h + 1) * LANES]
        r = lax.rsqrt(jnp.sum(t * t, axis=-1, keepdims=True) * (1.0 / A_HEAD_DIM) + EPS)
        t = rope(t * r * qg_ref[:, h * LANES:(h + 1) * LANES])
        q_ref[:, h * LANES:(h + 1) * LANES] = (t * Q_SCALE).astype(BF16)
    off = A_HEADS * LANES
    t = proj[:, off:off + LANES]
    t2 = t * t
    s_lo = jnp.sum(jnp.where(lo, t2, 0.0), axis=-1, keepdims=True)
    s_hi = jnp.sum(jnp.where(lo, 0.0, t2), axis=-1, keepdims=True)
    r = lax.rsqrt(jnp.where(lo, s_lo, s_hi) * (1.0 / A_HEAD_DIM) + EPS)
    k_ref[...] = rope(t * r * kg_ref[...]).astype(BF16)
    off += LANES
    xr_ref[...] = proj[:, off:off + B_WIDTH]
    off += B_WIDTH
    gr_ref[...] = proj[:, off:off + B_WIDTH]


def _even_proj(x, gain, w, wvt, qg, kg, cos, sin, S, tm=512):
    M = x.shape[0]
    nS = S // tm
    n_in = w.shape[1]
    row = lambda i: (i, 0)
    const = lambda i: (0, 0)
    return pl.pallas_call(
        _even_proj_kernel,
        grid=(M // tm,),
        in_specs=[pl.BlockSpec((tm, D_MODEL), row), pl.BlockSpec((1, D_MODEL), const),
                  pl.BlockSpec((D_MODEL, n_in), const), pl.BlockSpec((A_KV, D_MODEL), const),
                  pl.BlockSpec((1, A_HEADS * LANES), const), pl.BlockSpec((1, LANES), const),
                  pl.BlockSpec((tm, LANES), lambda i: (i % nS, 0)),
                  pl.BlockSpec((tm, LANES), lambda i: (i % nS, 0))],
        out_specs=[pl.BlockSpec((tm, A_HEADS * LANES), row), pl.BlockSpec((tm, LANES), row),
                   pl.BlockSpec((tm // LANES, VT_ROWS, LANES), lambda i: (i, 0, 0)),
                   pl.BlockSpec((tm, B_WIDTH), row), pl.BlockSpec((tm, B_WIDTH), row)],
        out_shape=[jax.ShapeDtypeStruct((M, A_HEADS * LANES), BF16),
                   jax.ShapeDtypeStruct((M, LANES), BF16),
                   jax.ShapeDtypeStruct((M // LANES, VT_ROWS, LANES), BF16),
                   jax.ShapeDtypeStruct((M, B_WIDTH), F32), jax.ShapeDtypeStruct((M, B_WIDTH), F32)],
        compiler_params=_cparams("parallel"),
        name="even_proj",
    )(x, gain, w, wvt, qg, kg, cos, sin)


def _gqa_kernel(q_ref, k_ref, vt_ref, o_ref, m_ref, acc_ref, *, tq, kstep, nstep, unroll, lookahead):
    ntile = A_HEADS // 2
    qt = [jnp.concatenate([q_ref[:, (2 * p) * LANES:(2 * p + 1) * LANES],
                           q_ref[:, (2 * p + 1) * LANES:(2 * p + 2) * LANES]], axis=0) for p in range(ntile)]
    m_ref[...] = jnp.full(m_ref.shape, NEG_INF, F32)
    acc_ref[...] = jnp.zeros(acc_ref.shape, F32)
    vrows = VT_ROWS // A_KV_HEADS

    per = kstep // LANES

    def scores(j, u, p):
        ks = k_ref[pl.ds(pl.multiple_of((j * unroll + u) * kstep, kstep), kstep), :]
        return lax.dot_general(ks, qt[p], (((1,), (1,)), ((), ())), preferred_element_type=F32)

    def update(j, u, p, st):
        g = (2 * p) // (A_HEADS // A_KV_HEADS)
        m_old = m_ref[p]
        m_new = jnp.maximum(m_old, jnp.max(st, axis=0, keepdims=True))
        alpha = jnp.exp2(m_old - m_new)
        pt = jnp.exp2(st - m_new).astype(BF16)
        first = (j * unroll + u) * per
        vts = jnp.concatenate([vt_ref[first + i, g * vrows:(g + 1) * vrows, :] for i in range(per)], axis=1)
        acc_ref[p] = alpha * acc_ref[p] + jnp.dot(vts, pt, preferred_element_type=F32)
        m_ref[p] = m_new

    pairs = [(u, p) for u in range(unroll) for p in range(ntile)]
    niter = nstep // unroll

    def body(j, pending):
        pending = list(pending)
        j_next = jnp.minimum(j + 1, niter - 1)
        for idx in range(len(pairs)):
            ahead = idx + lookahead
            if ahead < len(pairs):
                pending.append(scores(j, *pairs[ahead]))
            else:
                pending.append(scores(j_next, *pairs[ahead - len(pairs)]))
            update(j, *pairs[idx], pending.pop(0))
        return tuple(pending)

    lax.fori_loop(0, niter, body, tuple(scores(0, *pairs[i]) for i in range(lookahead)))
    for p in range(ntile):
        acc = acc_ref[p]
        o = acc[:A_HEAD_DIM] / acc[A_HEAD_DIM:A_HEAD_DIM + 1]
        both = jnp.concatenate([o[:, :tq], o[:, tq:]], axis=0)
        o_ref[:, p * LANES:(p + 1) * LANES] = jnp.transpose(both).astype(o_ref.dtype)


def _gqa(q, k, vt, B, S, tq=128, kstep=256, unroll=8, lookahead=5):
    M = q.shape[0]
    nq = S // tq
    nslab = S // LANES
    nstep = S // kstep
    unroll = min(unroll, nstep)
    kern = functools.partial(_gqa_kernel, tq=tq, kstep=kstep, nstep=nstep, unroll=unroll, lookahead=lookahead)
    return pl.pallas_call(
        kern,
        grid=(B, nq),
        in_specs=[pl.BlockSpec((tq, A_HEADS * LANES), lambda b, i: (b * nq + i, 0)),
                  pl.BlockSpec((S, LANES), lambda b, i: (b, 0)),
                  pl.BlockSpec((nslab, VT_ROWS, LANES), lambda b, i: (b, 0, 0))],
        out_specs=pl.BlockSpec((tq, A_Q), lambda b, i: (b * nq + i, 0)),
        out_shape=jax.ShapeDtypeStruct((M, A_Q), BF16),
        scratch_shapes=[pltpu.VMEM((A_HEADS // 2, 1, 2 * tq), F32),
                        pltpu.VMEM((A_HEADS // 2, VT_ROWS // A_KV_HEADS, 2 * tq), F32)],
        compiler_params=_cparams("parallel", "parallel"),
        name="gqa",
    )(q, k, vt)


def _rglru_kernel(xp_ref, xc_ref, xn_ref, yp_ref, yc_ref, yn_ref, cw_ref, cb_ref, wg_ref, bg_ref, lam_ref,
                  hf_ref, hr_ref, a_s, u_s, carry_s, *, tm, nblk):
    t = pl.program_id(1)

    @pl.when(t == 0)
    def _():
        carry_s[...] = jnp.zeros_like(carry_s)

    cw = cw_ref[...]
    ngrp = tm // 8
    row8 = lax.broadcasted_iota(jnp.int32, (ngrp, 8, B_WIDTH), 1)
    for d, (p_ref, c_ref, n_ref) in enumerate(((xp_ref, xc_ref, xn_ref), (yp_ref, yc_ref, yn_ref))):
        tt = t if d == 0 else nblk - 1 - t
        xc = _conv4(p_ref[...], c_ref[...], n_ref[...], cw, tt == 0, tt == nblk - 1) + cb_ref[...]
        xb = xc.astype(BF16)
        r = _sigmoid(jnp.dot(xb, wg_ref[2 * d], preferred_element_type=F32) + bg_ref[2 * d:2 * d + 1])
        i = _sigmoid(jnp.dot(xb, wg_ref[2 * d + 1], preferred_element_type=F32) + bg_ref[2 * d + 1:2 * d + 2])
        log_a = (-RG_C) * r * _softplus(-lam_ref[d:d + 1])
        a = jnp.exp(log_a).reshape(ngrp, 8, B_WIDTH)
        u = (jnp.sqrt(1.0 - jnp.exp(2.0 * log_a)) * (i * xc)).reshape(ngrp, 8, B_WIDTH)
        for sh in (1, 2, 4):
            keep = (row8 >= sh) if d == 0 else (row8 < 8 - sh)
            shift = sh if d == 0 else 8 - sh
            u = u + a * jnp.where(keep, pltpu.roll(u, shift, 1), 0.0)
            a = a * jnp.where(keep, pltpu.roll(a, shift, 1), 1.0)
        a_s[d] = a.reshape(tm, B_WIDTH)
        u_s[d] = u.reshape(tm, B_WIDTH)

    def body(gidx, carry):
        out = []
        for d, out_ref in enumerate((hf_ref, hr_ref)):
            gg = gidx if d == 0 else ngrp - 1 - gidx
            start = pl.multiple_of(gg * 8, 8)
            hg = u_s[d, pl.ds(start, 8), :] + a_s[d, pl.ds(start, 8), :] * carry[d]
            out_ref[pl.ds(start, 8), :] = hg
            edge = hg[7:8, :] if d == 0 else hg[0:1, :]
            out.append(jnp.broadcast_to(edge, (8, B_WIDTH)))
        return tuple(out)

    ends = lax.fori_loop(0, ngrp, body, (carry_s[0], carry_s[1]), unroll=4)
    carry_s[0] = ends[0]
    carry_s[1] = ends[1]


def _rglru(xr, cw, cb, wg, bg, lam, B, S, tm=256):
    M = xr.shape[0]
    nblk = S // tm
    r = tm // HALO
    nrb = M // HALO
    fwd = lambda b, t: b * nblk + t
    rev = lambda b, t: b * nblk + (nblk - 1 - t)
    specs = []
    for blk in (fwd, rev):
        specs += [pl.BlockSpec((HALO, B_WIDTH), lambda b, t, blk=blk: (jnp.maximum(blk(b, t) * r - 1, 0), 0)),
                  pl.BlockSpec((tm, B_WIDTH), lambda b, t, blk=blk: (blk(b, t), 0)),
                  pl.BlockSpec((HALO, B_WIDTH), lambda b, t, blk=blk: (jnp.minimum((blk(b, t) + 1) * r, nrb - 1), 0))]
    const2 = lambda b, t: (0, 0)
    const3 = lambda b, t: (0, 0, 0)
    kern = functools.partial(_rglru_kernel, tm=tm, nblk=nblk)
    return pl.pallas_call(
        kern,
        grid=(B, nblk),
        in_specs=specs + [pl.BlockSpec((4, B_WIDTH), const2), pl.BlockSpec((1, B_WIDTH), const2),
                          pl.BlockSpec((4, B_WIDTH, B_WIDTH), const3), pl.BlockSpec((4, B_WIDTH), const2),
                          pl.BlockSpec((2, B_WIDTH), const2)],
        out_specs=[pl.BlockSpec((tm, B_WIDTH), lambda b, t: (fwd(b, t), 0)),
                   pl.BlockSpec((tm, B_WIDTH), lambda b, t: (rev(b, t), 0))],
        out_shape=[jax.ShapeDtypeStruct((M, B_WIDTH), F32), jax.ShapeDtypeStruct((M, B_WIDTH), F32)],
        scratch_shapes=[pltpu.VMEM((2, tm, B_WIDTH), F32), pltpu.VMEM((2, tm, B_WIDTH), F32),
                        pltpu.VMEM((2, 8, B_WIDTH), F32)],
        compiler_params=_cparams("arbitrary", "arbitrary"),
        name="rglru",
    )(xr, xr, xr, xr, xr, xr, cw, cb, wg, bg, lam)


def _mlp_tail(x, ff, final):
    gff_ref, w1_ref, w2_ref, gfin_ref = ff
    xn = _rms(x, gff_ref[...]).astype(BF16)
    acc = x
    for c in range(D_FF // MLP_TF):
        h = jnp.dot(xn, w1_ref[:, c * MLP_TF:(c + 1) * MLP_TF], preferred_element_type=F32)
        h = jnp.square(jnp.maximum(h, 0.0)).astype(BF16)
        acc = acc + jnp.dot(h, w2_ref[c * MLP_TF:(c + 1) * MLP_TF, :], preferred_element_type=F32)
    return _rms(acc, gfin_ref[...]) if final else acc


def _resident(shape):
    return pl.BlockSpec(shape, lambda *_: (0,) * len(shape), pipeline_mode=pl.Buffered(1))


def _ff_specs():
    return [_resident((1, D_MODEL)), _resident((D_MODEL, D_FF)), _resident((D_FF, D_MODEL)),
            _resident((1, D_MODEL))]


def _even_out_kernel(x_ref, ya_ref, hf_ref, hr_ref, gr_ref, wa_ref, wb_ref, *rest, final):
    *ff, o_ref = rest
    yb = ((hf_ref[...] + hr_ref[...]) * _gelu_tanh(gr_ref[...])).astype(BF16)
    y = jnp.dot(ya_ref[...], wa_ref[...], preferred_element_type=F32)
    y = y + jnp.dot(yb, wb_ref[...], preferred_element_type=F32)
    o_ref[...] = _mlp_tail(x_ref[...] + y, ff, final)


def _even_out(x, ya, hf, hr, gr, wa, wb, ff, final, tm=512):
    M = x.shape[0]
    row = lambda i: (i, 0)
    return pl.pallas_call(
        functools.partial(_even_out_kernel, final=final),
        grid=(M // tm,),
        in_specs=[pl.BlockSpec((tm, D_MODEL), row), pl.BlockSpec((tm, A_Q), row),
                  pl.BlockSpec((tm, B_WIDTH), row), pl.BlockSpec((tm, B_WIDTH), row),
                  pl.BlockSpec((tm, B_WIDTH), row),
                  _resident((A_Q, D_MODEL)), _resident((B_WIDTH, D_MODEL))] + _ff_specs(),
        out_specs=pl.BlockSpec((tm, D_MODEL), row),
        out_shape=jax.ShapeDtypeStruct((M, D_MODEL), F32),
        compiler_params=_cparams("parallel"),
        name="even_out_mlp",
    )(x, ya, hf, hr, gr, wa, wb, *ff)


def _rope_tables(S):
    rows = S // GRID_W
    row = jnp.repeat(jnp.arange(rows, dtype=F32), GRID_W)
    col = jnp.tile(jnp.arange(GRID_W, dtype=F32), rows)
    n_freq = A_HEAD_DIM // 4
    inv = ROPE_THETA ** (-jnp.arange(n_freq, dtype=F32) / n_freq)
    ang_r = row[:, None] * inv
    ang_c = col[:, None] * inv
    cos = jnp.concatenate([jnp.cos(ang_r)] * 2 + [jnp.cos(ang_c)] * 2, axis=1)
    sin = jnp.concatenate([-jnp.sin(ang_r), jnp.sin(ang_r), -jnp.sin(ang_c), jnp.sin(ang_c)], axis=1)
    return jnp.tile(cos, (1, 2)), jnp.tile(sin, (1, 2))


def _prep_even(w_in, w_out, qn, kn, conv_w, conv_b, wr, br, wi, bi, lam):
    G = A_HEADS // A_KV_HEADS
    wq = w_in[:, :A_Q].reshape(D_MODEL, A_HEADS, A_HEAD_DIM)
    zeros = jnp.zeros_like(wq)
    half = (jnp.arange(A_HEADS) // G)[None, :, None]
    wq_pad = jnp.concatenate([jnp.where(half == 0, wq, zeros), jnp.where(half == 1, wq, zeros)], axis=-1)
    w = jnp.concatenate([wq_pad.reshape(D_MODEL, A_HEADS * LANES), w_in[:, A_Q:A_Q + A_KV],
                         w_in[:, A_Q + 2 * A_KV:]], axis=1).astype(BF16)
    wvt = w_in[:, A_Q + A_KV:A_Q + 2 * A_KV].T.astype(BF16)
    qg =jnp.tile(qn.astype(F32), 2 * A_HEADS)[None]
    kg = jnp.tile(kn.astype(F32), 2)[None]

    def dense(blocks):
        eye = jnp.eye(B_BLOCKS, dtype=blocks.dtype)
        return jnp.einsum('nde,nm->ndme', blocks, eye).reshape(B_WIDTH, B_WIDTH)

    wg = jnp.stack([dense(wr[0]), dense(wi[0]), dense(wr[1]), dense(wi[1])]).astype(BF16)
    bg = jnp.stack([br[0], bi[0], br[1], bi[1]]).astype(F32)
    return dict(w=w, wvt=wvt, qg=qg, kg=kg, cw=conv_w.astype(F32), cb=conv_b.astype(F32)[None], wg=wg, bg=bg,
                lam=lam.astype(F32), wa=w_out[:A_Q].astype(BF16), wb=w_out[A_Q:].astype(BF16))


def _even_layer(x, gain, p, ff, final, cos, sin, B, S):
    q, k, vt, xr, gr = _even_proj(x, gain, p['w'], p['wvt'], p['qg'], p['kg'], cos, sin, S)
    ya = _gqa(q, k, vt, B, S)
    hf, hr = _rglru(xr, p['cw'], p['cb'], p['wg'], p['bg'], p['lam'], B, S)
    return _even_out(x, ya, hf, hr, gr, p['wa'], p['wb'], ff, final)


def _odd_proj_kernel(x_ref, g_ref, w_ref, nega_ref, dtb_ref, perm_ref, qkv_ref, z_ref, *rest):
    *d_refs, gb_ref = rest
    tm = x_ref.shape[0]
    xn = _rms(x_ref[...], g_ref[...]).astype(BF16)
    proj = jnp.dot(xn, w_ref[...], preferred_element_type=F32)
    off = 3 * C_WIDTH
    qkv_ref[...] = proj[:, :off]
    z_ref[...] = proj[:, off:off + C_WIDTH]
    off += C_WIDTH
    for ti, scale in enumerate((D_HEAD_DIM ** -0.5, 1.0, 1.0)):
        for gi, (_, dil) in enumerate(D_GROUPS):
            xg = (proj[:, off + gi * D_GW:off + (gi + 1) * D_GW] * scale).astype(BF16)
            ref = d_refs[ti * len(D_GROUPS) + gi]
            if dil == 1:
                ref[...] = xg
            else:
                y = jnp.dot(perm_ref[gi - 1], xg, preferred_element_type=F32)
                ref[...] = y.reshape(dil, tm // dil, D_GW).astype(BF16)
        off += D_WIDTH
    t = proj[:, off:off + LANES]
    lane = lax.broadcasted_iota(jnp.int32, t.shape, 1)
    gate = jnp.where(lane < 2 * C_HEADS, _sigmoid(t), nega_ref[...] * _softplus(t + dtb_ref[...]))
    gb_ref[...] = jnp.transpose(gate)[:4 * C_HEADS]


def _class_major_spec(dil, S, tm):
    n_s = S // tm
    return pl.BlockSpec((None, dil, tm // dil, D_GW), lambda i: (i // n_s, 0, i % n_s, 0))


def _odd_proj(x, gain, w, nega, dtb, perm, B, S):
    tm = PERM_TM
    M = x.shape[0]
    row = lambda i: (i, 0)
    const = lambda i: (0, 0)
    d_specs, d_shapes = [], []
    for _ in range(3):
        for _, dil in D_GROUPS:
            if dil == 1:
                d_specs.append(pl.BlockSpec((tm, D_GW), row))
                d_shapes.append(jax.ShapeDtypeStruct((M, D_GW), BF16))
            else:
                d_specs.append(_class_major_spec(dil, S, tm))
                d_shapes.append(jax.ShapeDtypeStruct((B, dil, S // dil, D_GW), BF16))
    outs = pl.pallas_call(
        _odd_proj_kernel,
        grid=(M // tm,),
        in_specs=[pl.BlockSpec((tm, D_MODEL), row), pl.BlockSpec((1, D_MODEL), const),
                  pl.BlockSpec((D_MODEL, w.shape[1]), const),
                  pl.BlockSpec((1, LANES), const), pl.BlockSpec((1, LANES), const),
                  pl.BlockSpec(perm.shape, lambda i: (0, 0, 0))],
        out_specs=[pl.BlockSpec((tm, 3 * C_WIDTH), row), pl.BlockSpec((tm, C_WIDTH), row)] + d_specs
        + [pl.BlockSpec((4 * C_HEADS, tm), lambda i: (0, i))],
        out_shape=[jax.ShapeDtypeStruct((M, 3 * C_WIDTH), F32), jax.ShapeDtypeStruct((M, C_WIDTH), F32)]
        + d_shapes + [jax.ShapeDtypeStruct((4 * C_HEADS, M), F32)],
        compiler_params=_cparams("parallel"),
        name="odd_proj",
    )(x, gain, w, nega, dtb, perm)
    ng = len(D_GROUPS)
    return outs[0], outs[1], outs[2:2 + ng], outs[2 + ng:2 + 2 * ng], outs[2 + 2 * ng:2 + 3 * ng], outs[-1]


def _gdn_prep_kernel(p_ref, c_ref, n_ref, cw_ref, q_ref, k_ref, v_ref, *, nblk):
    t = pl.program_id(1)
    y = _silu(_conv4(p_ref[...], c_ref[...], n_ref[...], cw_ref[...], t == 0, t == nblk - 1))
    for h in range(C_HEADS):
        for part, ref, scale in ((0, q_ref, C_HEAD_DIM ** -0.5), (1, k_ref, 1.0)):
            a = y[:, part * C_WIDTH + h * LANES:part * C_WIDTH + (h + 1) * LANES]
            a = a * lax.rsqrt(jnp.sum(a * a, axis=-1, keepdims=True) + EPS)
            ref[:, h * LANES:(h + 1) * LANES] = a * scale
    v_ref[...] = y[:, 2 * C_WIDTH:]


def _gdn_prep(qkv, cw, B, S, tm=256):
    M = qkv.shape[0]
    nblk = S // tm
    r = tm // HALO
    nrb = M // HALO
    W = 3 * C_WIDTH
    blk = lambda b, t: b * nblk + t
    return pl.pallas_call(
        functools.partial(_gdn_prep_kernel, nblk=nblk),
        grid=(B, nblk),
        in_specs=[pl.BlockSpec((HALO, W), lambda b, t: (jnp.maximum(blk(b, t) * r - 1, 0), 0)),
                  pl.BlockSpec((tm, W), lambda b, t: (blk(b, t), 0)),
                  pl.BlockSpec((HALO, W), lambda b, t: (jnp.minimum((blk(b, t) + 1) * r, nrb - 1), 0)),
                  pl.BlockSpec((4, W), lambda b, t: (0, 0))],
        out_specs=[pl.BlockSpec((tm, C_WIDTH), lambda b, t: (blk(b, t), 0))] * 3,
        out_shape=[jax.ShapeDtypeStruct((M, C_WIDTH), F32)] * 3,
        compiler_params=_cparams("parallel", "parallel"),
        name="gdn_prep",
    )(qkv, qkv, qkv, cw)


def _nt_dot(a, b):
    return lax.dot_general(a, b, (((1,), (1,)), ((), ())), preferred_element_type=F32)


def _gdn_kernel(qf_ref, kf_ref, vf_ref, gf_ref, qr_ref, kr_ref, vr_ref, gr_ref, of_ref, or_ref, state, *, tm):
    C = C_CHUNK
    SB = LANES
    nsb = tm // SB
    nc = tm // C
    t = pl.program_id(1)

    @pl.when(t == 0)
    def _():
        state[...] = jnp.zeros_like(state)

    ri = lax.broadcasted_iota(jnp.int32, (SB, SB), 0)
    ci = lax.broadcasted_iota(jnp.int32, (SB, SB), 1)
    same = (ri // C) == (ci // C)
    eye = jnp.where(ri == ci, 1.0, 0.0)
    incl, strict, levels = [], [], []
    for d in range(2):
        hi_i, lo_i = (ci, ri) if d else (ri, ci)
        incl.append(same & (hi_i >= lo_i))
        strict.append(same & (hi_i > lo_i))
        lv = []
        s = 1
        while s < C:
            lv.append((((hi_i // s) % 2) == 1) & ((lo_i // s) == (hi_i // s) - 1))
            s *= 2
        levels.append(lv)

    pos = lax.broadcasted_iota(jnp.int32, (4 * C_HEADS, tm), 1) % C
    rows, cols = [], []
    for d, g_ref in enumerate((gf_ref, gr_ref)):
        gt = g_ref[...]
        x = gt
        s = 1
        while s < C:
            if d == 0:
                x = x + jnp.where(pos >= s, pltpu.roll(x, s, 1), 0.0)
            else:
                x = x + jnp.where(pos < C - s, pltpu.roll(x, tm - s, 1), 0.0)
            s *= 2
        lo = d * C_HEADS
        rw = jnp.concatenate([gt[lo:lo + C_HEADS], x[2 * C_HEADS + lo:2 * C_HEADS + lo + C_HEADS]], axis=0)
        rows.append(rw)
        cols.append(jnp.transpose(jnp.concatenate([rw, jnp.zeros((LANES - 2 * C_HEADS, tm), F32)], axis=0)))

    refs = ((qf_ref, kf_ref, vf_ref, of_ref), (qr_ref, kr_ref, vr_ref, or_ref))
    chains = []
    for d in range(2):
        for h in range(C_HEADS):
            for sb in range(nsb):
                rs = slice(sb * SB, (sb + 1) * SB)
                sl = slice(h * LANES, (h + 1) * LANES)
                q = refs[d][0][rs, sl]
                k = refs[d][1][rs, sl]
                v = refs[d][2][rs, sl]
                beta = jnp.broadcast_to(cols[d][rs, h:h + 1], (SB, LANES))
                gcc = jnp.broadcast_to(cols[d][rs, C_HEADS + h:C_HEADS + h + 1], (SB, LANES))
                gcr = jnp.broadcast_to(rows[d][C_HEADS + h:C_HEADS + h + 1, rs], (SB, SB))
                dec = jnp.exp(jnp.where(incl[d], gcc - gcr, NEG_INF))
                eg = jnp.exp(gcc)
                kbf = k.astype(BF16)
                kb = k * beta
                lmat = jnp.where(strict[d], _nt_dot(kb.astype(BF16), kbf) * dec, 0.0)
                amat = jnp.where(incl[d], _nt_dot(q.astype(BF16), kbf) * dec, 0.0).astype(BF16)
                chains.append(dict(d=d, h=h, sb=sb, k=k, gcc=gcc, lmat=lmat, amat=amat,
                                   rhs=jnp.concatenate([v * beta, kb * eg], axis=1).astype(BF16),
                                   qe=(q * eg).astype(BF16)))

    for ch in chains:
        ch['x'] = eye - jnp.where(levels[ch['d']][0], ch['lmat'], 0.0)
    for li in range(1, len(levels[0])):
        for ch in chains:
            ch['xb'] = ch['x'].astype(BF16)
            cmat = jnp.where(levels[ch['d']][li], ch['lmat'], 0.0).astype(BF16)
            ch['cx'] = jnp.dot(cmat, ch['xb'], preferred_element_type=F32).astype(BF16)
        for ch in chains:
            ch['x'] = ch['x'] - jnp.dot(ch['xb'], ch['cx'], preferred_element_type=F32)
    for ch in chains:
        sol = jnp.dot(ch['x'].astype(BF16), ch['rhs'], preferred_element_type=F32)
        ch['u'] = sol[:, :LANES]
        ch['w'] = sol[:, LANES:].astype(BF16)
        ch['vn'] = [None] * (SB // C)
        ch['ost'] = [None] * (SB // C)

    by_key = {(ch['d'], ch['h'], ch['sb']): ch for ch in chains}
    st = {(d, h): state[d, h] for d in range(2) for h in range(C_HEADS)}
    for step in range(nc):
        work = []
        for (d, h), s_val in st.items():
            c = nc - 1 - step if d else step
            ch = by_key[(d, h, c // (SB // C))]
            lc = c % (SB // C)
            rs = slice(lc * C, (lc + 1) * C)
            stb = s_val.astype(BF16)
            vn = ch['u'][rs] - jnp.dot(ch['w'][rs], stb, preferred_element_type=F32)
            ch['ost'][lc] = jnp.dot(ch['qe'][rs], stb, preferred_element_type=F32)
            ch['vn'][lc] = vn
            last = lc * C if d else (lc + 1) * C - 1
            gl = ch['gcc'][last:last + 1, :]
            kd = (ch['k'][rs] * jnp.exp(gl - ch['gcc'][rs])).astype(BF16)
            work.append(((d, h), s_val * jnp.exp(gl), kd, vn.astype(BF16)))
        for key, decayed, kd, vnb in work:
            st[key] = decayed + lax.dot_general(kd, vnb, (((0,), (0,)), ((), ())), preferred_element_type=F32)
    for (d, h), s_val in st.items():
        state[d, h] = s_val
    for ch in chains:
        rs = slice(ch['sb'] * SB, (ch['sb'] + 1) * SB)
        sl = slice(ch['h'] * LANES, (ch['h'] + 1) * LANES)
        vn_all = jnp.concatenate(ch['vn'], axis=0).astype(BF16)
        refs[ch['d']][3][rs, sl] = (jnp.concatenate(ch['ost'], axis=0)
                                   + jnp.dot(ch['amat'], vn_all, preferred_element_type=F32))


def _gdn(q, k, v, gbt, B, S, tm=256):
    M = q.shape[0]
    nblk = S // tm
    fwd = lambda b, t: b * nblk + t
    rev = lambda b, t: b * nblk + (nblk - 1 - t)
    specs = []
    for blk in (fwd, rev):
        specs += [pl.BlockSpec((tm, C_WIDTH), lambda b, t, blk=blk: (blk(b, t), 0))] * 3
        specs += [pl.BlockSpec((4 * C_HEADS, tm), lambda b, t, blk=blk: (0, blk(b, t)))]
    return pl.pallas_call(
        functools.partial(_gdn_kernel, tm=tm),
        grid=(B, nblk),
        in_specs=specs,
        out_specs=[pl.BlockSpec((tm, C_WIDTH), lambda b, t: (fwd(b, t), 0)),
                   pl.BlockSpec((tm, C_WIDTH), lambda b, t: (rev(b, t), 0))],
        out_shape=[jax.ShapeDtypeStruct((M, C_WIDTH), F32)] * 2,
        scratch_shapes=[pltpu.VMEM((2, C_HEADS, C_HEAD_DIM, C_HEAD_DIM), F32)],
        compiler_params=_cparams("arbitrary", "arbitrary"),
        name="gdn",
    )(q, k, v, gbt, q, k, v, gbt)


def _band_kernel(q_ref, kp_ref, kc_ref, kn_ref, vp_ref, vc_ref, vn_ref, bias_ref, o_ref, lse_ref, *, tq, m_len):
    t = pl.program_id(1)
    QT = BAND_QT
    nkeys = QT + 2 * D_STEPS
    kwin = jnp.concatenate([kp_ref[...], kc_ref[...], kn_ref[...]], axis=0)
    vwin = jnp.concatenate([vp_ref[...], vc_ref[...], vn_ref[...]], axis=0)
    lane = lax.broadcasted_iota(jnp.int32, (QT, D_GW), 1)
    mine = [(lane // D_HEAD_DIM) == h for h in range(D_HEADS_PER_GROUP)]
    col = lax.broadcasted_iota(jnp.int32, (QT, nkeys), 1)
    tiles = []
    for i in range(tq // QT):
        q = q_ref[i * QT:(i + 1) * QT, :]
        kpos = t * tq + i * QT - D_STEPS + col
        valid = (kpos >= 0) & (kpos < m_len)
        kw = kwin[i * QT:i * QT + nkeys]
        for h in range(D_HEADS_PER_GROUP):
            qh = jnp.where(mine[h], q, jnp.zeros_like(q))
            s = lax.dot_general(qh, kw, (((1,), (1,)), ((), ())), preferred_element_type=F32) + bias_ref[h]
            tiles.append([i, h, jnp.where(valid, s, NEG_INF)])
    for tile in tiles:
        s = tile[2]
        m = jnp.max(s, axis=-1, keepdims=True)
        e = jnp.exp(s - m)
        l = jnp.sum(e, axis=-1, keepdims=True)
        tile[2] = (e / l).astype(BF16)
        tile.append(m + jnp.log(l))
    for i in range(tq // QT):
        o = jnp.zeros((QT, D_GW), F32)
        lse_full = jnp.zeros((QT, D_GW), F32)
        vw = vwin[i * QT:i * QT + nkeys]
        for _, h, p, lse in tiles[i * D_HEADS_PER_GROUP:(i + 1) * D_HEADS_PER_GROUP]:
            o = jnp.where(mine[h], jnp.dot(p, vw, preferred_element_type=F32), o)
            lse_full = jnp.where(mine[h], lse, lse_full)
        o_ref[i * QT:(i + 1) * QT, :] = o
        lse_ref[i * QT:(i + 1) * QT, :] = lse_full


def _band(q, k, v, bias, Z, m_len):
    M = q.shape[0]
    tq = min(BAND_TQ, m_len)
    nq = m_len // tq
    r = tq // D_STEPS
    nhb = M // D_STEPS
    cur = lambda z, t: (z * nq + t, 0)
    prev = lambda z, t: (jnp.maximum((z * nq + t) * r - 1, 0), 0)
    nxt = lambda z, t: (jnp.minimum((z * nq + t + 1) * r, nhb - 1), 0)
    halo = lambda f: pl.BlockSpec((D_STEPS, D_GW), f)
    full = pl.BlockSpec((tq, D_GW), cur)
    return pl.pallas_call(
        functools.partial(_band_kernel, tq=tq, m_len=m_len),
        grid=(Z, nq),
        in_specs=[full, halo(prev), full, halo(nxt), halo(prev), full, halo(nxt),
                  pl.BlockSpec((D_HEADS_PER_GROUP, BAND_QT, BAND_QT + 2 * D_STEPS), lambda z, t: (0, 0, 0))],
        out_specs=[full, full],
        out_shape=[jax.ShapeDtypeStruct((M, D_GW), F32)] * 2,
        compiler_params=_cparams("parallel", "parallel"),
        name="band_attn",
    )(q, k, k, k, v, v, v, bias)


def _odd_out_kernel(x_ref, of_ref, or_ref, z_ref, cg_ref, o0_ref, o1_ref, o2_ref, l0_ref, l1_ref, l2_ref,
                    unperm_ref, wc_ref, wd_ref, *rest, final):
    *ff, out_ref = rest
    tm = x_ref.shape[0]
    oc = of_ref[...] + or_ref[...]
    z = z_ref[...]
    parts = []
    for h in range(C_HEADS):
        sl = slice(h * LANES, (h + 1) * LANES)
        parts.append(_rms(oc[:, sl], cg_ref[...]) * _silu(z[:, sl]))
    yc = jnp.concatenate(parts, axis=1).astype(BF16)

    def token_order(ref, gi):
        if D_GROUPS[gi][1] == 1:
            return ref[...]
        val = ref[...].reshape(tm, D_GW)
        hi = val.astype(BF16)
        lo = (val - hi.astype(F32)).astype(BF16)
        pt = unperm_ref[gi - 1]
        return jnp.dot(pt, hi, preferred_element_type=F32) + jnp.dot(pt, lo, preferred_element_type=F32)

    o0, o1, o2 = (token_order(r, gi) for gi, r in enumerate((o0_ref, o1_ref, o2_ref)))
    l0, l1, l2 = (token_order(r, gi) for gi, r in enumerate((l0_ref, l1_ref, l2_ref)))
    m = jnp.maximum(jnp.maximum(l0, l1), l2)
    e0, e1, e2 = jnp.exp(l0 - m), jnp.exp(l1 - m), jnp.exp(l2 - m)
    den = e0 + e1 + e2
    yd = ((e0 / den) * o0 + (e1 / den) * o1 + (e2 / den) * o2).astype(BF16)
    y = jnp.dot(yc, wc_ref[...], preferred_element_type=F32) + jnp.dot(yd, wd_ref[...], preferred_element_type=F32)
    out_ref[...] = _mlp_tail(x_ref[...] + y, ff, final)


def _odd_out(x, o_f, o_r, z, cg, os_, ls_, unperm, wc, wd, ff, final, B, S):
    tm = PERM_TM
    M = x.shape[0]
    row = lambda i: (i, 0)
    wide = pl.BlockSpec((tm, C_WIDTH), row)
    grp = [pl.BlockSpec((tm, D_GW), row) if dil == 1 else _class_major_spec(dil, S, tm) for _, dil in D_GROUPS]
    return pl.pallas_call(
        functools.partial(_odd_out_kernel, final=final),
        grid=(M // tm,),
        in_specs=[pl.BlockSpec((tm, D_MODEL), row), wide, wide, wide, _resident((1, LANES))] + grp + grp
        + [_resident(unperm.shape), _resident((C_WIDTH, D_MODEL)), _resident((D_GW, D_MODEL))] + _ff_specs(),
        out_specs=pl.BlockSpec((tm, D_MODEL), row),
        out_shape=jax.ShapeDtypeStruct((M, D_MODEL), F32),
        compiler_params=_cparams("parallel"),
        name="odd_out_mlp",
    )(x, o_f, o_r, z, cg, *os_, *ls_, unperm, wc, wd, *ff)


def _class_perms():
    mats = []
    for _, dil in D_GROUPS[1:]:
        out_row = np.arange(PERM_TM)
        src = (out_row % (PERM_TM // dil)) * dil + out_row // (PERM_TM // dil)
        mats.append(np.eye(PERM_TM, dtype=np.float32)[src])
    perm = np.stack(mats)
    return jnp.asarray(perm, BF16), jnp.asarray(perm.transpose(0, 2, 1), BF16)


def _t5_bucket(rel):
    nb = N_BUCKETS // 2
    max_exact = nb // 2
    n = np.abs(rel)
    large = max_exact + (np.log(np.maximum(n, 1) / max_exact) / math.log(MAX_DISTANCE / max_exact)
                         * (nb - max_exact)).astype(np.int64)
    large = np.minimum(large, nb - 1)
    return (np.where(rel > 0, nb, 0) + np.where(n < max_exact, n, large)).astype(np.int32)


def _band_bias(rel_bias, tq=128):
    i = np.arange(tq)[:, None]
    j = np.arange(tq + 2 * D_STEPS)[None, :]
    delta = j - D_STEPS - i
    in_band = np.abs(delta) <= D_STEPS
    idx = np.clip(delta + D_STEPS, 0, 2 * D_STEPS)
    tiles = []
    for gi, (window, dil) in enumerate(D_GROUPS):
        steps = window // (2 * dil)
        assert steps == D_STEPS
        buckets = _t5_bucket(np.arange(-steps, steps + 1) * dil)
        b = rel_bias.astype(F32)[jnp.asarray(buckets)][:, gi * D_HEADS_PER_GROUP:(gi + 1) * D_HEADS_PER_GROUP].T
        tiles.append(jnp.where(jnp.asarray(in_band)[None], b[:, jnp.asarray(idx)], NEG_INF))
    return jnp.stack(tiles)


def _prep_odd(w_in, w_out, conv_w, a_log, dt_bias, o_gain, rel_bias):
    n_c = 4 * C_WIDTH
    n_ba = 4 * C_HEADS
    w = jnp.concatenate([w_in[:, :n_c], w_in[:, n_c + n_ba:], w_in[:, n_c:n_c + n_ba],
                         jnp.zeros((D_MODEL, LANES - n_ba), w_in.dtype)], axis=1).astype(BF16)
    pad = jnp.zeros((LANES - n_ba,), F32)
    nega = jnp.concatenate([jnp.zeros((2 * C_HEADS,), F32), -jnp.exp(a_log.astype(F32)).reshape(-1), pad])[None]
    dtb = jnp.concatenate([jnp.zeros((2 * C_HEADS,), F32), dt_bias.astype(F32).reshape(-1), pad])[None]
    return dict(w=w, nega=nega, dtb=dtb, cw=conv_w.astype(F32), cg=o_gain.astype(F32)[None],
                bias=_band_bias(rel_bias), wc=w_out[:C_WIDTH].astype(BF16), wd=w_out[C_WIDTH:].astype(BF16))


def _odd_layer(x, gain, p, ff, final, B, S):
    perm, unperm = _class_perms()
    qkv, z, dq, dk, dv, gb = _odd_proj(x, gain, p['w'], p['nega'], p['dtb'], perm, B, S)
    cq, ck, cv = _gdn_prep(qkv, p['cw'], B, S)
    o_f, o_r = _gdn(cq, ck, cv, gb, B, S)
    outs, lses = [], []
    for gi, (_, dil) in enumerate(D_GROUPS):
        flat = lambda a: a.reshape(B * S, D_GW)
        o, lse = _band(flat(dq[gi]), flat(dk[gi]), flat(dv[gi]), p['bias'][gi], B * dil, S // dil)
        shape = (B * S, D_GW) if dil == 1 else (B, dil, S // dil, D_GW)
        outs.append(o.reshape(shape))
        lses.append(lse.reshape(shape))
    return _odd_out(x, o_f, o_r, z, p['cg'], outs, lses, unperm, p['wc'], p['wd'], ff, final, B, S)


def _trunk(x3, evens, odds, norm_mix, norm_ff, norm_final, w1, w2):
    B, S, _ = x3.shape
    x = x3.reshape(B * S, D_MODEL)
    cos, sin = _rope_tables(S)
    for layer in range(DEPTH):
        gain = norm_mix[layer][None]
        ff = (norm_ff[layer][None], w1[layer], w2[layer], norm_final[None])
        final = layer == DEPTH - 1
        if layer % 2 == 0:
            x = _even_layer(x, gain, evens[layer // 2], ff, final, cos, sin, B, S)
        else:
            x = _odd_layer(x, gain, odds[layer // 2], ff, final, B, S)
    return x.reshape(B, S, D_MODEL)


def kernel(x_prompt, x_sample, rel_bias, norm_mix, norm_ff, norm_final, w_ff1, w_ff2, w_in_e, w_out_e,
           a_qnorm, a_knorm, b_conv_w, b_conv_b, b_wr, b_br, b_wi, b_bi, b_lambda, w_in_o, w_out_o,
           c_conv_w, c_a_log, c_dt_bias, c_norm):
    evens = [_prep_even(w_in_e[j], w_out_e[j], a_qnorm[j], a_knorm[j], b_conv_w[j], b_conv_b[j],
                        b_wr[j], b_br[j], b_wi[j], b_bi[j], b_lambda[j]) for j in range((DEPTH + 1) // 2)]
    odds = [_prep_odd(w_in_o[j], w_out_o[j], c_conv_w[j], c_a_log[j], c_dt_bias[j], c_norm[j], rel_bias)
            for j in range(DEPTH // 2)]
    w1 = w_ff1.astype(BF16)
    w2 = w_ff2.astype(BF16)
    nm = norm_mix.astype(F32)
    nf = norm_ff.astype(F32)
    ng = norm_final.astype(F32)
    y_prompt = _trunk(x_prompt, evens, odds, nm, nf, ng, w1, w2)
    y_sample = _trunk(x_sample, evens, odds, nm, nf, ng, w1, w2)
    return (y_prompt, y_sample)
```

```python
import functools
import math

import numpy as np
import jax
import jax.numpy as jnp
from jax import lax
from jax.experimental import pallas as pl
from jax.experimental.pallas import tpu as pltpu

F32 = jnp.float32
BF16 = jnp.bfloat16

D_MODEL = 1024
D_FF = 4 * D_MODEL
DEPTH = 4
EPS = 1e-6
NEG_INF = -1e30
GRID_W = 64
LANES = 128

A_HEADS = 8
A_KV_HEADS = 2
A_HEAD_DIM = 64
A_Q = A_HEADS * A_HEAD_DIM
A_KV = A_KV_HEADS * A_HEAD_DIM
ROPE_THETA = 10000.0
B_WIDTH = 512
B_BLOCKS = 8
B_BLOCK_DIM = B_WIDTH // B_BLOCKS
RG_C = 8.0
C_HEADS = 4
C_HEAD_DIM = 128
C_WIDTH = C_HEADS * C_HEAD_DIM
C_CHUNK = 64
D_GROUPS = ((128, 1), (512, 4), (2048, 16))
D_HEADS_PER_GROUP = 4
D_HEAD_DIM = 64
D_NHEADS = len(D_GROUPS) * D_HEADS_PER_GROUP
D_WIDTH = D_NHEADS * D_HEAD_DIM
D_GW = D_HEADS_PER_GROUP * D_HEAD_DIM
D_STEPS = 64
N_BUCKETS = 32
MAX_DISTANCE = 1024
HALO = 8
Q_SCALE = A_HEAD_DIM ** -0.5 * math.log2(math.e)
MLP_TF = 1024
PERM_TM = 512
BAND_QT = 128
BAND_TQ = 512
GQA_TILE = 256
VT_PAD = 16
VT_ROWS = A_KV_HEADS * (A_HEAD_DIM + VT_PAD)

VMEM_LIMIT = 56 * 1024 * 1024


def _cparams(*sem):
    return pltpu.CompilerParams(dimension_semantics=sem, vmem_limit_bytes=VMEM_LIMIT)


def _rms(x, gain):
    return x * lax.rsqrt(jnp.mean(x * x, axis=-1, keepdims=True) + EPS) * gain


def _softplus(x):
    return jnp.maximum(x, 0.0) + jnp.log1p(jnp.exp(-jnp.abs(x)))


def _sigmoid(x):
    return 0.5 * jnp.tanh(0.5 * x) + 0.5


def _silu(x):
    return x * _sigmoid(x)


def _gelu_tanh(x):
    return 0.5 * x * (1.0 + jnp.tanh(math.sqrt(2.0 / math.pi) * (x + 0.044715 * (x * x * x))))


def _conv4(prev, cur, nxt, w, first, last):
    rows = cur.shape[0]
    prev = jnp.where(first, 0.0, prev)
    nxt = jnp.where(last, 0.0, nxt)
    full = jnp.concatenate([prev, cur, nxt], axis=0)
    y = full[HALO - 2:HALO - 2 + rows] * w[0:1]
    for j in range(1, 4):
        y = y + full[HALO - 2 + j:HALO - 2 + j + rows] * w[j:j + 1]
    return y


def _even_proj_kernel(x_ref, g_ref, w_ref, wvt_ref, qg_ref, kg_ref, cos_ref, sin_ref,
                      q_ref, k_ref, vt_ref, xr_ref, gr_ref):
    xn = _rms(x_ref[...], g_ref[...]).astype(BF16)
    proj = jnp.dot(xn, w_ref[...], preferred_element_type=F32)
    vt = lax.dot_general(wvt_ref[...], xn, (((1,), (1,)), ((), ())), preferred_element_type=F32).astype(BF16)
    ones = jnp.ones((VT_PAD, LANES), BF16)
    for c in range(vt_ref.shape[0]):
        cols = slice(c * LANES, (c + 1) * LANES)
        vt_ref[c] = jnp.concatenate([vt[:A_HEAD_DIM, cols], ones, vt[A_HEAD_DIM:, cols], ones], axis=0)
    cos = cos_ref[...]
    sin = sin_ref[...]
    lane = lax.broadcasted_iota(jnp.int32, cos.shape, 1)
    first_half = (lane % 32) < 16
    lo = lane < A_HEAD_DIM

    def rope(t):
        swapped = jnp.where(first_half, pltpu.roll(t, LANES - 16, 1), pltpu.roll(t, 16, 1))
        return t * cos + swapped * sin

    for h in range(A_HEADS):
        t = proj[:, h * LANES:(h + 1) * LANES]
        r = lax.rsqrt(jnp.sum(t * t, axis=-1, keepdims=True) * (1.0 / A_HEAD_DIM) + EPS)
        t = rope(t * r * qg_ref[:, h * LANES:(h + 1) * LANES])
        q_ref[:, h * LANES:(h + 1) * LANES] = (t * Q_SCALE).astype(BF16)
    off = A_HEADS * LANES
    t = proj[:, off:off + LANES]
    t2 = t * t
    s_lo = jnp.sum(jnp.where(lo, t2, 0.0), axis=-1, keepdims=True)
    s_hi = jnp.sum(jnp.where(lo, 0.0, t2), axis=-1, keepdims=True)
    r = lax.rsqrt(jnp.where(lo, s_lo, s_hi) * (1.0 / A_HEAD_DIM) + EPS)
    k_ref[...] = rope(t * r * kg_ref[...]).astype(BF16)
    off += LANES
    xr_ref[...] = proj[:, off:off + B_WIDTH]
    off += B_WIDTH
    gr_ref[...] = proj[:, off:off + B_WIDTH]


def _even_proj(x, gain, w, wvt, qg, kg, cos, sin, S, tm=512):
    M = x.shape[0]
    nS = S // tm
    n_in = w.shape[1]
    row = lambda i: (i, 0)
    const = lambda i: (0, 0)
    return pl.pallas_call(
        _even_proj_kernel,
        grid=(M // tm,),
        in_specs=[pl.BlockSpec((tm, D_MODEL), row), pl.BlockSpec((1, D_MODEL), const),
                  pl.BlockSpec((D_MODEL, n_in), const), pl.BlockSpec((A_KV, D_MODEL), const),
                  pl.BlockSpec((1, A_HEADS * LANES), const), pl.BlockSpec((1, LANES), const),
                  pl.BlockSpec((tm, LANES), lambda i: (i % nS, 0)),
                  pl.BlockSpec((tm, LANES), lambda i: (i % nS, 0))],
        out_specs=[pl.BlockSpec((tm, A_HEADS * LANES), row), pl.BlockSpec((tm, LANES), row),
                   pl.BlockSpec((tm // LANES, VT_ROWS, LANES), lambda i: (i, 0, 0)),
                   pl.BlockSpec((tm, B_WIDTH), row), pl.BlockSpec((tm, B_WIDTH), row)],
        out_shape=[jax.ShapeDtypeStruct((M, A_HEADS * LANES), BF16),
                   jax.ShapeDtypeStruct((M, LANES), BF16),
                   jax.ShapeDtypeStruct((M // LANES, VT_ROWS, LANES), BF16),
                   jax.ShapeDtypeStruct((M, B_WIDTH), F32), jax.ShapeDtypeStruct((M, B_WIDTH), F32)],
        compiler_params=_cparams("parallel"),
        name="even_proj",
    )(x, gain, w, wvt, qg, kg, cos, sin)


def _gqa_kernel(q_ref, k_ref, vt_ref, o_ref, m_ref, acc_ref, *, tq, kstep, nstep, unroll, lookahead):
    hpt = GQA_TILE // tq
    ntile = A_HEADS // hpt
    qt = [jnp.concatenate([q_ref[:, h * LANES:(h + 1) * LANES] for h in range(p * hpt, (p + 1) * hpt)], axis=0)
          for p in range(ntile)]
    m_ref[...] = jnp.full(m_ref.shape, NEG_INF, F32)
    acc_ref[...] = jnp.zeros(acc_ref.shape, F32)
    vrows = VT_ROWS // A_KV_HEADS

    per = kstep // LANES

    def scores(j, u, p):
        ks = k_ref[pl.ds(pl.multiple_of((j * unroll + u) * kstep, kstep), kstep), :]
        return lax.dot_general(ks, qt[p], (((1,), (1,)), ((), ())), preferred_element_type=F32)

    def update(j, u, p, st):
        g = (p * hpt) // (A_HEADS // A_KV_HEADS)
        m_old = m_ref[p]
        m_new = jnp.maximum(m_old, jnp.max(st, axis=0, keepdims=True))
        alpha = jnp.exp2(m_old - m_new)
        pt = jnp.exp2(st - m_new).astype(BF16)
        first = (j * unroll + u) * per
        vts = jnp.concatenate([vt_ref[first + i, g * vrows:(g + 1) * vrows, :] for i in range(per)], axis=1)
        acc_ref[p] = alpha * acc_ref[p] + jnp.dot(vts, pt, preferred_element_type=F32)
        m_ref[p] = m_new

    pairs = [(u, p) for u in range(unroll) for p in range(ntile)]
    niter = nstep // unroll

    def body(j, pending):
        pending = list(pending)
        j_next = jnp.minimum(j + 1, niter - 1)
        for idx in range(len(pairs)):
            ahead = idx + lookahead
            if ahead < len(pairs):
                pending.append(scores(j, *pairs[ahead]))
            else:
                pending.append(scores(j_next, *pairs[ahead - len(pairs)]))
            update(j, *pairs[idx], pending.pop(0))
        return tuple(pending)

    lax.fori_loop(0, niter, body, tuple(scores(0, *pairs[i]) for i in range(lookahead)))
    heads = []
    for p in range(ntile):
        acc = acc_ref[p]
        o = acc[:A_HEAD_DIM] / acc[A_HEAD_DIM:A_HEAD_DIM + 1]
        heads += [o[:, i * tq:(i + 1) * tq] for i in range(hpt)]
    for e in range(A_HEADS // 2):
        both = jnp.concatenate(heads[2 * e:2 * e + 2], axis=0)
        o_ref[:, e * LANES:(e + 1) * LANES] = jnp.transpose(both).astype(o_ref.dtype)


def _gqa(q, k, vt, B, S, tq=256, kstep=256, unroll=4, lookahead=5):
    M = q.shape[0]
    nq = S // tq
    nslab = S // LANES
    nstep = S // kstep
    unroll = min(unroll, nstep)
    kern = functools.partial(_gqa_kernel, tq=tq, kstep=kstep, nstep=nstep, unroll=unroll, lookahead=lookahead)
    return pl.pallas_call(
        kern,
        grid=(B, nq),
        in_specs=[pl.BlockSpec((tq, A_HEADS * LANES), lambda b, i: (b * nq + i, 0)),
                  pl.BlockSpec((S, LANES), lambda b, i: (b, 0)),
                  pl.BlockSpec((nslab, VT_ROWS, LANES), lambda b, i: (b, 0, 0))],
        out_specs=pl.BlockSpec((tq, A_Q), lambda b, i: (b * nq + i, 0)),
        out_shape=jax.ShapeDtypeStruct((M, A_Q), BF16),
        scratch_shapes=[pltpu.VMEM((A_HEADS * tq // GQA_TILE, 1, GQA_TILE), F32),
                        pltpu.VMEM((A_HEADS * tq // GQA_TILE, VT_ROWS // A_KV_HEADS, GQA_TILE), F32)],
        compiler_params=_cparams("parallel", "parallel"),
        name="gqa",
    )(q, k, vt)


def _rglru_kernel(xp_ref, xc_ref, xn_ref, yp_ref, yc_ref, yn_ref, cw_ref, cb_ref, wg_ref, bg_ref, lam_ref,
                  hf_ref, hr_ref, a_s, u_s, carry_s, *, tm, nblk):
    t = pl.program_id(1)

    @pl.when(t == 0)
    def _():
        carry_s[...] = jnp.zeros_like(carry_s)

    cw = cw_ref[...]
    ngrp = tm // 8
    row8 = lax.broadcasted_iota(jnp.int32, (ngrp, 8, B_WIDTH), 1)
    for d, (p_ref, c_ref, n_ref) in enumerate(((xp_ref, xc_ref, xn_ref), (yp_ref, yc_ref, yn_ref))):
        tt = t if d == 0 else nblk - 1 - t
        xc = _conv4(p_ref[...], c_ref[...], n_ref[...], cw, tt == 0, tt == nblk - 1) + cb_ref[...]
        xb = xc.astype(BF16)
        r = _sigmoid(jnp.dot(xb, wg_ref[2 * d], preferred_element_type=F32) + bg_ref[2 * d:2 * d + 1])
        i = _sigmoid(jnp.dot(xb, wg_ref[2 * d + 1], preferred_element_type=F32) + bg_ref[2 * d + 1:2 * d + 2])
        log_a = (-RG_C) * r * _softplus(-lam_ref[d:d + 1])
        a = jnp.exp(log_a).reshape(ngrp, 8, B_WIDTH)
        u = (jnp.sqrt(1.0 - jnp.exp(2.0 * log_a)) * (i * xc)).reshape(ngrp, 8, B_WIDTH)
        for sh in (1, 2, 4):
            keep = (row8 >= sh) if d == 0 else (row8 < 8 - sh)
            shift = sh if d == 0 else 8 - sh
            u = u + a * jnp.where(keep, pltpu.roll(u, shift, 1), 0.0)
            a = a * jnp.where(keep, pltpu.roll(a, shift, 1), 1.0)
        a_s[d] = a.reshape(tm, B_WIDTH)
        u_s[d] = u.reshape(tm, B_WIDTH)

    def body(gidx, carry):
        out = []
        for d, out_ref in enumerate((hf_ref, hr_ref)):
            gg = gidx if d == 0 else ngrp - 1 - gidx
            start = pl.multiple_of(gg * 8, 8)
            hg = u_s[d, pl.ds(start, 8), :] + a_s[d, pl.ds(start, 8), :] * carry[d]
            out_ref[pl.ds(start, 8), :] = hg
            edge = hg[7:8, :] if d == 0 else hg[0:1, :]
            out.append(jnp.broadcast_to(edge, (8, B_WIDTH)))
        return tuple(out)

    ends = lax.fori_loop(0, ngrp, body, (carry_s[0], carry_s[1]), unroll=4)
    carry_s[0] = ends[0]
    carry_s[1] = ends[1]


def _rglru(xr, cw, cb, wg, bg, lam, B, S, tm=256):
    M = xr.shape[0]
    nblk = S // tm
    r = tm // HALO
    nrb = M // HALO
    fwd = lambda b, t: b * nblk + t
    rev = lambda b, t: b * nblk + (nblk - 1 - t)
    specs = []
    for blk in (fwd, rev):
        specs += [pl.BlockSpec((HALO, B_WIDTH), lambda b, t, blk=blk: (jnp.maximum(blk(b, t) * r - 1, 0), 0)),
                  pl.BlockSpec((tm, B_WIDTH), lambda b, t, blk=blk: (blk(b, t), 0)),
                  pl.BlockSpec((HALO, B_WIDTH), lambda b, t, blk=blk: (jnp.minimum((blk(b, t) + 1) * r, nrb - 1), 0))]
    const2 = lambda b, t: (0, 0)
    const3 = lambda b, t: (0, 0, 0)
    kern = functools.partial(_rglru_kernel, tm=tm, nblk=nblk)
    return pl.pallas_call(
        kern,
        grid=(B, nblk),
        in_specs=specs + [pl.BlockSpec((4, B_WIDTH), const2), pl.BlockSpec((1, B_WIDTH), const2),
                          pl.BlockSpec((4, B_WIDTH, B_WIDTH), const3), pl.BlockSpec((4, B_WIDTH), const2),
                          pl.BlockSpec((2, B_WIDTH), const2)],
        out_specs=[pl.BlockSpec((tm, B_WIDTH), lambda b, t: (fwd(b, t), 0)),
                   pl.BlockSpec((tm, B_WIDTH), lambda b, t: (rev(b, t), 0))],
        out_shape=[jax.ShapeDtypeStruct((M, B_WIDTH), F32), jax.ShapeDtypeStruct((M, B_WIDTH), F32)],
        scratch_shapes=[pltpu.VMEM((2, tm, B_WIDTH), F32), pltpu.VMEM((2, tm, B_WIDTH), F32),
                        pltpu.VMEM((2, 8, B_WIDTH), F32)],
        compiler_params=_cparams("arbitrary", "arbitrary"),
        name="rglru",
    )(xr, xr, xr, xr, xr, xr, cw, cb, wg, bg, lam)


def _mlp_tail(x, ff, final):
    gff_ref, w1_ref, w2_ref, gfin_ref = ff
    xn = _rms(x, gff_ref[...]).astype(BF16)
    acc = x
    for c in range(D_FF // MLP_TF):
        h = jnp.dot(xn, w1_ref[:, c * MLP_TF:(c + 1) * MLP_TF], preferred_element_type=F32)
        h = jnp.square(jnp.maximum(h, 0.0)).astype(BF16)
        acc = acc + jnp.dot(h, w2_ref[c * MLP_TF:(c + 1) * MLP_TF, :], preferred_element_type=F32)
    return _rms(acc, gfin_ref[...]) if final else acc


def _resident(shape):
    return pl.BlockSpec(shape, lambda *_: (0,) * len(shape), pipeline_mode=pl.Buffered(1))


def _ff_specs():
    return [_resident((1, D_MODEL)), _resident((D_MODEL, D_FF)), _resident((D_FF, D_MODEL)),
            _resident((1, D_MODEL))]


def _even_out_kernel(x_ref, ya_ref, hf_ref, hr_ref, gr_ref, wa_ref, wb_ref, *rest, final):
    *ff, o_ref = rest
    yb = ((hf_ref[...] + hr_ref[...]) * _gelu_tanh(gr_ref[...])).astype(BF16)
    y = jnp.dot(ya_ref[...], wa_ref[...], preferred_element_type=F32)
    y = y + jnp.dot(yb, wb_ref[...], preferred_element_type=F32)
    o_ref[...] = _mlp_tail(x_ref[...] + y, ff, final)


def _even_out(x, ya, hf, hr, gr, wa, wb, ff, final, tm=512):
    M = x.shape[0]
    row = lambda i: (i, 0)
    return pl.pallas_call(
        functools.partial(_even_out_kernel, final=final),
        grid=(M // tm,),
        in_specs=[pl.BlockSpec((tm, D_MODEL), row), pl.BlockSpec((tm, A_Q), row),
                  pl.BlockSpec((tm, B_WIDTH), row), pl.BlockSpec((tm, B_WIDTH), row),
                  pl.BlockSpec((tm, B_WIDTH), row),
                  _resident((A_Q, D_MODEL)), _resident((B_WIDTH, D_MODEL))] + _ff_specs(),
        out_specs=pl.BlockSpec((tm, D_MODEL), row),
        out_shape=jax.ShapeDtypeStruct((M, D_MODEL), F32),
        compiler_params=_cparams("parallel"),
        name="even_out_mlp",
    )(x, ya, hf, hr, gr, wa, wb, *ff)


def _rope_tables(S):
    rows = S // GRID_W
    row = jnp.repeat(jnp.arange(rows, dtype=F32), GRID_W)
    col = jnp.tile(jnp.arange(GRID_W, dtype=F32), rows)
    n_freq = A_HEAD_DIM // 4
    inv = ROPE_THETA ** (-jnp.arange(n_freq, dtype=F32) / n_freq)
    ang_r = row[:, None] * inv
    ang_c = col[:, None] * inv
    cos = jnp.concatenate([jnp.cos(ang_r)] * 2 + [jnp.cos(ang_c)] * 2, axis=1)
    sin = jnp.concatenate([-jnp.sin(ang_r), jnp.sin(ang_r), -jnp.sin(ang_c), jnp.sin(ang_c)], axis=1)
    return jnp.tile(cos, (1, 2)), jnp.tile(sin, (1, 2))


def _prep_even(w_in, w_out, qn, kn, conv_w, conv_b, wr, br, wi, bi, lam):
    G = A_HEADS // A_KV_HEADS
    wq = w_in[:, :A_Q].reshape(D_MODEL, A_HEADS, A_HEAD_DIM)
    zeros = jnp.zeros_like(wq)
    half = (jnp.arange(A_HEADS) // G)[None, :, None]
    wq_pad = jnp.concatenate([jnp.where(half == 0, wq, zeros), jnp.where(half == 1, wq, zeros)], axis=-1)
    w = jnp.concatenate([wq_pad.reshape(D_MODEL, A_HEADS * LANES), w_in[:, A_Q:A_Q + A_KV],
                         w_in[:, A_Q + 2 * A_KV:]], axis=1).astype(BF16)
    wvt = w_in[:, A_Q + A_KV:A_Q + 2 * A_KV].T.astype(BF16)
    qg =jnp.tile(qn.astype(F32), 2 * A_HEADS)[None]
    kg = jnp.tile(kn.astype(F32), 2)[None]

    def dense(blocks):
        eye = jnp.eye(B_BLOCKS, dtype=blocks.dtype)
        return jnp.einsum('nde,nm->ndme', blocks, eye).reshape(B_WIDTH, B_WIDTH)

    wg = jnp.stack([dense(wr[0]), dense(wi[0]), dense(wr[1]), dense(wi[1])]).astype(BF16)
    bg = jnp.stack([br[0], bi[0], br[1], bi[1]]).astype(F32)
    return dict(w=w, wvt=wvt, qg=qg, kg=kg, cw=conv_w.astype(F32), cb=conv_b.astype(F32)[None], wg=wg, bg=bg,
                lam=lam.astype(F32), wa=w_out[:A_Q].astype(BF16), wb=w_out[A_Q:].astype(BF16))


def _even_layer(x, gain, p, ff, final, cos, sin, B, S):
    q, k, vt, xr, gr = _even_proj(x, gain, p['w'], p['wvt'], p['qg'], p['kg'], cos, sin, S)
    ya = _gqa(q, k, vt, B, S)
    hf, hr = _rglru(xr, p['cw'], p['cb'], p['wg'], p['bg'], p['lam'], B, S)
    return _even_out(x, ya, hf, hr, gr, p['wa'], p['wb'], ff, final)


def _odd_proj_kernel(xp_ref, x_ref, xn_ref, g_ref, w_ref, cw_ref, nega_ref, dtb_ref, perm_ref,
                     cq_ref, ck_ref, cv_ref, z_ref, *rest, n_s):
    *d_refs, gb_ref = rest
    tm = x_ref.shape[0]
    i = pl.program_id(0)
    gain = g_ref[...]
    xn = _rms(x_ref[...], gain).astype(BF16)
    ext = jnp.concatenate([_rms(xp_ref[...], gain).astype(BF16), xn, _rms(xn_ref[...], gain).astype(BF16)], axis=0)
    n_c = 3 * C_WIDTH
    pq = jnp.dot(ext, w_ref[:, :n_c], preferred_element_type=F32)
    proj = jnp.dot(xn, w_ref[:, n_c:], preferred_element_type=F32)
    y = _silu(_conv4(pq[:HALO], pq[HALO:HALO + tm], pq[HALO + tm:], cw_ref[...], i % n_s == 0, i % n_s == n_s - 1))
    for h in range(C_HEADS):
        for part, ref, scale in ((0, cq_ref, C_HEAD_DIM ** -0.5), (1, ck_ref, 1.0)):
            a = y[:, part * C_WIDTH + h * LANES:part * C_WIDTH + (h + 1) * LANES]
            a = a * lax.rsqrt(jnp.sum(a * a, axis=-1, keepdims=True) + EPS)
            ref[:, h * LANES:(h + 1) * LANES] = a * scale
    cv_ref[...] = y[:, 2 * C_WIDTH:]
    z_ref[...] = proj[:, :C_WIDTH]
    off = C_WIDTH
    for ti, scale in enumerate((D_HEAD_DIM ** -0.5, 1.0, 1.0)):
        for gi, (_, dil) in enumerate(D_GROUPS):
            xg = (proj[:, off + gi * D_GW:off + (gi + 1) * D_GW] * scale).astype(BF16)
            ref = d_refs[ti * len(D_GROUPS) + gi]
            if dil == 1:
                ref[...] = xg
            else:
                y = jnp.dot(perm_ref[gi - 1], xg, preferred_element_type=F32)
                ref[...] = y.reshape(dil, tm // dil, D_GW).astype(BF16)
        off += D_WIDTH
    t = proj[:, off:off + LANES]
    lane = lax.broadcasted_iota(jnp.int32, t.shape, 1)
    gate = jnp.where(lane < 2 * C_HEADS, _sigmoid(t), nega_ref[...] * _softplus(t + dtb_ref[...]))
    gb_ref[...] = jnp.transpose(gate)[:4 * C_HEADS]


def _class_major_spec(dil, S, tm):
    n_s = S // tm
    return pl.BlockSpec((None, dil, tm // dil, D_GW), lambda i: (i // n_s, 0, i % n_s, 0))


def _odd_proj(x, gain, w, cw, nega, dtb, perm, B, S):
    tm = PERM_TM
    M = x.shape[0]
    row = lambda i: (i, 0)
    const = lambda i: (0, 0)
    hb = tm // HALO
    last_hb = M // HALO - 1
    d_specs, d_shapes = [], []
    for _ in range(3):
        for _, dil in D_GROUPS:
            if dil == 1:
                d_specs.append(pl.BlockSpec((tm, D_GW), row))
                d_shapes.append(jax.ShapeDtypeStruct((M, D_GW), BF16))
            else:
                d_specs.append(_class_major_spec(dil, S, tm))
                d_shapes.append(jax.ShapeDtypeStruct((B, dil, S // dil, D_GW), BF16))
    wide = pl.BlockSpec((tm, C_WIDTH), row)
    outs = pl.pallas_call(
        functools.partial(_odd_proj_kernel, n_s=S // tm),
        grid=(M // tm,),
        in_specs=[pl.BlockSpec((HALO, D_MODEL), lambda i: (jnp.maximum(i * hb - 1, 0), 0)),
                  pl.BlockSpec((tm, D_MODEL), row),
                  pl.BlockSpec((HALO, D_MODEL), lambda i: (jnp.minimum((i + 1) * hb, last_hb), 0)),
                  pl.BlockSpec((1, D_MODEL), const), pl.BlockSpec((D_MODEL, w.shape[1]), const),
                  pl.BlockSpec((4, 3 * C_WIDTH), const),
                  pl.BlockSpec((1, LANES), const), pl.BlockSpec((1, LANES), const),
                  pl.BlockSpec(perm.shape, lambda i: (0, 0, 0))],
        out_specs=[wide, wide, wide, wide] + d_specs + [pl.BlockSpec((4 * C_HEADS, tm), lambda i: (0, i))],
        out_shape=[jax.ShapeDtypeStruct((M, C_WIDTH), F32)] * 4 + d_shapes
        + [jax.ShapeDtypeStruct((4 * C_HEADS, M), F32)],
        compiler_params=_cparams("parallel"),
        name="odd_proj",
    )(x, x, x, gain, w, cw, nega, dtb, perm)
    ng = len(D_GROUPS)
    return outs[:4], outs[4:4 + ng], outs[4 + ng:4 + 2 * ng], outs[4 + 2 * ng:4 + 3 * ng], outs[-1]


def _nt_dot(a, b):
    return lax.dot_general(a, b, (((1,), (1,)), ((), ())), preferred_element_type=F32)


def _gdn_kernel(qf_ref, kf_ref, vf_ref, gf_ref, qr_ref, kr_ref, vr_ref, gr_ref, of_ref, or_ref, state, *, tm):
    C = C_CHUNK
    SB = LANES
    nsb = tm // SB
    nc = tm // C
    t = pl.program_id(1)

    @pl.when(t == 0)
    def _():
        state[...] = jnp.zeros_like(state)

    ri = lax.broadcasted_iota(jnp.int32, (SB, SB), 0)
    ci = lax.broadcasted_iota(jnp.int32, (SB, SB), 1)
    same = (ri // C) == (ci // C)
    eye = jnp.where(ri == ci, 1.0, 0.0)
    incl, strict, levels = [], [], []
    for d in range(2):
        hi_i, lo_i = (ci, ri) if d else (ri, ci)
        incl.append(same & (hi_i >= lo_i))
        strict.append(same & (hi_i > lo_i))
        lv = []
        s = 1
        while s < C:
            lv.append((((hi_i // s) % 2) == 1) & ((lo_i // s) == (hi_i // s) - 1))
            s *= 2
        levels.append(lv)

    pos = lax.broadcasted_iota(jnp.int32, (4 * C_HEADS, tm), 1) % C
    rows, cols = [], []
    for d, g_ref in enumerate((gf_ref, gr_ref)):
        gt = g_ref[...]
        x = gt
        s = 1
        while s < C:
            if d == 0:
                x = x + jnp.where(pos >= s, pltpu.roll(x, s, 1), 0.0)
            else:
                x = x + jnp.where(pos < C - s, pltpu.roll(x, tm - s, 1), 0.0)
            s *= 2
        lo = d * C_HEADS
        rw = jnp.concatenate([gt[lo:lo + C_HEADS], x[2 * C_HEADS + lo:2 * C_HEADS + lo + C_HEADS]], axis=0)
        rows.append(rw)
        cols.append(jnp.transpose(jnp.concatenate([rw, jnp.zeros((LANES - 2 * C_HEADS, tm), F32)], axis=0)))

    refs = ((qf_ref, kf_ref, vf_ref, of_ref), (qr_ref, kr_ref, vr_ref, or_ref))
    chains = []
    for d in range(2):
        for h in range(C_HEADS):
            for sb in range(nsb):
                rs = slice(sb * SB, (sb + 1) * SB)
                sl = slice(h * LANES, (h + 1) * LANES)
                q = refs[d][0][rs, sl]
                k = refs[d][1][rs, sl]
                v = refs[d][2][rs, sl]
                beta = jnp.broadcast_to(cols[d][rs, h:h + 1], (SB, LANES))
                gcc = jnp.broadcast_to(cols[d][rs, C_HEADS + h:C_HEADS + h + 1], (SB, LANES))
                gcr = jnp.broadcast_to(rows[d][C_HEADS + h:C_HEADS + h + 1, rs], (SB, SB))
                dec = jnp.exp(jnp.where(incl[d], gcc - gcr, NEG_INF))
                eg = jnp.exp(gcc)
                kbf = k.astype(BF16)
                kb = k * beta
                lmat = jnp.where(strict[d], _nt_dot(kb.astype(BF16), kbf) * dec, 0.0)
                amat = jnp.where(incl[d], _nt_dot(q.astype(BF16), kbf) * dec, 0.0).astype(BF16)
                chains.append(dict(d=d, h=h, sb=sb, k=k, gcc=gcc, lmat=lmat, amat=amat,
                                   rhs=jnp.concatenate([v * beta, kb * eg], axis=1).astype(BF16),
                                   qe=(q * eg).astype(BF16)))

    for ch in chains:
        ch['x'] = eye - jnp.where(levels[ch['d']][0], ch['lmat'], 0.0)
    for li in range(1, len(levels[0])):
        for ch in chains:
            ch['xb'] = ch['x'].astype(BF16)
            cmat = jnp.where(levels[ch['d']][li], ch['lmat'], 0.0).astype(BF16)
            ch['cx'] = jnp.dot(cmat, ch['xb'], preferred_element_type=F32).astype(BF16)
        for ch in chains:
            ch['x'] = ch['x'] - jnp.dot(ch['xb'], ch['cx'], preferred_element_type=F32)
    for ch in chains:
        sol = jnp.dot(ch['x'].astype(BF16), ch['rhs'], preferred_element_type=F32)
        ch['u'] = sol[:, :LANES]
        ch['w'] = sol[:, LANES:].astype(BF16)
        ch['vn'] = [None] * (SB // C)
        ch['ost'] = [None] * (SB // C)

    by_key = {(ch['d'], ch['h'], ch['sb']): ch for ch in chains}
    st = {(d, h): state[d, h] for d in range(2) for h in range(C_HEADS)}
    for step in range(nc):
        work = []
        for (d, h), s_val in st.items():
            c = nc - 1 - step if d else step
            ch = by_key[(d, h, c // (SB // C))]
            lc = c % (SB // C)
            rs = slice(lc * C, (lc + 1) * C)
            stb = s_val.astype(BF16)
            vn = ch['u'][rs] - jnp.dot(ch['w'][rs], stb, preferred_element_type=F32)
            ch['ost'][lc] = jnp.dot(ch['qe'][rs], stb, preferred_element_type=F32)
            ch['vn'][lc] = vn
            last = lc * C if d else (lc + 1) * C - 1
            gl = ch['gcc'][last:last + 1, :]
            kd = (ch['k'][rs] * jnp.exp(gl - ch['gcc'][rs])).astype(BF16)
            work.append(((d, h), s_val * jnp.exp(gl), kd, vn.astype(BF16)))
        for key, decayed, kd, vnb in work:
            st[key] = decayed + lax.dot_general(kd, vnb, (((0,), (0,)), ((), ())), preferred_element_type=F32)
    for (d, h), s_val in st.items():
        state[d, h] = s_val
    for ch in chains:
        rs = slice(ch['sb'] * SB, (ch['sb'] + 1) * SB)
        sl = slice(ch['h'] * LANES, (ch['h'] + 1) * LANES)
        vn_all = jnp.concatenate(ch['vn'], axis=0).astype(BF16)
        refs[ch['d']][3][rs, sl] = (jnp.concatenate(ch['ost'], axis=0)
                                   + jnp.dot(ch['amat'], vn_all, preferred_element_type=F32))


def _gdn(q, k, v, gbt, B, S, tm=256):
    M = q.shape[0]
    nblk = S // tm
    fwd = lambda b, t: b * nblk + t
    rev = lambda b, t: b * nblk + (nblk - 1 - t)
    specs = []
    for blk in (fwd, rev):
        specs += [pl.BlockSpec((tm, C_WIDTH), lambda b, t, blk=blk: (blk(b, t), 0))] * 3
        specs += [pl.BlockSpec((4 * C_HEADS, tm), lambda b, t, blk=blk: (0, blk(b, t)))]
    return pl.pallas_call(
        functools.partial(_gdn_kernel, tm=tm),
        grid=(B, nblk),
        in_specs=specs,
        out_specs=[pl.BlockSpec((tm, C_WIDTH), lambda b, t: (fwd(b, t), 0)),
                   pl.BlockSpec((tm, C_WIDTH), lambda b, t: (rev(b, t), 0))],
        out_shape=[jax.ShapeDtypeStruct((M, C_WIDTH), F32)] * 2,
        scratch_shapes=[pltpu.VMEM((2, C_HEADS, C_HEAD_DIM, C_HEAD_DIM), F32)],
        compiler_params=_cparams("arbitrary", "arbitrary"),
        name="gdn",
    )(q, k, v, gbt, q, k, v, gbt)


def _band_kernel(q_ref, kp_ref, kc_ref, kn_ref, vp_ref, vc_ref, vn_ref, bias_ref, o_ref, lse_ref, *, tq, m_len):
    t = pl.program_id(1)
    QT = BAND_QT
    nkeys = QT + 2 * D_STEPS
    kwin = jnp.concatenate([kp_ref[...], kc_ref[...], kn_ref[...]], axis=0)
    vwin = jnp.concatenate([vp_ref[...], vc_ref[...], vn_ref[...]], axis=0)
    lane = lax.broadcasted_iota(jnp.int32, (QT, D_GW), 1)
    mine = [(lane // D_HEAD_DIM) == h for h in range(D_HEADS_PER_GROUP)]
    col = lax.broadcasted_iota(jnp.int32, (QT, nkeys), 1)
    tiles = []
    for i in range(tq // QT):
        q = q_ref[i * QT:(i + 1) * QT, :]
        kpos = t * tq + i * QT - D_STEPS + col
        valid = (kpos >= 0) & (kpos < m_len)
        kw = kwin[i * QT:i * QT + nkeys]
        for h in range(D_HEADS_PER_GROUP):
            qh = jnp.where(mine[h], q, jnp.zeros_like(q))
            s = lax.dot_general(qh, kw, (((1,), (1,)), ((), ())), preferred_element_type=F32) + bias_ref[h]
            tiles.append([i, h, jnp.where(valid, s, NEG_INF)])
    for tile in tiles:
        s = tile[2]
        m = jnp.max(s, axis=-1, keepdims=True)
        e = jnp.exp(s - m)
        l = jnp.sum(e, axis=-1, keepdims=True)
        tile[2] = (e * (1.0 / l)).astype(BF16)
        tile.append(m + jnp.log(l))
    for i in range(tq // QT):
        o = jnp.zeros((QT, D_GW), F32)
        lse_full = jnp.zeros((QT, D_GW), F32)
        vw = vwin[i * QT:i * QT + nkeys]
        for _, h, p, lse in tiles[i * D_HEADS_PER_GROUP:(i + 1) * D_HEADS_PER_GROUP]:
            o = jnp.where(mine[h], jnp.dot(p, vw, preferred_element_type=F32), o)
            lse_full = jnp.where(mine[h], lse, lse_full)
        o_ref[i * QT:(i + 1) * QT, :] = o
        lse_ref[i * QT:(i + 1) * QT, :] = lse_full


def _band(q, k, v, bias, Z, m_len):
    M = q.shape[0]
    tq = min(BAND_TQ, m_len)
    nq = m_len // tq
    r = tq // D_STEPS
    nhb = M // D_STEPS
    cur = lambda z, t: (z * nq + t, 0)
    prev = lambda z, t: (jnp.maximum((z * nq + t) * r - 1, 0), 0)
    nxt = lambda z, t: (jnp.minimum((z * nq + t + 1) * r, nhb - 1), 0)
    halo = lambda f: pl.BlockSpec((D_STEPS, D_GW), f)
    full = pl.BlockSpec((tq, D_GW), cur)
    return pl.pallas_call(
        functools.partial(_band_kernel, tq=tq, m_len=m_len),
        grid=(Z, nq),
        in_specs=[full, halo(prev), full, halo(nxt), halo(prev), full, halo(nxt),
                  pl.BlockSpec((D_HEADS_PER_GROUP, BAND_QT, BAND_QT + 2 * D_STEPS), lambda z, t: (0, 0, 0))],
        out_specs=[full, full],
        out_shape=[jax.ShapeDtypeStruct((M, D_GW), F32)] * 2,
        compiler_params=_cparams("parallel", "parallel"),
        name="band_attn",
    )(q, k, k, k, v, v, v, bias)


def _odd_out_kernel(x_ref, of_ref, or_ref, z_ref, cg_ref, o0_ref, o1_ref, o2_ref, l0_ref, l1_ref, l2_ref,
                    unperm_ref, wc_ref, wd_ref, *rest, final):
    *ff, out_ref = rest
    tm = x_ref.shape[0]
    oc = of_ref[...] + or_ref[...]
    z = z_ref[...]
    parts = []
    for h in range(C_HEADS):
        sl = slice(h * LANES, (h + 1) * LANES)
        parts.append(_rms(oc[:, sl], cg_ref[...]) * _silu(z[:, sl]))
    yc = jnp.concatenate(parts, axis=1).astype(BF16)

    def token_order(ref, gi):
        if D_GROUPS[gi][1] == 1:
            return ref[...]
        val = ref[...].reshape(tm, D_GW)
        hi = val.astype(BF16)
        lo = (val - hi.astype(F32)).astype(BF16)
        pt = unperm_ref[gi - 1]
        return jnp.dot(pt, hi, preferred_element_type=F32) + jnp.dot(pt, lo, preferred_element_type=F32)

    o0, o1, o2 = (token_order(r, gi) for gi, r in enumerate((o0_ref, o1_ref, o2_ref)))
    l0, l1, l2 = (token_order(r, gi) for gi, r in enumerate((l0_ref, l1_ref, l2_ref)))
    m = jnp.maximum(jnp.maximum(l0, l1), l2)
    e0, e1, e2 = jnp.exp(l0 - m), jnp.exp(l1 - m), jnp.exp(l2 - m)
    inv = 1.0 / (e0 + e1 + e2)
    yd = ((e0 * o0 + e1 * o1 + e2 * o2) * inv).astype(BF16)
    y = jnp.dot(yc, wc_ref[...], preferred_element_type=F32) + jnp.dot(yd, wd_ref[...], preferred_element_type=F32)
    out_ref[...] = _mlp_tail(x_ref[...] + y, ff, final)


def _odd_out(x, o_f, o_r, z, cg, os_, ls_, unperm, wc, wd, ff, final, B, S):
    tm = PERM_TM
    M = x.shape[0]
    row = lambda i: (i, 0)
    wide = pl.BlockSpec((tm, C_WIDTH), row)
    grp = [pl.BlockSpec((tm, D_GW), row) if dil == 1 else _class_major_spec(dil, S, tm) for _, dil in D_GROUPS]
    return pl.pallas_call(
        functools.partial(_odd_out_kernel, final=final),
        grid=(M // tm,),
        in_specs=[pl.BlockSpec((tm, D_MODEL), row), wide, wide, wide, _resident((1, LANES))] + grp + grp
        + [_resident(unperm.shape), _resident((C_WIDTH, D_MODEL)), _resident((D_GW, D_MODEL))] + _ff_specs(),
        out_specs=pl.BlockSpec((tm, D_MODEL), row),
        out_shape=jax.ShapeDtypeStruct((M, D_MODEL), F32),
        compiler_params=_cparams("parallel"),
        name="odd_out_mlp",
    )(x, o_f, o_r, z, cg, *os_, *ls_, unperm, wc, wd, *ff)


def _class_perms():
    mats = []
    for _, dil in D_GROUPS[1:]:
        out_row = np.arange(PERM_TM)
        src = (out_row % (PERM_TM // dil)) * dil + out_row // (PERM_TM // dil)
        mats.append(np.eye(PERM_TM, dtype=np.float32)[src])
    perm = np.stack(mats)
    return jnp.asarray(perm, BF16), jnp.asarray(perm.transpose(0, 2, 1), BF16)


def _t5_bucket(rel):
    nb = N_BUCKETS // 2
    max_exact = nb // 2
    n = np.abs(rel)
    large = max_exact + (np.log(np.maximum(n, 1) / max_exact) / math.log(MAX_DISTANCE / max_exact)
                         * (nb - max_exact)).astype(np.int64)
    large = np.minimum(large, nb - 1)
    return (np.where(rel > 0, nb, 0) + np.where(n < max_exact, n, large)).astype(np.int32)


def _band_bias(rel_bias):
    nkeys = BAND_QT + 2 * D_STEPS
    tiles = []
    for gi, (window, dil) in enumerate(D_GROUPS):
        steps = window // (2 * dil)
        assert steps == D_STEPS
        buckets = _t5_bucket(np.arange(-steps, steps + 1) * dil)
        hsl = slice(gi * D_HEADS_PER_GROUP, (gi + 1) * D_HEADS_PER_GROUP)
        b = rel_bias.astype(F32)[jnp.asarray(buckets)][:, hsl].T
        period = jnp.concatenate([b, jnp.full((D_HEADS_PER_GROUP, nkeys + 1 - b.shape[1]), NEG_INF, F32)], axis=1)
        flat = jnp.tile(period, (1, BAND_QT))[:, :BAND_QT * nkeys]
        tiles.append(flat.reshape(D_HEADS_PER_GROUP, BAND_QT, nkeys))
    return jnp.stack(tiles)


def _prep_odd(w_in, w_out, conv_w, a_log, dt_bias, o_gain, rel_bias):
    n_c = 4 * C_WIDTH
    n_ba = 4 * C_HEADS
    w = jnp.concatenate([w_in[:, :n_c], w_in[:, n_c + n_ba:], w_in[:, n_c:n_c + n_ba],
                         jnp.zeros((D_MODEL, LANES - n_ba), w_in.dtype)], axis=1).astype(BF16)
    pad = jnp.zeros((LANES - n_ba,), F32)
    nega = jnp.concatenate([jnp.zeros((2 * C_HEADS,), F32), -jnp.exp(a_log.astype(F32)).reshape(-1), pad])[None]
    dtb = jnp.concatenate([jnp.zeros((2 * C_HEADS,), F32), dt_bias.astype(F32).reshape(-1), pad])[None]
    return dict(w=w, nega=nega, dtb=dtb, cw=conv_w.astype(F32), cg=o_gain.astype(F32)[None],
                bias=_band_bias(rel_bias), wc=w_out[:C_WIDTH].astype(BF16), wd=w_out[C_WIDTH:].astype(BF16))


def _odd_layer(x, gain, p, ff, final, B, S):
    perm, unperm = _class_perms()
    (cq, ck, cv, z), dq, dk, dv, gb = _odd_proj(x, gain, p['w'], p['cw'], p['nega'], p['dtb'], perm, B, S)
    o_f, o_r = _gdn(cq, ck, cv, gb, B, S)
    outs, lses = [], []
    for gi, (_, dil) in enumerate(D_GROUPS):
        flat = lambda a: a.reshape(B * S, D_GW)
        o, lse = _band(flat(dq[gi]), flat(dk[gi]), flat(dv[gi]), p['bias'][gi], B * dil, S // dil)
        shape = (B * S, D_GW) if dil == 1 else (B, dil, S // dil, D_GW)
        outs.append(o.reshape(shape))
        lses.append(lse.reshape(shape))
    return _odd_out(x, o_f, o_r, z, p['cg'], outs, lses, unperm, p['wc'], p['wd'], ff, final, B, S)


def _trunk(x3, evens, odds, norm_mix, norm_ff, norm_final, w1, w2):
    B, S, _ = x3.shape
    x = x3.reshape(B * S, D_MODEL)
    cos, sin = _rope_tables(S)
    for layer in range(DEPTH):
        gain = norm_mix[layer][None]
        ff = (norm_ff[layer][None], w1[layer], w2[layer], norm_final[None])
        final = layer == DEPTH - 1
        if layer % 2 == 0:
            x = _even_layer(x, gain, evens[layer // 2], ff, final, cos, sin, B, S)
        else:
            x = _odd_layer(x, gain, odds[layer // 2], ff, final, B, S)
    return x.reshape(B, S, D_MODEL)


def kernel(x_prompt, x_sample, rel_bias, norm_mix, norm_ff, norm_final, w_ff1, w_ff2, w_in_e, w_out_e,
           a_qnorm, a_knorm, b_conv_w, b_conv_b, b_wr, b_br, b_wi, b_bi, b_lambda, w_in_o, w_out_o,
           c_conv_w, c_a_log, c_dt_bias, c_norm):
    evens = [_prep_even(w_in_e[j], w_out_e[j], a_qnorm[j], a_knorm[j], b_conv_w[j], b_conv_b[j],
                        b_wr[j], b_br[j], b_wi[j], b_bi[j], b_lambda[j]) for j in range((DEPTH + 1) // 2)]
    odds = [_prep_odd(w_in_o[j], w_out_o[j], c_conv_w[j], c_a_log[j], c_dt_bias[j], c_norm[j], rel_bias)
            for j in range(DEPTH // 2)]
    w1 = w_ff1.astype(BF16)
    w2 = w_ff2.astype(BF16)
    nm = norm_mix.astype(F32)
    nf = norm_ff.astype(F32)
    ng = norm_final.astype(F32)
    y_prompt = _trunk(x_prompt, evens, odds, nm, nf, ng, w1, w2)
    y_sample = _trunk(x_sample, evens, odds, nm, nf, ng, w1, w2)
    return (y_prompt, y_sample)
```

```python
import functools
import math

import numpy as np
import jax
import jax.numpy as jnp
from jax import lax
from jax.experimental import pallas as pl
from jax.experimental.pallas import tpu as pltpu

F32 = jnp.float32
BF16 = jnp.bfloat16

D_MODEL = 1024
D_FF = 4 * D_MODEL
DEPTH = 4
EPS = 1e-6
NEG_INF = -1e30
GRID_W = 64
LANES = 128

A_HEADS = 8
A_KV_HEADS = 2
A_HEAD_DIM = 64
A_Q = A_HEADS * A_HEAD_DIM
A_KV = A_KV_HEADS * A_HEAD_DIM
ROPE_THETA = 10000.0
B_WIDTH = 512
B_BLOCKS = 8
B_BLOCK_DIM = B_WIDTH // B_BLOCKS
RG_C = 8.0
C_HEADS = 4
C_HEAD_DIM = 128
C_WIDTH = C_HEADS * C_HEAD_DIM
C_CHUNK = 64
D_GROUPS = ((128, 1), (512, 4), (2048, 16))
D_HEADS_PER_GROUP = 4
D_HEAD_DIM = 64
D_NHEADS = len(D_GROUPS) * D_HEADS_PER_GROUP
D_WIDTH = D_NHEADS * D_HEAD_DIM
D_GW = D_HEADS_PER_GROUP * D_HEAD_DIM
D_STEPS = 64
N_BUCKETS = 32
MAX_DISTANCE = 1024
HALO = 8
Q_SCALE = A_HEAD_DIM ** -0.5 * math.log2(math.e)
MLP_TF = 1024
PROJ_ROWS = 256
PERM_TM = 512
BAND_QT = 128
BAND_TQ = 1024
GQA_TILE = 256
VT_PAD = 16
VT_ROWS = A_KV_HEADS * (A_HEAD_DIM + VT_PAD)

VMEM_LIMIT = 56 * 1024 * 1024


def _cparams(*sem):
    return pltpu.CompilerParams(dimension_semantics=sem, vmem_limit_bytes=VMEM_LIMIT)


def _rms(x, gain):
    return x * lax.rsqrt(jnp.mean(x * x, axis=-1, keepdims=True) + EPS) * gain


def _softplus(x):
    return jnp.maximum(x, 0.0) + jnp.log1p(jnp.exp(-jnp.abs(x)))


def _sigmoid(x):
    return 0.5 * jnp.tanh(0.5 * x) + 0.5


def _silu(x):
    return x * _sigmoid(x)


def _gelu_tanh(x):
    return 0.5 * x * (1.0 + jnp.tanh(math.sqrt(2.0 / math.pi) * (x + 0.044715 * (x * x * x))))


def _conv4(prev, cur, nxt, w, first, last):
    rows = cur.shape[0]
    prev = jnp.where(first, 0.0, prev)
    nxt = jnp.where(last, 0.0, nxt)
    full = jnp.concatenate([prev, cur, nxt], axis=0)
    y = full[HALO - 2:HALO - 2 + rows] * w[0:1]
    for j in range(1, 4):
        y = y + full[HALO - 2 + j:HALO - 2 + j + rows] * w[j:j + 1]
    return y


def _even_proj_kernel(x_ref, g_ref, w_ref, wvt_ref, qg_ref, kg_ref, cos_ref, sin_ref,
                      q_ref, k_ref, vt_ref, xr_ref, gr_ref):
    tm = x_ref.shape[0]
    parts = []
    for r0 in range(0, tm, PROJ_ROWS):
        xn = _rms(x_ref[r0:r0 + PROJ_ROWS, :], g_ref[...]).astype(BF16)
        proj = jnp.dot(xn, w_ref[...], preferred_element_type=F32)
        vt = lax.dot_general(wvt_ref[...], xn, (((1,), (1,)), ((), ())), preferred_element_type=F32)
        parts.append((r0, proj, vt))
    lane = lax.broadcasted_iota(jnp.int32, (PROJ_ROWS, LANES), 1)
    first_half = (lane % 32) < 16
    lo = lane < A_HEAD_DIM
    ones = jnp.ones((VT_PAD, LANES), BF16)
    for r0, proj, vt in parts:
        rows = slice(r0, r0 + PROJ_ROWS)
        cos = cos_ref[rows, :]
        sin = sin_ref[rows, :]

        def rope(t):
            swapped = jnp.where(first_half, pltpu.roll(t, LANES - 16, 1), pltpu.roll(t, 16, 1))
            return t * cos + swapped * sin

        vt = vt.astype(BF16)
        for c in range(PROJ_ROWS // LANES):
            cols = slice(c * LANES, (c + 1) * LANES)
            vt_ref[r0 // LANES + c] = jnp.concatenate(
                [vt[:A_HEAD_DIM, cols], ones, vt[A_HEAD_DIM:, cols], ones], axis=0)
        for h in range(A_HEADS):
            t = proj[:, h * LANES:(h + 1) * LANES]
            r = lax.rsqrt(jnp.sum(t * t, axis=-1, keepdims=True) * (1.0 / A_HEAD_DIM) + EPS)
            t = rope(t * r * qg_ref[:, h * LANES:(h + 1) * LANES])
            q_ref[rows, h * LANES:(h + 1) * LANES] = (t * Q_SCALE).astype(BF16)
        off = A_HEADS * LANES
        t = proj[:, off:off + LANES]
        t2 = t * t
        s_lo = jnp.sum(jnp.where(lo, t2, 0.0), axis=-1, keepdims=True)
        s_hi = jnp.sum(jnp.where(lo, 0.0, t2), axis=-1, keepdims=True)
        r = lax.rsqrt(jnp.where(lo, s_lo, s_hi) * (1.0 / A_HEAD_DIM) + EPS)
        k_ref[rows, :] = rope(t * r * kg_ref[...]).astype(BF16)
        off += LANES
        xr_ref[rows, :] = proj[:, off:off + B_WIDTH]
        off += B_WIDTH
        gr_ref[rows, :] = proj[:, off:off + B_WIDTH]


def _even_proj(x, gain, w, wvt, qg, kg, cos, sin, S, tm=512):
    M = x.shape[0]
    nS = S // tm
    n_in = w.shape[1]
    row = lambda i: (i, 0)
    const = lambda i: (0, 0)
    return pl.pallas_call(
        _even_proj_kernel,
        grid=(M // tm,),
        in_specs=[pl.BlockSpec((tm, D_MODEL), row), pl.BlockSpec((1, D_MODEL), const),
                  pl.BlockSpec((D_MODEL, n_in), const), pl.BlockSpec((A_KV, D_MODEL), const),
                  pl.BlockSpec((1, A_HEADS * LANES), const), pl.BlockSpec((1, LANES), const),
                  pl.BlockSpec((tm, LANES), lambda i: (i % nS, 0)),
                  pl.BlockSpec((tm, LANES), lambda i: (i % nS, 0))],
        out_specs=[pl.BlockSpec((tm, A_HEADS * LANES), row), pl.BlockSpec((tm, LANES), row),
                   pl.BlockSpec((tm // LANES, VT_ROWS, LANES), lambda i: (i, 0, 0)),
                   pl.BlockSpec((tm, B_WIDTH), row), pl.BlockSpec((tm, B_WIDTH), row)],
        out_shape=[jax.ShapeDtypeStruct((M, A_HEADS * LANES), BF16),
                   jax.ShapeDtypeStruct((M, LANES), BF16),
                   jax.ShapeDtypeStruct((M // LANES, VT_ROWS, LANES), BF16),
                   jax.ShapeDtypeStruct((M, B_WIDTH), F32), jax.ShapeDtypeStruct((M, B_WIDTH), F32)],
        compiler_params=_cparams("parallel"),
        name="even_proj",
    )(x, gain, w, wvt, qg, kg, cos, sin)


def _gqa_kernel(q_ref, k_ref, vt_ref, o_ref, m_ref, acc_ref, *, tq, kstep, nstep, unroll, lookahead):
    hpt = GQA_TILE // tq
    ntile = A_HEADS // hpt
    qt = [jnp.concatenate([q_ref[:, h * LANES:(h + 1) * LANES] for h in range(p * hpt, (p + 1) * hpt)], axis=0)
          for p in range(ntile)]
    m_ref[...] = jnp.full(m_ref.shape, NEG_INF, F32)
    acc_ref[...] = jnp.zeros(acc_ref.shape, F32)
    vrows = VT_ROWS // A_KV_HEADS

    per = kstep // LANES

    def scores(j, u, p):
        ks = k_ref[pl.ds(pl.multiple_of((j * unroll + u) * kstep, kstep), kstep), :]
        return lax.dot_general(ks, qt[p], (((1,), (1,)), ((), ())), preferred_element_type=F32)

    def update(j, u, p, st):
        g = (p * hpt) // (A_HEADS // A_KV_HEADS)
        m_old = m_ref[p]
        m_new = jnp.maximum(m_old, jnp.max(st, axis=0, keepdims=True))
        alpha = jnp.exp2(m_old - m_new)
        pt = jnp.exp2(st - m_new).astype(BF16)
        first = (j * unroll + u) * per
        vts = jnp.concatenate([vt_ref[first + i, g * vrows:(g + 1) * vrows, :] for i in range(per)], axis=1)
        acc_ref[p] = alpha * acc_ref[p] + jnp.dot(vts, pt, preferred_element_type=F32)
        m_ref[p] = m_new

    pairs = [(u, p) for u in range(unroll) for p in range(ntile)]
    niter = nstep // unroll

    def body(j, pending):
        pending = list(pending)
        j_next = jnp.minimum(j + 1, niter - 1)
        for idx in range(len(pairs)):
            ahead = idx + lookahead
            if ahead < len(pairs):
                pending.append(scores(j, *pairs[ahead]))
            else:
                pending.append(scores(j_next, *pairs[ahead - len(pairs)]))
            update(j, *pairs[idx], pending.pop(0))
        return tuple(pending)

    lax.fori_loop(0, niter, body, tuple(scores(0, *pairs[i]) for i in range(lookahead)))
    heads = []
    for p in range(ntile):
        acc = acc_ref[p]
        o = acc[:A_HEAD_DIM] / acc[A_HEAD_DIM:A_HEAD_DIM + 1]
        heads += [o[:, i * tq:(i + 1) * tq] for i in range(hpt)]
    for e in range(A_HEADS // 2):
        both = jnp.concatenate(heads[2 * e:2 * e + 2], axis=0)
        o_ref[:, e * LANES:(e + 1) * LANES] = jnp.transpose(both).astype(o_ref.dtype)


def _gqa(q, k, vt, B, S, kstep=256, lookahead=5):
    tq, unroll = (GQA_TILE, 4) if S <= 2048 else (GQA_TILE // 2, 8)
    M = q.shape[0]
    nq = S // tq
    nslab = S // LANES
    nstep = S // kstep
    unroll = min(unroll, nstep)
    kern = functools.partial(_gqa_kernel, tq=tq, kstep=kstep, nstep=nstep, unroll=unroll, lookahead=lookahead)
    return pl.pallas_call(
        kern,
        grid=(B, nq),
        in_specs=[pl.BlockSpec((tq, A_HEADS * LANES), lambda b, i: (b * nq + i, 0)),
                  pl.BlockSpec((S, LANES), lambda b, i: (b, 0)),
                  pl.BlockSpec((nslab, VT_ROWS, LANES), lambda b, i: (b, 0, 0))],
        out_specs=pl.BlockSpec((tq, A_Q), lambda b, i: (b * nq + i, 0)),
        out_shape=jax.ShapeDtypeStruct((M, A_Q), BF16),
        scratch_shapes=[pltpu.VMEM((A_HEADS * tq // GQA_TILE, 1, GQA_TILE), F32),
                        pltpu.VMEM((A_HEADS * tq // GQA_TILE, VT_ROWS // A_KV_HEADS, GQA_TILE), F32)],
        compiler_params=_cparams("parallel", "parallel"),
        name="gqa",
    )(q, k, vt)


def _rglru_kernel(xp_ref, xc_ref, xn_ref, yp_ref, yc_ref, yn_ref, cw_ref, cb_ref, wg_ref, bg_ref, lam_ref,
                  hf_ref, hr_ref, a_s, u_s, carry_s, *, tm, nblk):
    t = pl.program_id(1)

    @pl.when(t == 0)
    def _():
        carry_s[...] = jnp.zeros_like(carry_s)

    cw = cw_ref[...]
    ngrp = tm // 8
    row8 = lax.broadcasted_iota(jnp.int32, (ngrp, 8, B_WIDTH), 1)
    for d, (p_ref, c_ref, n_ref) in enumerate(((xp_ref, xc_ref, xn_ref), (yp_ref, yc_ref, yn_ref))):
        tt = t if d == 0 else nblk - 1 - t
        xc = _conv4(p_ref[...], c_ref[...], n_ref[...], cw, tt == 0, tt == nblk - 1) + cb_ref[...]
        xb = xc.astype(BF16)
        r = _sigmoid(jnp.dot(xb, wg_ref[2 * d], preferred_element_type=F32) + bg_ref[2 * d:2 * d + 1])
        i = _sigmoid(jnp.dot(xb, wg_ref[2 * d + 1], preferred_element_type=F32) + bg_ref[2 * d + 1:2 * d + 2])
        log_a = (-RG_C) * r * _softplus(-lam_ref[d:d + 1])
        a = jnp.exp(log_a).reshape(ngrp, 8, B_WIDTH)
        u = (jnp.sqrt(1.0 - jnp.exp(2.0 * log_a)) * (i * xc)).reshape(ngrp, 8, B_WIDTH)
        for sh in (1, 2, 4):
            keep = (row8 >= sh) if d == 0 else (row8 < 8 - sh)
            shift = sh if d == 0 else 8 - sh
            u = u + a * jnp.where(keep, pltpu.roll(u, shift, 1), 0.0)
            a = a * jnp.where(keep, pltpu.roll(a, shift, 1), 1.0)
        a_s[d] = a.reshape(tm, B_WIDTH)
        u_s[d] = u.reshape(tm, B_WIDTH)

    def body(gidx, carry):
        out = []
        for d, out_ref in enumerate((hf_ref, hr_ref)):
            gg = gidx if d == 0 else ngrp - 1 - gidx
            start = pl.multiple_of(gg * 8, 8)
            hg = u_s[d, pl.ds(start, 8), :] + a_s[d, pl.ds(start, 8), :] * carry[d]
            out_ref[pl.ds(start, 8), :] = hg
            edge = hg[7:8, :] if d == 0 else hg[0:1, :]
            out.append(jnp.broadcast_to(edge, (8, B_WIDTH)))
        return tuple(out)

    ends = lax.fori_loop(0, ngrp, body, (carry_s[0], carry_s[1]), unroll=4)
    carry_s[0] = ends[0]
    carry_s[1] = ends[1]


def _rglru(xr, cw, cb, wg, bg, lam, B, S, tm=256):
    M = xr.shape[0]
    nblk = S // tm
    r = tm // HALO
    nrb = M // HALO
    fwd = lambda b, t: b * nblk + t
    rev = lambda b, t: b * nblk + (nblk - 1 - t)
    specs = []
    for blk in (fwd, rev):
        specs += [pl.BlockSpec((HALO, B_WIDTH), lambda b, t, blk=blk: (jnp.maximum(blk(b, t) * r - 1, 0), 0)),
                  pl.BlockSpec((tm, B_WIDTH), lambda b, t, blk=blk: (blk(b, t), 0)),
                  pl.BlockSpec((HALO, B_WIDTH), lambda b, t, blk=blk: (jnp.minimum((blk(b, t) + 1) * r, nrb - 1), 0))]
    const2 = lambda b, t: (0, 0)
    const3 = lambda b, t: (0, 0, 0)
    kern = functools.partial(_rglru_kernel, tm=tm, nblk=nblk)
    return pl.pallas_call(
        kern,
        grid=(B, nblk),
        in_specs=specs + [pl.BlockSpec((4, B_WIDTH), const2), pl.BlockSpec((1, B_WIDTH), const2),
                          pl.BlockSpec((4, B_WIDTH, B_WIDTH), const3), pl.BlockSpec((4, B_WIDTH), const2),
                          pl.BlockSpec((2, B_WIDTH), const2)],
        out_specs=[pl.BlockSpec((tm, B_WIDTH), lambda b, t: (fwd(b, t), 0)),
                   pl.BlockSpec((tm, B_WIDTH), lambda b, t: (rev(b, t), 0))],
        out_shape=[jax.ShapeDtypeStruct((M, B_WIDTH), F32), jax.ShapeDtypeStruct((M, B_WIDTH), F32)],
        scratch_shapes=[pltpu.VMEM((2, tm, B_WIDTH), F32), pltpu.VMEM((2, tm, B_WIDTH), F32),
                        pltpu.VMEM((2, 8, B_WIDTH), F32)],
        compiler_params=_cparams("arbitrary", "arbitrary"),
        name="rglru",
    )(xr, xr, xr, xr, xr, xr, cw, cb, wg, bg, lam)


def _mlp_tail(x, ff, final):
    gff_ref, w1_ref, w2_ref, gfin_ref = ff
    xn = _rms(x, gff_ref[...]).astype(BF16)
    acc = x
    for c in range(D_FF // MLP_TF):
        h = jnp.dot(xn, w1_ref[:, c * MLP_TF:(c + 1) * MLP_TF], preferred_element_type=F32)
        h = jnp.square(jnp.maximum(h, 0.0)).astype(BF16)
        acc = acc + jnp.dot(h, w2_ref[c * MLP_TF:(c + 1) * MLP_TF, :], preferred_element_type=F32)
    return _rms(acc, gfin_ref[...]) if final else acc


def _resident(shape):
    return pl.BlockSpec(shape, lambda *_: (0,) * len(shape), pipeline_mode=pl.Buffered(1))


def _ff_specs():
    return [_resident((1, D_MODEL)), _resident((D_MODEL, D_FF)), _resident((D_FF, D_MODEL)),
            _resident((1, D_MODEL))]


def _even_out_kernel(x_ref, ya_ref, hf_ref, hr_ref, gr_ref, wa_ref, wb_ref, *rest, final):
    *ff, o_ref = rest
    yb = ((hf_ref[...] + hr_ref[...]) * _gelu_tanh(gr_ref[...])).astype(BF16)
    y = jnp.dot(ya_ref[...], wa_ref[...], preferred_element_type=F32)
    y = y + jnp.dot(yb, wb_ref[...], preferred_element_type=F32)
    o_ref[...] = _mlp_tail(x_ref[...] + y, ff, final)


def _even_out(x, ya, hf, hr, gr, wa, wb, ff, final, tm=512):
    M = x.shape[0]
    row = lambda i: (i, 0)
    return pl.pallas_call(
        functools.partial(_even_out_kernel, final=final),
        grid=(M // tm,),
        in_specs=[pl.BlockSpec((tm, D_MODEL), row), pl.BlockSpec((tm, A_Q), row),
                  pl.BlockSpec((tm, B_WIDTH), row), pl.BlockSpec((tm, B_WIDTH), row),
                  pl.BlockSpec((tm, B_WIDTH), row),
                  _resident((A_Q, D_MODEL)), _resident((B_WIDTH, D_MODEL))] + _ff_specs(),
        out_specs=pl.BlockSpec((tm, D_MODEL), row),
        out_shape=jax.ShapeDtypeStruct((M, D_MODEL), F32),
        compiler_params=_cparams("parallel"),
        name="even_out_mlp",
    )(x, ya, hf, hr, gr, wa, wb, *ff)


def _rope_tables(S):
    rows = S // GRID_W
    row = jnp.repeat(jnp.arange(rows, dtype=F32), GRID_W)
    col = jnp.tile(jnp.arange(GRID_W, dtype=F32), rows)
    n_freq = A_HEAD_DIM // 4
    inv = ROPE_THETA ** (-jnp.arange(n_freq, dtype=F32) / n_freq)
    ang_r = row[:, None] * inv
    ang_c = col[:, None] * inv
    cos = jnp.concatenate([jnp.cos(ang_r)] * 2 + [jnp.cos(ang_c)] * 2, axis=1)
    sin = jnp.concatenate([-jnp.sin(ang_r), jnp.sin(ang_r), -jnp.sin(ang_c), jnp.sin(ang_c)], axis=1)
    return jnp.tile(cos, (1, 2)), jnp.tile(sin, (1, 2))


def _prep_even(w_in, w_out, qn, kn, conv_w, conv_b, wr, br, wi, bi, lam):
    G = A_HEADS // A_KV_HEADS
    wq = w_in[:, :A_Q].reshape(D_MODEL, A_HEADS, A_HEAD_DIM)
    zeros = jnp.zeros_like(wq)
    half = (jnp.arange(A_HEADS) // G)[None, :, None]
    wq_pad = jnp.concatenate([jnp.where(half == 0, wq, zeros), jnp.where(half == 1, wq, zeros)], axis=-1)
    w = jnp.concatenate([wq_pad.reshape(D_MODEL, A_HEADS * LANES), w_in[:, A_Q:A_Q + A_KV],
                         w_in[:, A_Q + 2 * A_KV:]], axis=1).astype(BF16)
    wvt = w_in[:, A_Q + A_KV:A_Q + 2 * A_KV].T.astype(BF16)
    qg =jnp.tile(qn.astype(F32), 2 * A_HEADS)[None]
    kg = jnp.tile(kn.astype(F32), 2)[None]

    def dense(blocks):
        eye = jnp.eye(B_BLOCKS, dtype=blocks.dtype)
        return jnp.einsum('nde,nm->ndme', blocks, eye).reshape(B_WIDTH, B_WIDTH)

    wg = jnp.stack([dense(wr[0]), dense(wi[0]), dense(wr[1]), dense(wi[1])]).astype(BF16)
    bg = jnp.stack([br[0], bi[0], br[1], bi[1]]).astype(F32)
    return dict(w=w, wvt=wvt, qg=qg, kg=kg, cw=conv_w.astype(F32), cb=conv_b.astype(F32)[None], wg=wg, bg=bg,
                lam=lam.astype(F32), wa=w_out[:A_Q].astype(BF16), wb=w_out[A_Q:].astype(BF16))


def _even_layer(x, gain, p, ff, final, cos, sin, B, S):
    q, k, vt, xr, gr = _even_proj(x, gain, p['w'], p['wvt'], p['qg'], p['kg'], cos, sin, S)
    ya = _gqa(q, k, vt, B, S)
    hf, hr = _rglru(xr, p['cw'], p['cb'], p['wg'], p['bg'], p['lam'], B, S)
    return _even_out(x, ya, hf, hr, gr, p['wa'], p['wb'], ff, final)


def _odd_proj_kernel(xp_ref, x_ref, xn_ref, g_ref, w_ref, cw_ref, nega_ref, dtb_ref, perm_ref,
                     cq_ref, ck_ref, cv_ref, z_ref, *rest, n_s):
    *d_refs, gb_ref = rest
    tm = x_ref.shape[0]
    i = pl.program_id(0)
    gain = g_ref[...]
    xn = [_rms(x_ref[r0:r0 + PROJ_ROWS, :], gain).astype(BF16) for r0 in range(0, tm, PROJ_ROWS)]
    ext = jnp.concatenate([_rms(xp_ref[...], gain).astype(BF16)] + xn + [_rms(xn_ref[...], gain).astype(BF16)],
                          axis=0)
    n_c = 3 * C_WIDTH
    pq = jnp.dot(ext, w_ref[:, :n_c], preferred_element_type=F32)
    proj = jnp.concatenate([jnp.dot(part, w_ref[:, n_c:], preferred_element_type=F32) for part in xn], axis=0)
    y = _silu(_conv4(pq[:HALO], pq[HALO:HALO + tm], pq[HALO + tm:], cw_ref[...], i % n_s == 0, i % n_s == n_s - 1))
    for h in range(C_HEADS):
        for part, ref, scale in ((0, cq_ref, C_HEAD_DIM ** -0.5), (1, ck_ref, 1.0)):
            a = y[:, part * C_WIDTH + h * LANES:part * C_WIDTH + (h + 1) * LANES]
            a = a * lax.rsqrt(jnp.sum(a * a, axis=-1, keepdims=True) + EPS)
            ref[:, h * LANES:(h + 1) * LANES] = a * scale
    cv_ref[...] = y[:, 2 * C_WIDTH:]
    z_ref[...] = proj[:, :C_WIDTH]
    off = C_WIDTH
    for ti, scale in enumerate((D_HEAD_DIM ** -0.5, 1.0, 1.0)):
        for gi, (_, dil) in enumerate(D_GROUPS):
            xg = (proj[:, off + gi * D_GW:off + (gi + 1) * D_GW] * scale).astype(BF16)
            ref = d_refs[ti * len(D_GROUPS) + gi]
            if dil == 1:
                ref[...] = xg
            else:
                y = jnp.dot(perm_ref[gi - 1], xg, preferred_element_type=F32)
                ref[...] = y.reshape(dil, tm // dil, D_GW).astype(BF16)
        off += D_WIDTH
    t = proj[:, off:off + LANES]
    lane = lax.broadcasted_iota(jnp.int32, t.shape, 1)
    gate = jnp.where(lane < 2 * C_HEADS, _sigmoid(t), nega_ref[...] * _softplus(t + dtb_ref[...]))
    gb_ref[...] = jnp.transpose(gate)[:4 * C_HEADS]


def _class_major_spec(dil, S, tm):
    n_s = S // tm
    return pl.BlockSpec((None, dil, tm // dil, D_GW), lambda i: (i // n_s, 0, i % n_s, 0))


def _odd_proj(x, gain, w, cw, nega, dtb, perm, B, S):
    tm = PERM_TM
    M = x.shape[0]
    row = lambda i: (i, 0)
    const = lambda i: (0, 0)
    hb = tm // HALO
    last_hb = M // HALO - 1
    d_specs, d_shapes = [], []
    for _ in range(3):
        for _, dil in D_GROUPS:
            if dil == 1:
                d_specs.append(pl.BlockSpec((tm, D_GW), row))
                d_shapes.append(jax.ShapeDtypeStruct((M, D_GW), BF16))
            else:
                d_specs.append(_class_major_spec(dil, S, tm))
                d_shapes.append(jax.ShapeDtypeStruct((B, dil, S // dil, D_GW), BF16))
    wide = pl.BlockSpec((tm, C_WIDTH), row)
    outs = pl.pallas_call(
        functools.partial(_odd_proj_kernel, n_s=S // tm),
        grid=(M // tm,),
        in_specs=[pl.BlockSpec((HALO, D_MODEL), lambda i: (jnp.maximum(i * hb - 1, 0), 0)),
                  pl.BlockSpec((tm, D_MODEL), row),
                  pl.BlockSpec((HALO, D_MODEL), lambda i: (jnp.minimum((i + 1) * hb, last_hb), 0)),
                  pl.BlockSpec((1, D_MODEL), const), pl.BlockSpec((D_MODEL, w.shape[1]), const),
                  pl.BlockSpec((4, 3 * C_WIDTH), const),
                  pl.BlockSpec((1, LANES), const), pl.BlockSpec((1, LANES), const),
                  pl.BlockSpec(perm.shape, lambda i: (0, 0, 0))],
        out_specs=[wide, wide, wide, wide] + d_specs + [pl.BlockSpec((4 * C_HEADS, tm), lambda i: (0, i))],
        out_shape=[jax.ShapeDtypeStruct((M, C_WIDTH), F32)] * 4 + d_shapes
        + [jax.ShapeDtypeStruct((4 * C_HEADS, M), F32)],
        compiler_params=_cparams("parallel"),
        name="odd_proj",
    )(x, x, x, gain, w, cw, nega, dtb, perm)
    ng = len(D_GROUPS)
    return outs[:4], outs[4:4 + ng], outs[4 + ng:4 + 2 * ng], outs[4 + 2 * ng:4 + 3 * ng], outs[-1]


def _nt_dot(a, b):
    return lax.dot_general(a, b, (((1,), (1,)), ((), ())), preferred_element_type=F32)


def _gdn_kernel(qf_ref, kf_ref, vf_ref, gf_ref, qr_ref, kr_ref, vr_ref, gr_ref, of_ref, or_ref, state, *, tm):
    C = C_CHUNK
    SB = LANES
    nsb = tm // SB
    nc = tm // C
    t = pl.program_id(1)

    @pl.when(t == 0)
    def _():
        state[...] = jnp.zeros_like(state)

    ri = lax.broadcasted_iota(jnp.int32, (SB, SB), 0)
    ci = lax.broadcasted_iota(jnp.int32, (SB, SB), 1)
    same = (ri // C) == (ci // C)
    eye = jnp.where(ri == ci, 1.0, 0.0)
    incl, strict, levels = [], [], []
    for d in range(2):
        hi_i, lo_i = (ci, ri) if d else (ri, ci)
        incl.append(same & (hi_i >= lo_i))
        strict.append(same & (hi_i > lo_i))
        lv = []
        s = 1
        while s < C:
            lv.append((((hi_i // s) % 2) == 1) & ((lo_i // s) == (hi_i // s) - 1))
            s *= 2
        levels.append(lv)

    pos = lax.broadcasted_iota(jnp.int32, (4 * C_HEADS, tm), 1) % C
    rows, cols = [], []
    for d, g_ref in enumerate((gf_ref, gr_ref)):
        gt = g_ref[...]
        x = gt
        s = 1
        while s < C:
            if d == 0:
                x = x + jnp.where(pos >= s, pltpu.roll(x, s, 1), 0.0)
            else:
                x = x + jnp.where(pos < C - s, pltpu.roll(x, tm - s, 1), 0.0)
            s *= 2
        lo = d * C_HEADS
        rw = jnp.concatenate([gt[lo:lo + C_HEADS], x[2 * C_HEADS + lo:2 * C_HEADS + lo + C_HEADS]], axis=0)
        rows.append(rw)
        cols.append(jnp.transpose(jnp.concatenate([rw, jnp.zeros((LANES - 2 * C_HEADS, tm), F32)], axis=0)))

    refs = ((qf_ref, kf_ref, vf_ref, of_ref), (qr_ref, kr_ref, vr_ref, or_ref))
    chains = []
    for d in range(2):
        for h in range(C_HEADS):
            for sb in range(nsb):
                rs = slice(sb * SB, (sb + 1) * SB)
                sl = slice(h * LANES, (h + 1) * LANES)
                q = refs[d][0][rs, sl]
                k = refs[d][1][rs, sl]
                v = refs[d][2][rs, sl]
                beta = jnp.broadcast_to(cols[d][rs, h:h + 1], (SB, LANES))
                gcc = jnp.broadcast_to(cols[d][rs, C_HEADS + h:C_HEADS + h + 1], (SB, LANES))
                gcr = jnp.broadcast_to(rows[d][C_HEADS + h:C_HEADS + h + 1, rs], (SB, SB))
                dec = jnp.exp(jnp.where(incl[d], gcc - gcr, NEG_INF))
                eg = jnp.exp(gcc)
                kbf = k.astype(BF16)
                kb = k * beta
                lmat = jnp.where(strict[d], _nt_dot(kb.astype(BF16), kbf) * dec, 0.0)
                amat = jnp.where(incl[d], _nt_dot(q.astype(BF16), kbf) * dec, 0.0).astype(BF16)
                chains.append(dict(d=d, h=h, sb=sb, k=k, gcc=gcc, lmat=lmat, amat=amat,
                                   rhs=jnp.concatenate([v * beta, kb * eg], axis=1).astype(BF16),
                                   qe=(q * eg).astype(BF16)))

    for ch in chains:
        ch['x'] = eye - jnp.where(levels[ch['d']][0], ch['lmat'], 0.0)
    for li in range(1, len(levels[0])):
        for ch in chains:
            ch['xb'] = ch['x'].astype(BF16)
            cmat = jnp.where(levels[ch['d']][li], ch['lmat'], 0.0).astype(BF16)
            ch['cx'] = jnp.dot(cmat, ch['xb'], preferred_element_type=F32).astype(BF16)
        for ch in chains:
            ch['x'] = ch['x'] - jnp.dot(ch['xb'], ch['cx'], preferred_element_type=F32)
    for ch in chains:
        sol = jnp.dot(ch['x'].astype(BF16), ch['rhs'], preferred_element_type=F32)
        ch['u'] = sol[:, :LANES]
        ch['w'] = sol[:, LANES:].astype(BF16)
        ch['vn'] = [None] * (SB // C)
        ch['ost'] = [None] * (SB // C)

    by_key = {(ch['d'], ch['h'], ch['sb']): ch for ch in chains}
    st = {(d, h): state[d, h] for d in range(2) for h in range(C_HEADS)}
    for step in range(nc):
        work = []
        for (d, h), s_val in st.items():
            c = nc - 1 - step if d else step
            ch = by_key[(d, h, c // (SB // C))]
            lc = c % (SB // C)
            rs = slice(lc * C, (lc + 1) * C)
            stb = s_val.astype(BF16)
            vn = ch['u'][rs] - jnp.dot(ch['w'][rs], stb, preferred_element_type=F32)
            ch['ost'][lc] = jnp.dot(ch['qe'][rs], stb, preferred_element_type=F32)
            ch['vn'][lc] = vn
            last = lc * C if d else (lc + 1) * C - 1
            gl = ch['gcc'][last:last + 1, :]
            kd = (ch['k'][rs] * jnp.exp(gl - ch['gcc'][rs])).astype(BF16)
            work.append(((d, h), s_val * jnp.exp(gl), kd, vn.astype(BF16)))
        for key, decayed, kd, vnb in work:
            st[key] = decayed + lax.dot_general(kd, vnb, (((0,), (0,)), ((), ())), preferred_element_type=F32)
    for (d, h), s_val in st.items():
        state[d, h] = s_val
    for ch in chains:
        rs = slice(ch['sb'] * SB, (ch['sb'] + 1) * SB)
        sl = slice(ch['h'] * LANES, (ch['h'] + 1) * LANES)
        vn_all = jnp.concatenate(ch['vn'], axis=0).astype(BF16)
        refs[ch['d']][3][rs, sl] = (jnp.concatenate(ch['ost'], axis=0)
                                   + jnp.dot(ch['amat'], vn_all, preferred_element_type=F32))


def _gdn(q, k, v, gbt, B, S, tm=256):
    M = q.shape[0]
    nblk = S // tm
    fwd = lambda b, t: b * nblk + t
    rev = lambda b, t: b * nblk + (nblk - 1 - t)
    specs = []
    for blk in (fwd, rev):
        specs += [pl.BlockSpec((tm, C_WIDTH), lambda b, t, blk=blk: (blk(b, t), 0))] * 3
        specs += [pl.BlockSpec((4 * C_HEADS, tm), lambda b, t, blk=blk: (0, blk(b, t)))]
    return pl.pallas_call(
        functools.partial(_gdn_kernel, tm=tm),
        grid=(B, nblk),
        in_specs=specs,
        out_specs=[pl.BlockSpec((tm, C_WIDTH), lambda b, t: (fwd(b, t), 0)),
                   pl.BlockSpec((tm, C_WIDTH), lambda b, t: (rev(b, t), 0))],
        out_shape=[jax.ShapeDtypeStruct((M, C_WIDTH), F32)] * 2,
        scratch_shapes=[pltpu.VMEM((2, C_HEADS, C_HEAD_DIM, C_HEAD_DIM), F32)],
        compiler_params=_cparams("arbitrary", "arbitrary"),
        name="gdn",
    )(q, k, v, gbt, q, k, v, gbt)


def _band_kernel(q_ref, kp_ref, kc_ref, kn_ref, vp_ref, vc_ref, vn_ref, bias_ref, o_ref, lse_ref, *, tq, m_len):
    t = pl.program_id(1)
    QT = BAND_QT
    nkeys = QT + 2 * D_STEPS
    kwin = jnp.concatenate([kp_ref[...], kc_ref[...], kn_ref[...]], axis=0)
    vwin = jnp.concatenate([vp_ref[...], vc_ref[...], vn_ref[...]], axis=0)
    lane = lax.broadcasted_iota(jnp.int32, (QT, D_GW), 1)
    mine = [(lane // D_HEAD_DIM) == h for h in range(D_HEADS_PER_GROUP)]
    col = lax.broadcasted_iota(jnp.int32, (QT, nkeys), 1)
    tiles = []
    for i in range(tq // QT):
        q = q_ref[i * QT:(i + 1) * QT, :]
        kpos = t * tq + i * QT - D_STEPS + col
        valid = (kpos >= 0) & (kpos < m_len)
        kw = kwin[i * QT:i * QT + nkeys]
        for h in range(D_HEADS_PER_GROUP):
            qh = jnp.where(mine[h], q, jnp.zeros_like(q))
            s = lax.dot_general(qh, kw, (((1,), (1,)), ((), ())), preferred_element_type=F32) + bias_ref[h]
            tiles.append([i, h, jnp.where(valid, s, NEG_INF)])
    for tile in tiles:
        s = tile[2]
        m = jnp.max(s, axis=-1, keepdims=True)
        e = jnp.exp(s - m)
        l = jnp.sum(e, axis=-1, keepdims=True)
        tile[2] = (e * (1.0 / l)).astype(BF16)
        tile.append(m + jnp.log(l))
    for i in range(tq // QT):
        o = jnp.zeros((QT, D_GW), F32)
        lse_full = jnp.zeros((QT, D_GW), F32)
        vw = vwin[i * QT:i * QT + nkeys]
        for _, h, p, lse in tiles[i * D_HEADS_PER_GROUP:(i + 1) * D_HEADS_PER_GROUP]:
            o = jnp.where(mine[h], jnp.dot(p, vw, preferred_element_type=F32), o)
            lse_full = jnp.where(mine[h], lse, lse_full)
        o_ref[i * QT:(i + 1) * QT, :] = o
        lse_ref[i * QT:(i + 1) * QT, :] = lse_full


def _band(q, k, v, bias, Z, m_len):
    M = q.shape[0]
    tq = min(BAND_TQ, m_len)
    nq = m_len // tq
    r = tq // D_STEPS
    nhb = M // D_STEPS
    cur = lambda z, t: (z * nq + t, 0)
    prev = lambda z, t: (jnp.maximum((z * nq + t) * r - 1, 0), 0)
    nxt = lambda z, t: (jnp.minimum((z * nq + t + 1) * r, nhb - 1), 0)
    halo = lambda f: pl.BlockSpec((D_STEPS, D_GW), f)
    full = pl.BlockSpec((tq, D_GW), cur)
    return pl.pallas_call(
        functools.partial(_band_kernel, tq=tq, m_len=m_len),
        grid=(Z, nq),
        in_specs=[full, halo(prev), full, halo(nxt), halo(prev), full, halo(nxt),
                  pl.BlockSpec((D_HEADS_PER_GROUP, BAND_QT, BAND_QT + 2 * D_STEPS), lambda z, t: (0, 0, 0))],
        out_specs=[full, full],
        out_shape=[jax.ShapeDtypeStruct((M, D_GW), F32)] * 2,
        compiler_params=_cparams("parallel", "parallel"),
        name="band_attn",
    )(q, k, k, k, v, v, v, bias)


def _odd_out_kernel(x_ref, of_ref, or_ref, z_ref, cg_ref, o0_ref, o1_ref, o2_ref, l0_ref, l1_ref, l2_ref,
                    unperm_ref, wc_ref, wd_ref, *rest, final):
    *ff, out_ref = rest
    tm = x_ref.shape[0]
    oc = of_ref[...] + or_ref[...]
    z = z_ref[...]
    parts = []
    for h in range(C_HEADS):
        sl = slice(h * LANES, (h + 1) * LANES)
        parts.append(_rms(oc[:, sl], cg_ref[...]) * _silu(z[:, sl]))
    yc = jnp.concatenate(parts, axis=1).astype(BF16)

    def token_order(ref, gi):
        if D_GROUPS[gi][1] == 1:
            return ref[...]
        val = ref[...].reshape(tm, D_GW)
        hi = val.astype(BF16)
        lo = (val - hi.astype(F32)).astype(BF16)
        pt = unperm_ref[gi - 1]
        return jnp.dot(pt, hi, preferred_element_type=F32) + jnp.dot(pt, lo, preferred_element_type=F32)

    o0, o1, o2 = (token_order(r, gi) for gi, r in enumerate((o0_ref, o1_ref, o2_ref)))
    l0, l1, l2 = (token_order(r, gi) for gi, r in enumerate((l0_ref, l1_ref, l2_ref)))
    m = jnp.maximum(jnp.maximum(l0, l1), l2)
    e0, e1, e2 = jnp.exp(l0 - m), jnp.exp(l1 - m), jnp.exp(l2 - m)
    inv = 1.0 / (e0 + e1 + e2)
    yd = ((e0 * o0 + e1 * o1 + e2 * o2) * inv).astype(BF16)
    y = jnp.dot(yc, wc_ref[...], preferred_element_type=F32) + jnp.dot(yd, wd_ref[...], preferred_element_type=F32)
    out_ref[...] = _mlp_tail(x_ref[...] + y, ff, final)


def _odd_out(x, o_f, o_r, z, cg, os_, ls_, unperm, wc, wd, ff, final, B, S):
    tm = PERM_TM
    M = x.shape[0]
    row = lambda i: (i, 0)
    wide = pl.BlockSpec((tm, C_WIDTH), row)
    grp = [pl.BlockSpec((tm, D_GW), row) if dil == 1 else _class_major_spec(dil, S, tm) for _, dil in D_GROUPS]
    return pl.pallas_call(
        functools.partial(_odd_out_kernel, final=final),
        grid=(M // tm,),
        in_specs=[pl.BlockSpec((tm, D_MODEL), row), wide, wide, wide, _resident((1, LANES))] + grp + grp
        + [_resident(unperm.shape), _resident((C_WIDTH, D_MODEL)), _resident((D_GW, D_MODEL))] + _ff_specs(),
        out_specs=pl.BlockSpec((tm, D_MODEL), row),
        out_shape=jax.ShapeDtypeStruct((M, D_MODEL), F32),
        compiler_params=_cparams("parallel"),
        name="odd_out_mlp",
    )(x, o_f, o_r, z, cg, *os_, *ls_, unperm, wc, wd, *ff)


def _class_perms():
    mats = []
    for _, dil in D_GROUPS[1:]:
        out_row = np.arange(PERM_TM)
        src = (out_row % (PERM_TM // dil)) * dil + out_row // (PERM_TM // dil)
        mats.append(np.eye(PERM_TM, dtype=np.float32)[src])
    perm = np.stack(mats)
    return jnp.asarray(perm, BF16), jnp.asarray(perm.transpose(0, 2, 1), BF16)


def _t5_bucket(rel):
    nb = N_BUCKETS // 2
    max_exact = nb // 2
    n = np.abs(rel)
    large = max_exact + (np.log(np.maximum(n, 1) / max_exact) / math.log(MAX_DISTANCE / max_exact)
                         * (nb - max_exact)).astype(np.int64)
    large = np.minimum(large, nb - 1)
    return (np.where(rel > 0, nb, 0) + np.where(n < max_exact, n, large)).astype(np.int32)


def _band_bias(rel_bias):
    nkeys = BAND_QT + 2 * D_STEPS
    tiles = []
    for gi, (window, dil) in enumerate(D_GROUPS):
        steps = window // (2 * dil)
        assert steps == D_STEPS
        buckets = _t5_bucket(np.arange(-steps, steps + 1) * dil)
        hsl = slice(gi * D_HEADS_PER_GROUP, (gi + 1) * D_HEADS_PER_GROUP)
        b = rel_bias.astype(F32)[jnp.asarray(buckets)][:, hsl].T
        period = jnp.concatenate([b, jnp.full((D_HEADS_PER_GROUP, nkeys + 1 - b.shape[1]), NEG_INF, F32)], axis=1)
        flat = jnp.tile(period, (1, BAND_QT))[:, :BAND_QT * nkeys]
        tiles.append(flat.reshape(D_HEADS_PER_GROUP, BAND_QT, nkeys))
    return jnp.stack(tiles)


def _prep_odd(w_in, w_out, conv_w, a_log, dt_bias, o_gain, rel_bias):
    n_c = 4 * C_WIDTH
    n_ba = 4 * C_HEADS
    w = jnp.concatenate([w_in[:, :n_c], w_in[:, n_c + n_ba:], w_in[:, n_c:n_c + n_ba],
                         jnp.zeros((D_MODEL, LANES - n_ba), w_in.dtype)], axis=1).astype(BF16)
    pad = jnp.zeros((LANES - n_ba,), F32)
    nega = jnp.concatenate([jnp.zeros((2 * C_HEADS,), F32), -jnp.exp(a_log.astype(F32)).reshape(-1), pad])[None]
    dtb = jnp.concatenate([jnp.zeros((2 * C_HEADS,), F32), dt_bias.astype(F32).reshape(-1), pad])[None]
    return dict(w=w, nega=nega, dtb=dtb, cw=conv_w.astype(F32), cg=o_gain.astype(F32)[None],
                bias=_band_bias(rel_bias), wc=w_out[:C_WIDTH].astype(BF16), wd=w_out[C_WIDTH:].astype(BF16))


def _odd_layer(x, gain, p, ff, final, B, S):
    perm, unperm = _class_perms()
    (cq, ck, cv, z), dq, dk, dv, gb = _odd_proj(x, gain, p['w'], p['cw'], p['nega'], p['dtb'], perm, B, S)
    o_f, o_r = _gdn(cq, ck, cv, gb, B, S)
    outs, lses = [], []
    for gi, (_, dil) in enumerate(D_GROUPS):
        flat = lambda a: a.reshape(B * S, D_GW)
        o, lse = _band(flat(dq[gi]), flat(dk[gi]), flat(dv[gi]), p['bias'][gi], B * dil, S // dil)
        shape = (B * S, D_GW) if dil == 1 else (B, dil, S // dil, D_GW)
        outs.append(o.reshape(shape))
        lses.append(lse.reshape(shape))
    return _odd_out(x, o_f, o_r, z, p['cg'], outs, lses, unperm, p['wc'], p['wd'], ff, final, B, S)


def _trunk(x3, evens, odds, norm_mix, norm_ff, norm_final, w1, w2):
    B, S, _ = x3.shape
    x = x3.reshape(B * S, D_MODEL)
    cos, sin = _rope_tables(S)
    for layer in range(DEPTH):
        gain = norm_mix[layer][None]
        ff = (norm_ff[layer][None], w1[layer], w2[layer], norm_final[None])
        final = layer == DEPTH - 1
        if layer % 2 == 0:
            x = _even_layer(x, gain, evens[layer // 2], ff, final, cos, sin, B, S)
        else:
            x = _odd_layer(x, gain, odds[layer // 2], ff, final, B, S)
    return x.reshape(B, S, D_MODEL)


def kernel(x_prompt, x_sample, rel_bias, norm_mix, norm_ff, norm_final, w_ff1, w_ff2, w_in_e, w_out_e,
           a_qnorm, a_knorm, b_conv_w, b_conv_b, b_wr, b_br, b_wi, b_bi, b_lambda, w_in_o, w_out_o,
           c_conv_w, c_a_log, c_dt_bias, c_norm):
    evens = [_prep_even(w_in_e[j], w_out_e[j], a_qnorm[j], a_knorm[j], b_conv_w[j], b_conv_b[j],
                        b_wr[j], b_br[j], b_wi[j], b_bi[j], b_lambda[j]) for j in range((DEPTH + 1) // 2)]
    odds = [_prep_odd(w_in_o[j], w_out_o[j], c_conv_w[j], c_a_log[j], c_dt_bias[j], c_norm[j], rel_bias)
            for j in range(DEPTH // 2)]
    w1 = w_ff1.astype(BF16)
    w2 = w_ff2.astype(BF16)
    nm = norm_mix.astype(F32)
    nf = norm_ff.astype(F32)
    ng = norm_final.astype(F32)
    y_prompt = _trunk(x_prompt, evens, odds, nm, nf, ng, w1, w2)
    y_sample = _trunk(x_sample, evens, odds, nm, nf, ng, w1, w2)
    return (y_prompt, y_sample)
```

```python
import functools
import math

import numpy as np
import jax
import jax.numpy as jnp
from jax import lax
from jax.experimental import pallas as pl
from jax.experimental.pallas import tpu as pltpu

F32 = jnp.float32
BF16 = jnp.bfloat16

D_MODEL = 1024
D_FF = 4 * D_MODEL
DEPTH = 4
EPS = 1e-6
NEG_INF = -1e30
GRID_W = 64
LANES = 128

A_HEADS = 8
A_KV_HEADS = 2
A_HEAD_DIM = 64
A_Q = A_HEADS * A_HEAD_DIM
A_KV = A_KV_HEADS * A_HEAD_DIM
ROPE_THETA = 10000.0
B_WIDTH = 512
B_BLOCKS = 8
B_BLOCK_DIM = B_WIDTH // B_BLOCKS
RG_C = 8.0
C_HEADS = 4
C_HEAD_DIM = 128
C_WIDTH = C_HEADS * C_HEAD_DIM
C_CHUNK = 64
D_GROUPS = ((128, 1), (512, 4), (2048, 16))
D_HEADS_PER_GROUP = 4
D_HEAD_DIM = 64
D_NHEADS = len(D_GROUPS) * D_HEADS_PER_GROUP
D_WIDTH = D_NHEADS * D_HEAD_DIM
D_GW = D_HEADS_PER_GROUP * D_HEAD_DIM
D_STEPS = 64
N_BUCKETS = 32
MAX_DISTANCE = 1024
HALO = 8
Q_SCALE = A_HEAD_DIM ** -0.5 * math.log2(math.e)
MLP_TF = 1024
PROJ_ROWS = 256
PERM_TM = 512
BAND_QT = 128
BAND_TQ = 1024
GQA_TILE = 256
VT_PAD = 16
VT_ROWS = A_KV_HEADS * (A_HEAD_DIM + VT_PAD)

VMEM_LIMIT = 56 * 1024 * 1024


def _cparams(*sem):
    return pltpu.CompilerParams(dimension_semantics=sem, vmem_limit_bytes=VMEM_LIMIT)


def _rms(x, gain):
    return x * lax.rsqrt(jnp.mean(x * x, axis=-1, keepdims=True) + EPS) * gain


def _softplus(x):
    return jnp.maximum(x, 0.0) + jnp.log1p(jnp.exp(-jnp.abs(x)))


def _sigmoid(x):
    return 0.5 * jnp.tanh(0.5 * x) + 0.5


def _silu(x):
    return x * _sigmoid(x)


def _gelu_tanh(x):
    return 0.5 * x * (1.0 + jnp.tanh(math.sqrt(2.0 / math.pi) * (x + 0.044715 * (x * x * x))))


def _conv4(prev, cur, nxt, w, first, last):
    rows = cur.shape[0]
    prev = jnp.where(first, 0.0, prev)
    nxt = jnp.where(last, 0.0, nxt)
    full = jnp.concatenate([prev, cur, nxt], axis=0)
    y = full[HALO - 2:HALO - 2 + rows] * w[0:1]
    for j in range(1, 4):
        y = y + full[HALO - 2 + j:HALO - 2 + j + rows] * w[j:j + 1]
    return y


def _even_proj_kernel(x_ref, g_ref, w_ref, wvt_ref, qg_ref, kg_ref, cos_ref, sin_ref,
                      q_ref, k_ref, vt_ref, xr_ref, gr_ref):
    tm = x_ref.shape[0]
    parts = []
    for r0 in range(0, tm, PROJ_ROWS):
        xn = _rms(x_ref[r0:r0 + PROJ_ROWS, :], g_ref[...]).astype(BF16)
        proj = jnp.dot(xn, w_ref[...], preferred_element_type=F32)
        vt = lax.dot_general(wvt_ref[...], xn, (((1,), (1,)), ((), ())), preferred_element_type=F32)
        parts.append((r0, proj, vt))
    lane = lax.broadcasted_iota(jnp.int32, (PROJ_ROWS, LANES), 1)
    first_half = (lane % 32) < 16
    lo = lane < A_HEAD_DIM
    ones = jnp.ones((VT_PAD, LANES), BF16)
    for r0, proj, vt in parts:
        rows = slice(r0, r0 + PROJ_ROWS)
        cos = cos_ref[rows, :]
        sin = sin_ref[rows, :]

        def rope(t):
            swapped = jnp.where(first_half, pltpu.roll(t, LANES - 16, 1), pltpu.roll(t, 16, 1))
            return t * cos + swapped * sin

        vt = vt.astype(BF16)
        for c in range(PROJ_ROWS // LANES):
            cols = slice(c * LANES, (c + 1) * LANES)
            vt_ref[r0 // LANES + c] = jnp.concatenate(
                [vt[:A_HEAD_DIM, cols], ones, vt[A_HEAD_DIM:, cols], ones], axis=0)
        for h in range(A_HEADS):
            t = proj[:, h * LANES:(h + 1) * LANES]
            r = lax.rsqrt(jnp.sum(t * t, axis=-1, keepdims=True) * (1.0 / A_HEAD_DIM) + EPS)
            t = rope(t * r * qg_ref[:, h * LANES:(h + 1) * LANES])
            q_ref[rows, h * LANES:(h + 1) * LANES] = (t * Q_SCALE).astype(BF16)
        off = A_HEADS * LANES
        t = proj[:, off:off + LANES]
        t2 = t * t
        s_lo = jnp.sum(jnp.where(lo, t2, 0.0), axis=-1, keepdims=True)
        s_hi = jnp.sum(jnp.where(lo, 0.0, t2), axis=-1, keepdims=True)
        r = lax.rsqrt(jnp.where(lo, s_lo, s_hi) * (1.0 / A_HEAD_DIM) + EPS)
        k_ref[rows, :] = rope(t * r * kg_ref[...]).astype(BF16)
        off += LANES
        xr_ref[rows, :] = proj[:, off:off + B_WIDTH]
        off += B_WIDTH
        gr_ref[rows, :] = proj[:, off:off + B_WIDTH]


def _even_proj(x, gain, w, wvt, qg, kg, cos, sin, S, tm=512):
    M = x.shape[0]
    nS = S // tm
    n_in = w.shape[1]
    row = lambda i: (i, 0)
    const = lambda i: (0, 0)
    return pl.pallas_call(
        _even_proj_kernel,
        grid=(M // tm,),
        in_specs=[pl.BlockSpec((tm, D_MODEL), row), pl.BlockSpec((1, D_MODEL), const),
                  pl.BlockSpec((D_MODEL, n_in), const), pl.BlockSpec((A_KV, D_MODEL), const),
                  pl.BlockSpec((1, A_HEADS * LANES), const), pl.BlockSpec((1, LANES), const),
                  pl.BlockSpec((tm, LANES), lambda i: (i % nS, 0)),
                  pl.BlockSpec((tm, LANES), lambda i: (i % nS, 0))],
        out_specs=[pl.BlockSpec((tm, A_HEADS * LANES), row), pl.BlockSpec((tm, LANES), row),
                   pl.BlockSpec((tm // LANES, VT_ROWS, LANES), lambda i: (i, 0, 0)),
                   pl.BlockSpec((tm, B_WIDTH), row), pl.BlockSpec((tm, B_WIDTH), row)],
        out_shape=[jax.ShapeDtypeStruct((M, A_HEADS * LANES), BF16),
                   jax.ShapeDtypeStruct((M, LANES), BF16),
                   jax.ShapeDtypeStruct((M // LANES, VT_ROWS, LANES), BF16),
                   jax.ShapeDtypeStruct((M, B_WIDTH), F32), jax.ShapeDtypeStruct((M, B_WIDTH), F32)],
        compiler_params=_cparams("parallel"),
        name="even_proj",
    )(x, gain, w, wvt, qg, kg, cos, sin)


def _gqa_kernel(q_ref, k_ref, vt_ref, o_ref, m_ref, acc_ref, *, tq, kstep, nstep, unroll, lookahead):
    hpt = GQA_TILE // tq
    ntile = A_HEADS // hpt
    qt = [jnp.concatenate([q_ref[:, h * LANES:(h + 1) * LANES] for h in range(p * hpt, (p + 1) * hpt)], axis=0)
          for p in range(ntile)]
    m_ref[...] = jnp.full(m_ref.shape, NEG_INF, F32)
    acc_ref[...] = jnp.zeros(acc_ref.shape, F32)
    vrows = VT_ROWS // A_KV_HEADS

    per = kstep // LANES

    def scores(j, u, p):
        ks = k_ref[pl.ds(pl.multiple_of((j * unroll + u) * kstep, kstep), kstep), :]
        return lax.dot_general(ks, qt[p], (((1,), (1,)), ((), ())), preferred_element_type=F32)

    def update(j, u, p, st):
        g = (p * hpt) // (A_HEADS // A_KV_HEADS)
        m_old = m_ref[p]
        m_new = jnp.maximum(m_old, jnp.max(st, axis=0, keepdims=True))
        alpha = jnp.exp2(m_old - m_new)
        pt = jnp.exp2(st - m_new).astype(BF16)
        first = (j * unroll + u) * per
        vts = jnp.concatenate([vt_ref[first + i, g * vrows:(g + 1) * vrows, :] for i in range(per)], axis=1)
        acc_ref[p] = alpha * acc_ref[p] + jnp.dot(vts, pt, preferred_element_type=F32)
        m_ref[p] = m_new

    pairs = [(u, p) for u in range(unroll) for p in range(ntile)]
    niter = nstep // unroll

    def body(j, pending):
        pending = list(pending)
        j_next = jnp.minimum(j + 1, niter - 1)
        for idx in range(len(pairs)):
            ahead = idx + lookahead
            if ahead < len(pairs):
                pending.append(scores(j, *pairs[ahead]))
            else:
                pending.append(scores(j_next, *pairs[ahead - len(pairs)]))
            update(j, *pairs[idx], pending.pop(0))
        return tuple(pending)

    lax.fori_loop(0, niter, body, tuple(scores(0, *pairs[i]) for i in range(lookahead)))
    heads = []
    for p in range(ntile):
        acc = acc_ref[p]
        o = acc[:A_HEAD_DIM] / acc[A_HEAD_DIM:A_HEAD_DIM + 1]
        heads += [o[:, i * tq:(i + 1) * tq] for i in range(hpt)]
    for e in range(A_HEADS // 2):
        both = jnp.concatenate(heads[2 * e:2 * e + 2], axis=0)
        o_ref[:, e * LANES:(e + 1) * LANES] = jnp.transpose(both).astype(o_ref.dtype)


def _gqa(q, k, vt, B, S, kstep=256, lookahead=5):
    tq, unroll = (GQA_TILE, 8) if S <= 2048 else (GQA_TILE // 2, 32)
    M = q.shape[0]
    nq = S // tq
    nslab = S // LANES
    nstep = S // kstep
    unroll = min(unroll, nstep)
    kern = functools.partial(_gqa_kernel, tq=tq, kstep=kstep, nstep=nstep, unroll=unroll, lookahead=lookahead)
    return pl.pallas_call(
        kern,
        grid=(B, nq),
        in_specs=[pl.BlockSpec((tq, A_HEADS * LANES), lambda b, i: (b * nq + i, 0)),
                  pl.BlockSpec((S, LANES), lambda b, i: (b, 0)),
                  pl.BlockSpec((nslab, VT_ROWS, LANES), lambda b, i: (b, 0, 0))],
        out_specs=pl.BlockSpec((tq, A_Q), lambda b, i: (b * nq + i, 0)),
        out_shape=jax.ShapeDtypeStruct((M, A_Q), BF16),
        scratch_shapes=[pltpu.VMEM((A_HEADS * tq // GQA_TILE, 1, GQA_TILE), F32),
                        pltpu.VMEM((A_HEADS * tq // GQA_TILE, VT_ROWS // A_KV_HEADS, GQA_TILE), F32)],
        compiler_params=_cparams("parallel", "parallel"),
        name="gqa",
    )(q, k, vt)


def _rglru_kernel(xp_ref, xc_ref, xn_ref, yp_ref, yc_ref, yn_ref, cw_ref, cb_ref, wg_ref, bg_ref, lam_ref,
                  hf_ref, hr_ref, a_s, u_s, carry_s, *, tm, nblk):
    t = pl.program_id(1)

    @pl.when(t == 0)
    def _():
        carry_s[...] = jnp.zeros_like(carry_s)

    cw = cw_ref[...]
    ngrp = tm // 8
    row8 = lax.broadcasted_iota(jnp.int32, (ngrp, 8, B_WIDTH), 1)
    for d, (p_ref, c_ref, n_ref) in enumerate(((xp_ref, xc_ref, xn_ref), (yp_ref, yc_ref, yn_ref))):
        tt = t if d == 0 else nblk - 1 - t
        xc = _conv4(p_ref[...], c_ref[...], n_ref[...], cw, tt == 0, tt == nblk - 1) + cb_ref[...]
        xb = xc.astype(BF16)
        r = _sigmoid(jnp.dot(xb, wg_ref[2 * d], preferred_element_type=F32) + bg_ref[2 * d:2 * d + 1])
        i = _sigmoid(jnp.dot(xb, wg_ref[2 * d + 1], preferred_element_type=F32) + bg_ref[2 * d + 1:2 * d + 2])
        log_a = (-RG_C) * r * _softplus(-lam_ref[d:d + 1])
        a = jnp.exp(log_a).reshape(ngrp, 8, B_WIDTH)
        u = (jnp.sqrt(1.0 - jnp.exp(2.0 * log_a)) * (i * xc)).reshape(ngrp, 8, B_WIDTH)
        for sh in (1, 2, 4):
            keep = (row8 >= sh) if d == 0 else (row8 < 8 - sh)
            shift = sh if d == 0 else 8 - sh
            u = u + a * jnp.where(keep, pltpu.roll(u, shift, 1), 0.0)
            a = a * jnp.where(keep, pltpu.roll(a, shift, 1), 1.0)
        a_s[d] = a.reshape(tm, B_WIDTH)
        u_s[d] = u.reshape(tm, B_WIDTH)

    def body(gidx, carry):
        out = []
        for d, out_ref in enumerate((hf_ref, hr_ref)):
            gg = gidx if d == 0 else ngrp - 1 - gidx
            start = pl.multiple_of(gg * 8, 8)
            hg = u_s[d, pl.ds(start, 8), :] + a_s[d, pl.ds(start, 8), :] * carry[d]
            out_ref[pl.ds(start, 8), :] = hg
            edge = hg[7:8, :] if d == 0 else hg[0:1, :]
            out.append(jnp.broadcast_to(edge, (8, B_WIDTH)))
        return tuple(out)

    ends = lax.fori_loop(0, ngrp, body, (carry_s[0], carry_s[1]), unroll=4)
    carry_s[0] = ends[0]
    carry_s[1] = ends[1]


def _rglru(xr, cw, cb, wg, bg, lam, B, S, tm=256):
    M = xr.shape[0]
    nblk = S // tm
    r = tm // HALO
    nrb = M // HALO
    fwd = lambda b, t: b * nblk + t
    rev = lambda b, t: b * nblk + (nblk - 1 - t)
    specs = []
    for blk in (fwd, rev):
        specs += [pl.BlockSpec((HALO, B_WIDTH), lambda b, t, blk=blk: (jnp.maximum(blk(b, t) * r - 1, 0), 0)),
                  pl.BlockSpec((tm, B_WIDTH), lambda b, t, blk=blk: (blk(b, t), 0)),
                  pl.BlockSpec((HALO, B_WIDTH), lambda b, t, blk=blk: (jnp.minimum((blk(b, t) + 1) * r, nrb - 1), 0))]
    const2 = lambda b, t: (0, 0)
    const3 = lambda b, t: (0, 0, 0)
    kern = functools.partial(_rglru_kernel, tm=tm, nblk=nblk)
    return pl.pallas_call(
        kern,
        grid=(B, nblk),
        in_specs=specs + [pl.BlockSpec((4, B_WIDTH), const2), pl.BlockSpec((1, B_WIDTH), const2),
                          pl.BlockSpec((4, B_WIDTH, B_WIDTH), const3), pl.BlockSpec((4, B_WIDTH), const2),
                          pl.BlockSpec((2, B_WIDTH), const2)],
        out_specs=[pl.BlockSpec((tm, B_WIDTH), lambda b, t: (fwd(b, t), 0)),
                   pl.BlockSpec((tm, B_WIDTH), lambda b, t: (rev(b, t), 0))],
        out_shape=[jax.ShapeDtypeStruct((M, B_WIDTH), F32), jax.ShapeDtypeStruct((M, B_WIDTH), F32)],
        scratch_shapes=[pltpu.VMEM((2, tm, B_WIDTH), F32), pltpu.VMEM((2, tm, B_WIDTH), F32),
                        pltpu.VMEM((2, 8, B_WIDTH), F32)],
        compiler_params=_cparams("arbitrary", "arbitrary"),
        name="rglru",
    )(xr, xr, xr, xr, xr, xr, cw, cb, wg, bg, lam)


def _mlp_tail(x, ff, final):
    gff_ref, w1_ref, w2_ref, gfin_ref = ff
    xn = _rms(x, gff_ref[...]).astype(BF16)
    acc = x
    for c in range(D_FF // MLP_TF):
        h = jnp.dot(xn, w1_ref[:, c * MLP_TF:(c + 1) * MLP_TF], preferred_element_type=F32)
        h = jnp.square(jnp.maximum(h, 0.0)).astype(BF16)
        acc = acc + jnp.dot(h, w2_ref[c * MLP_TF:(c + 1) * MLP_TF, :], preferred_element_type=F32)
    return _rms(acc, gfin_ref[...]) if final else acc


def _resident(shape):
    return pl.BlockSpec(shape, lambda *_: (0,) * len(shape), pipeline_mode=pl.Buffered(1))


def _ff_specs():
    return [_resident((1, D_MODEL)), _resident((D_MODEL, D_FF)), _resident((D_FF, D_MODEL)),
            _resident((1, D_MODEL))]


def _even_out_kernel(x_ref, ya_ref, hf_ref, hr_ref, gr_ref, wa_ref, wb_ref, *rest, final):
    *ff, o_ref = rest
    yb = ((hf_ref[...] + hr_ref[...]) * _gelu_tanh(gr_ref[...])).astype(BF16)
    y = jnp.dot(ya_ref[...], wa_ref[...], preferred_element_type=F32)
    y = y + jnp.dot(yb, wb_ref[...], preferred_element_type=F32)
    o_ref[...] = _mlp_tail(x_ref[...] + y, ff, final)


def _even_out(x, ya, hf, hr, gr, wa, wb, ff, final, tm=512):
    M = x.shape[0]
    row = lambda i: (i, 0)
    return pl.pallas_call(
        functools.partial(_even_out_kernel, final=final),
        grid=(M // tm,),
        in_specs=[pl.BlockSpec((tm, D_MODEL), row), pl.BlockSpec((tm, A_Q), row),
                  pl.BlockSpec((tm, B_WIDTH), row), pl.BlockSpec((tm, B_WIDTH), row),
                  pl.BlockSpec((tm, B_WIDTH), row),
                  _resident((A_Q, D_MODEL)), _resident((B_WIDTH, D_MODEL))] + _ff_specs(),
        out_specs=pl.BlockSpec((tm, D_MODEL), row),
        out_shape=jax.ShapeDtypeStruct((M, D_MODEL), F32),
        compiler_params=_cparams("parallel"),
        name="even_out_mlp",
    )(x, ya, hf, hr, gr, wa, wb, *ff)


def _rope_tables(S):
    rows = S // GRID_W
    row = jnp.repeat(jnp.arange(rows, dtype=F32), GRID_W)
    col = jnp.tile(jnp.arange(GRID_W, dtype=F32), rows)
    n_freq = A_HEAD_DIM // 4
    inv = ROPE_THETA ** (-jnp.arange(n_freq, dtype=F32) / n_freq)
    ang_r = row[:, None] * inv
    ang_c = col[:, None] * inv
    cos = jnp.concatenate([jnp.cos(ang_r)] * 2 + [jnp.cos(ang_c)] * 2, axis=1)
    sin = jnp.concatenate([-jnp.sin(ang_r), jnp.sin(ang_r), -jnp.sin(ang_c), jnp.sin(ang_c)], axis=1)
    return jnp.tile(cos, (1, 2)), jnp.tile(sin, (1, 2))


def _prep_even(w_in, w_out, qn, kn, conv_w, conv_b, wr, br, wi, bi, lam):
    G = A_HEADS // A_KV_HEADS
    wq = w_in[:, :A_Q].reshape(D_MODEL, A_HEADS, A_HEAD_DIM)
    zeros = jnp.zeros_like(wq)
    half = (jnp.arange(A_HEADS) // G)[None, :, None]
    wq_pad = jnp.concatenate([jnp.where(half == 0, wq, zeros), jnp.where(half == 1, wq, zeros)], axis=-1)
    w = jnp.concatenate([wq_pad.reshape(D_MODEL, A_HEADS * LANES), w_in[:, A_Q:A_Q + A_KV],
                         w_in[:, A_Q + 2 * A_KV:]], axis=1).astype(BF16)
    wvt = w_in[:, A_Q + A_KV:A_Q + 2 * A_KV].T.astype(BF16)
    qg =jnp.tile(qn.astype(F32), 2 * A_HEADS)[None]
    kg = jnp.tile(kn.astype(F32), 2)[None]

    def dense(blocks):
        eye = jnp.eye(B_BLOCKS, dtype=blocks.dtype)
        return jnp.einsum('nde,nm->ndme', blocks, eye).reshape(B_WIDTH, B_WIDTH)

    wg = jnp.stack([dense(wr[0]), dense(wi[0]), dense(wr[1]), dense(wi[1])]).astype(BF16)
    bg = jnp.stack([br[0], bi[0], br[1], bi[1]]).astype(F32)
    return dict(w=w, wvt=wvt, qg=qg, kg=kg, cw=conv_w.astype(F32), cb=conv_b.astype(F32)[None], wg=wg, bg=bg,
                lam=lam.astype(F32), wa=w_out[:A_Q].astype(BF16), wb=w_out[A_Q:].astype(BF16))


def _even_layer(x, gain, p, ff, final, cos, sin, B, S):
    q, k, vt, xr, gr = _even_proj(x, gain, p['w'], p['wvt'], p['qg'], p['kg'], cos, sin, S)
    ya = _gqa(q, k, vt, B, S)
    hf, hr = _rglru(xr, p['cw'], p['cb'], p['wg'], p['bg'], p['lam'], B, S)
    return _even_out(x, ya, hf, hr, gr, p['wa'], p['wb'], ff, final)


def _odd_proj_kernel(xp_ref, x_ref, xn_ref, g_ref, w_ref, cw_ref, nega_ref, dtb_ref, perm_ref,
                     cq_ref, ck_ref, cv_ref, z_ref, *rest, n_s):
    *d_refs, gb_ref = rest
    tm = x_ref.shape[0]
    i = pl.program_id(0)
    gain = g_ref[...]
    xn = [_rms(x_ref[r0:r0 + PROJ_ROWS, :], gain).astype(BF16) for r0 in range(0, tm, PROJ_ROWS)]
    ext = jnp.concatenate([_rms(xp_ref[...], gain).astype(BF16)] + xn + [_rms(xn_ref[...], gain).astype(BF16)],
                          axis=0)
    n_c = 3 * C_WIDTH
    pq = jnp.dot(ext, w_ref[:, :n_c], preferred_element_type=F32)
    proj = jnp.concatenate([jnp.dot(part, w_ref[:, n_c:], preferred_element_type=F32) for part in xn], axis=0)
    y = _silu(_conv4(pq[:HALO], pq[HALO:HALO + tm], pq[HALO + tm:], cw_ref[...], i % n_s == 0, i % n_s == n_s - 1))
    for h in range(C_HEADS):
        for part, ref, scale in ((0, cq_ref, C_HEAD_DIM ** -0.5), (1, ck_ref, 1.0)):
            a = y[:, part * C_WIDTH + h * LANES:part * C_WIDTH + (h + 1) * LANES]
            a = a * lax.rsqrt(jnp.sum(a * a, axis=-1, keepdims=True) + EPS)
            ref[:, h * LANES:(h + 1) * LANES] = a * scale
    cv_ref[...] = y[:, 2 * C_WIDTH:]
    z_ref[...] = proj[:, :C_WIDTH]
    off = C_WIDTH
    for ti, scale in enumerate((D_HEAD_DIM ** -0.5, 1.0, 1.0)):
        for gi, (_, dil) in enumerate(D_GROUPS):
            xg = (proj[:, off + gi * D_GW:off + (gi + 1) * D_GW] * scale).astype(BF16)
            ref = d_refs[ti * len(D_GROUPS) + gi]
            if dil == 1:
                ref[...] = xg
            else:
                y = jnp.dot(perm_ref[gi - 1], xg, preferred_element_type=F32)
                ref[...] = y.reshape(dil, tm // dil, D_GW).astype(BF16)
        off += D_WIDTH
    t = proj[:, off:off + LANES]
    lane = lax.broadcasted_iota(jnp.int32, t.shape, 1)
    gate = jnp.where(lane < 2 * C_HEADS, _sigmoid(t), nega_ref[...] * _softplus(t + dtb_ref[...]))
    gb_ref[...] = jnp.transpose(gate)[:4 * C_HEADS]


def _class_major_spec(dil, S, tm):
    n_s = S // tm
    return pl.BlockSpec((None, dil, tm // dil, D_GW), lambda i: (i // n_s, 0, i % n_s, 0))


def _odd_proj(x, gain, w, cw, nega, dtb, perm, B, S):
    tm = PERM_TM
    M = x.shape[0]
    row = lambda i: (i, 0)
    const = lambda i: (0, 0)
    hb = tm // HALO
    last_hb = M // HALO - 1
    d_specs, d_shapes = [], []
    for _ in range(3):
        for _, dil in D_GROUPS:
            if dil == 1:
                d_specs.append(pl.BlockSpec((tm, D_GW), row))
                d_shapes.append(jax.ShapeDtypeStruct((M, D_GW), BF16))
            else:
                d_specs.append(_class_major_spec(dil, S, tm))
                d_shapes.append(jax.ShapeDtypeStruct((B, dil, S // dil, D_GW), BF16))
    wide = pl.BlockSpec((tm, C_WIDTH), row)
    outs = pl.pallas_call(
        functools.partial(_odd_proj_kernel, n_s=S // tm),
        grid=(M // tm,),
        in_specs=[pl.BlockSpec((HALO, D_MODEL), lambda i: (jnp.maximum(i * hb - 1, 0), 0)),
                  pl.BlockSpec((tm, D_MODEL), row),
                  pl.BlockSpec((HALO, D_MODEL), lambda i: (jnp.minimum((i + 1) * hb, last_hb), 0)),
                  pl.BlockSpec((1, D_MODEL), const), pl.BlockSpec((D_MODEL, w.shape[1]), const),
                  pl.BlockSpec((4, 3 * C_WIDTH), const),
                  pl.BlockSpec((1, LANES), const), pl.BlockSpec((1, LANES), const),
                  pl.BlockSpec(perm.shape, lambda i: (0, 0, 0))],
        out_specs=[wide, wide, wide, wide] + d_specs + [pl.BlockSpec((4 * C_HEADS, tm), lambda i: (0, i))],
        out_shape=[jax.ShapeDtypeStruct((M, C_WIDTH), F32)] * 4 + d_shapes
        + [jax.ShapeDtypeStruct((4 * C_HEADS, M), F32)],
        compiler_params=_cparams("parallel"),
        name="odd_proj",
    )(x, x, x, gain, w, cw, nega, dtb, perm)
    ng = len(D_GROUPS)
    return outs[:4], outs[4:4 + ng], outs[4 + ng:4 + 2 * ng], outs[4 + 2 * ng:4 + 3 * ng], outs[-1]


def _nt_dot(a, b):
    return lax.dot_general(a, b, (((1,), (1,)), ((), ())), preferred_element_type=F32)


def _gdn_kernel(qf_ref, kf_ref, vf_ref, gf_ref, qr_ref, kr_ref, vr_ref, gr_ref, of_ref, or_ref, state, *, tm):
    C = C_CHUNK
    SB = LANES
    nsb = tm // SB
    nc = tm // C
    t = pl.program_id(1)

    @pl.when(t == 0)
    def _():
        state[...] = jnp.zeros_like(state)

    ri = lax.broadcasted_iota(jnp.int32, (SB, SB), 0)
    ci = lax.broadcasted_iota(jnp.int32, (SB, SB), 1)
    same = (ri // C) == (ci // C)
    eye = jnp.where(ri == ci, 1.0, 0.0)
    incl, strict, levels = [], [], []
    for d in range(2):
        hi_i, lo_i = (ci, ri) if d else (ri, ci)
        incl.append(same & (hi_i >= lo_i))
        strict.append(same & (hi_i > lo_i))
        lv = []
        s = 1
        while s < C:
            lv.append((((hi_i // s) % 2) == 1) & ((lo_i // s) == (hi_i // s) - 1))
            s *= 2
        levels.append(lv)

    pos = lax.broadcasted_iota(jnp.int32, (4 * C_HEADS, tm), 1) % C
    rows, cols = [], []
    for d, g_ref in enumerate((gf_ref, gr_ref)):
        gt = g_ref[...]
        x = gt
        s = 1
        while s < C:
            if d == 0:
                x = x + jnp.where(pos >= s, pltpu.roll(x, s, 1), 0.0)
            else:
                x = x + jnp.where(pos < C - s, pltpu.roll(x, tm - s, 1), 0.0)
            s *= 2
        lo = d * C_HEADS
        rw = jnp.concatenate([gt[lo:lo + C_HEADS], x[2 * C_HEADS + lo:2 * C_HEADS + lo + C_HEADS]], axis=0)
        rows.append(rw)
        cols.append(jnp.transpose(jnp.concatenate([rw, jnp.zeros((LANES - 2 * C_HEADS, tm), F32)], axis=0)))

    refs = ((qf_ref, kf_ref, vf_ref, of_ref), (qr_ref, kr_ref, vr_ref, or_ref))
    chains = []
    for d in range(2):
        for h in range(C_HEADS):
            for sb in range(nsb):
                rs = slice(sb * SB, (sb + 1) * SB)
                sl = slice(h * LANES, (h + 1) * LANES)
                q = refs[d][0][rs, sl]
                k = refs[d][1][rs, sl]
                v = refs[d][2][rs, sl]
                beta = jnp.broadcast_to(cols[d][rs, h:h + 1], (SB, LANES))
                gcc = jnp.broadcast_to(cols[d][rs, C_HEADS + h:C_HEADS + h + 1], (SB, LANES))
                gcr = jnp.broadcast_to(rows[d][C_HEADS + h:C_HEADS + h + 1, rs], (SB, SB))
                dec = jnp.exp(jnp.where(incl[d], gcc - gcr, NEG_INF))
                eg = jnp.exp(gcc)
                kbf = k.astype(BF16)
                kb = k * beta
                lmat = jnp.where(strict[d], _nt_dot(kb.astype(BF16), kbf) * dec, 0.0)
                amat = jnp.where(incl[d], _nt_dot(q.astype(BF16), kbf) * dec, 0.0).astype(BF16)
                chains.append(dict(d=d, h=h, sb=sb, k=k, gcc=gcc, lmat=lmat, amat=amat,
                                   rhs=jnp.concatenate([v * beta, kb * eg], axis=1).astype(BF16),
                                   qe=(q * eg).astype(BF16)))

    for ch in chains:
        ch['x'] = eye - jnp.where(levels[ch['d']][0], ch['lmat'], 0.0)
    for li in range(1, len(levels[0])):
        for ch in chains:
            ch['xb'] = ch['x'].astype(BF16)
            cmat = jnp.where(levels[ch['d']][li], ch['lmat'], 0.0).astype(BF16)
            ch['cx'] = jnp.dot(cmat, ch['xb'], preferred_element_type=F32).astype(BF16)
        for ch in chains:
            ch['x'] = ch['x'] - jnp.dot(ch['xb'], ch['cx'], preferred_element_type=F32)
    for ch in chains:
        sol = jnp.dot(ch['x'].astype(BF16), ch['rhs'], preferred_element_type=F32)
        ch['u'] = sol[:, :LANES]
        ch['w'] = sol[:, LANES:].astype(BF16)
        ch['vn'] = [None] * (SB // C)
        ch['ost'] = [None] * (SB // C)

    by_key = {(ch['d'], ch['h'], ch['sb']): ch for ch in chains}
    st = {(d, h): state[d, h] for d in range(2) for h in range(C_HEADS)}
    for step in range(nc):
        work = []
        for (d, h), s_val in st.items():
            c = nc - 1 - step if d else step
            ch = by_key[(d, h, c // (SB // C))]
            lc = c % (SB // C)
            rs = slice(lc * C, (lc + 1) * C)
            stb = s_val.astype(BF16)
            vn = ch['u'][rs] - jnp.dot(ch['w'][rs], stb, preferred_element_type=F32)
            ch['ost'][lc] = jnp.dot(ch['qe'][rs], stb, preferred_element_type=F32)
            ch['vn'][lc] = vn
            last = lc * C if d else (lc + 1) * C - 1
            gl = ch['gcc'][last:last + 1, :]
            kd = (ch['k'][rs] * jnp.exp(gl - ch['gcc'][rs])).astype(BF16)
            work.append(((d, h), s_val * jnp.exp(gl), kd, vn.astype(BF16)))
        for key, decayed, kd, vnb in work:
            st[key] = decayed + lax.dot_general(kd, vnb, (((0,), (0,)), ((), ())), preferred_element_type=F32)
    for (d, h), s_val in st.items():
        state[d, h] = s_val
    for ch in chains:
        rs = slice(ch['sb'] * SB, (ch['sb'] + 1) * SB)
        sl = slice(ch['h'] * LANES, (ch['h'] + 1) * LANES)
        vn_all = jnp.concatenate(ch['vn'], axis=0).astype(BF16)
        refs[ch['d']][3][rs, sl] = (jnp.concatenate(ch['ost'], axis=0)
                                   + jnp.dot(ch['amat'], vn_all, preferred_element_type=F32))


def _gdn(q, k, v, gbt, B, S, tm=256):
    M = q.shape[0]
    nblk = S // tm
    fwd = lambda b, t: b * nblk + t
    rev = lambda b, t: b * nblk + (nblk - 1 - t)
    specs = []
    for blk in (fwd, rev):
        specs += [pl.BlockSpec((tm, C_WIDTH), lambda b, t, blk=blk: (blk(b, t), 0))] * 3
        specs += [pl.BlockSpec((4 * C_HEADS, tm), lambda b, t, blk=blk: (0, blk(b, t)))]
    return pl.pallas_call(
        functools.partial(_gdn_kernel, tm=tm),
        grid=(B, nblk),
        in_specs=specs,
        out_specs=[pl.BlockSpec((tm, C_WIDTH), lambda b, t: (fwd(b, t), 0)),
                   pl.BlockSpec((tm, C_WIDTH), lambda b, t: (rev(b, t), 0))],
        out_shape=[jax.ShapeDtypeStruct((M, C_WIDTH), F32)] * 2,
        scratch_shapes=[pltpu.VMEM((2, C_HEADS, C_HEAD_DIM, C_HEAD_DIM), F32)],
        compiler_params=_cparams("arbitrary", "arbitrary"),
        name="gdn",
    )(q, k, v, gbt, q, k, v, gbt)


def _band_kernel(q_ref, kp_ref, kc_ref, kn_ref, vp_ref, vc_ref, vn_ref, bias_ref, o_ref, lse_ref, *, tq, m_len):
    t = pl.program_id(1)
    QT = BAND_QT
    nkeys = QT + 2 * D_STEPS
    kwin = jnp.concatenate([kp_ref[...], kc_ref[...], kn_ref[...]], axis=0)
    vwin = jnp.concatenate([vp_ref[...], vc_ref[...], vn_ref[...]], axis=0)
    lane = lax.broadcasted_iota(jnp.int32, (QT, D_GW), 1)
    mine = [(lane // D_HEAD_DIM) == h for h in range(D_HEADS_PER_GROUP)]
    col = lax.broadcasted_iota(jnp.int32, (QT, nkeys), 1)
    tiles = []
    for i in range(tq // QT):
        q = q_ref[i * QT:(i + 1) * QT, :]
        kpos = t * tq + i * QT - D_STEPS + col
        valid = (kpos >= 0) & (kpos < m_len)
        kw = kwin[i * QT:i * QT + nkeys]
        for h in range(D_HEADS_PER_GROUP):
            qh = jnp.where(mine[h], q, jnp.zeros_like(q))
            s = lax.dot_general(qh, kw, (((1,), (1,)), ((), ())), preferred_element_type=F32) + bias_ref[h]
            tiles.append([i, h, jnp.where(valid, s, NEG_INF)])
    for tile in tiles:
        s = tile[2]
        m = jnp.max(s, axis=-1, keepdims=True)
        e = jnp.exp(s - m)
        l = jnp.sum(e, axis=-1, keepdims=True)
        tile[2] = (e * (1.0 / l)).astype(BF16)
        tile.append(m + jnp.log(l))
    for i in range(tq // QT):
        o = jnp.zeros((QT, D_GW), F32)
        lse_full = jnp.zeros((QT, D_GW), F32)
        vw = vwin[i * QT:i * QT + nkeys]
        for _, h, p, lse in tiles[i * D_HEADS_PER_GROUP:(i + 1) * D_HEADS_PER_GROUP]:
            o = jnp.where(mine[h], jnp.dot(p, vw, preferred_element_type=F32), o)
            lse_full = jnp.where(mine[h], lse, lse_full)
        o_ref[i * QT:(i + 1) * QT, :] = o
        lse_ref[i * QT:(i + 1) * QT, :] = lse_full


def _band(q, k, v, bias, Z, m_len):
    M = q.shape[0]
    tq = min(BAND_TQ, m_len)
    nq = m_len // tq
    r = tq // D_STEPS
    nhb = M // D_STEPS
    cur = lambda z, t: (z * nq + t, 0)
    prev = lambda z, t: (jnp.maximum((z * nq + t) * r - 1, 0), 0)
    nxt = lambda z, t: (jnp.minimum((z * nq + t + 1) * r, nhb - 1), 0)
    halo = lambda f: pl.BlockSpec((D_STEPS, D_GW), f)
    full = pl.BlockSpec((tq, D_GW), cur)
    return pl.pallas_call(
        functools.partial(_band_kernel, tq=tq, m_len=m_len),
        grid=(Z, nq),
        in_specs=[full, halo(prev), full, halo(nxt), halo(prev), full, halo(nxt),
                  pl.BlockSpec((D_HEADS_PER_GROUP, BAND_QT, BAND_QT + 2 * D_STEPS), lambda z, t: (0, 0, 0))],
        out_specs=[full, full],
        out_shape=[jax.ShapeDtypeStruct((M, D_GW), F32)] * 2,
        compiler_params=_cparams("parallel", "parallel"),
        name="band_attn",
    )(q, k, k, k, v, v, v, bias)


def _odd_out_kernel(x_ref, of_ref, or_ref, z_ref, cg_ref, o0_ref, o1_ref, o2_ref, l0_ref, l1_ref, l2_ref,
                    unperm_ref, wc_ref, wd_ref, *rest, final):
    *ff, out_ref = rest
    tm = x_ref.shape[0]
    oc = of_ref[...] + or_ref[...]
    z = z_ref[...]
    parts = []
    for h in range(C_HEADS):
        sl = slice(h * LANES, (h + 1) * LANES)
        parts.append(_rms(oc[:, sl], cg_ref[...]) * _silu(z[:, sl]))
    yc = jnp.concatenate(parts, axis=1).astype(BF16)

    def token_order(ref, gi):
        if D_GROUPS[gi][1] == 1:
            return ref[...]
        val = ref[...].reshape(tm, D_GW)
        hi = val.astype(BF16)
        lo = (val - hi.astype(F32)).astype(BF16)
        pt = unperm_ref[gi - 1]
        return jnp.dot(pt, hi, preferred_element_type=F32) + jnp.dot(pt, lo, preferred_element_type=F32)

    o0, o1, o2 = (token_order(r, gi) for gi, r in enumerate((o0_ref, o1_ref, o2_ref)))
    l0, l1, l2 = (token_order(r, gi) for gi, r in enumerate((l0_ref, l1_ref, l2_ref)))
    m = jnp.maximum(jnp.maximum(l0, l1), l2)
    e0, e1, e2 = jnp.exp(l0 - m), jnp.exp(l1 - m), jnp.exp(l2 - m)
    inv = 1.0 / (e0 + e1 + e2)
    yd = ((e0 * o0 + e1 * o1 + e2 * o2) * inv).astype(BF16)
    y = jnp.dot(yc, wc_ref[...], preferred_element_type=F32) + jnp.dot(yd, wd_ref[...], preferred_element_type=F32)
    out_ref[...] = _mlp_tail(x_ref[...] + y, ff, final)


def _odd_out(x, o_f, o_r, z, cg, os_, ls_, unperm, wc, wd, ff, final, B, S):
    tm = PERM_TM
    M = x.shape[0]
    row = lambda i: (i, 0)
    wide = pl.BlockSpec((tm, C_WIDTH), row)
    grp = [pl.BlockSpec((tm, D_GW), row) if dil == 1 else _class_major_spec(dil, S, tm) for _, dil in D_GROUPS]
    return pl.pallas_call(
        functools.partial(_odd_out_kernel, final=final),
        grid=(M // tm,),
        in_specs=[pl.BlockSpec((tm, D_MODEL), row), wide, wide, wide, _resident((1, LANES))] + grp + grp
        + [_resident(unperm.shape), _resident((C_WIDTH, D_MODEL)), _resident((D_GW, D_MODEL))] + _ff_specs(),
        out_specs=pl.BlockSpec((tm, D_MODEL), row),
        out_shape=jax.ShapeDtypeStruct((M, D_MODEL), F32),
        compiler_params=_cparams("parallel"),
        name="odd_out_mlp",
    )(x, o_f, o_r, z, cg, *os_, *ls_, unperm, wc, wd, *ff)


def _class_perms():
    mats = []
    for _, dil in D_GROUPS[1:]:
        out_row = np.arange(PERM_TM)
        src = (out_row % (PERM_TM // dil)) * dil + out_row // (PERM_TM // dil)
        mats.append(np.eye(PERM_TM, dtype=np.float32)[src])
    perm = np.stack(mats)
    return jnp.asarray(perm, BF16), jnp.asarray(perm.transpose(0, 2, 1), BF16)


def _t5_bucket(rel):
    nb = N_BUCKETS // 2
    max_exact = nb // 2
    n = np.abs(rel)
    large = max_exact + (np.log(np.maximum(n, 1) / max_exact) / math.log(MAX_DISTANCE / max_exact)
                         * (nb - max_exact)).astype(np.int64)
    large = np.minimum(large, nb - 1)
    return (np.where(rel > 0, nb, 0) + np.where(n < max_exact, n, large)).astype(np.int32)


def _band_bias(rel_bias):
    nkeys = BAND_QT + 2 * D_STEPS
    tiles = []
    for gi, (window, dil) in enumerate(D_GROUPS):
        steps = window // (2 * dil)
        assert steps == D_STEPS
        buckets = _t5_bucket(np.arange(-steps, steps + 1) * dil)
        hsl = slice(gi * D_HEADS_PER_GROUP, (gi + 1) * D_HEADS_PER_GROUP)
        b = rel_bias.astype(F32)[jnp.asarray(buckets)][:, hsl].T
        period = jnp.concatenate([b, jnp.full((D_HEADS_PER_GROUP, nkeys + 1 - b.shape[1]), NEG_INF, F32)], axis=1)
        flat = jnp.tile(period, (1, BAND_QT))[:, :BAND_QT * nkeys]
        tiles.append(flat.reshape(D_HEADS_PER_GROUP, BAND_QT, nkeys))
    return jnp.stack(tiles)


def _prep_odd(w_in, w_out, conv_w, a_log, dt_bias, o_gain, rel_bias):
    n_c = 4 * C_WIDTH
    n_ba = 4 * C_HEADS
    w = jnp.concatenate([w_in[:, :n_c], w_in[:, n_c + n_ba:], w_in[:, n_c:n_c + n_ba],
                         jnp.zeros((D_MODEL, LANES - n_ba), w_in.dtype)], axis=1).astype(BF16)
    pad = jnp.zeros((LANES - n_ba,), F32)
    nega = jnp.concatenate([jnp.zeros((2 * C_HEADS,), F32), -jnp.exp(a_log.astype(F32)).reshape(-1), pad])[None]
    dtb = jnp.concatenate([jnp.zeros((2 * C_HEADS,), F32), dt_bias.astype(F32).reshape(-1), pad])[None]
    return dict(w=w, nega=nega, dtb=dtb, cw=conv_w.astype(F32), cg=o_gain.astype(F32)[None],
                bias=_band_bias(rel_bias), wc=w_out[:C_WIDTH].astype(BF16), wd=w_out[C_WIDTH:].astype(BF16))


def _odd_layer(x, gain, p, ff, final, B, S):
    perm, unperm = _class_perms()
    (cq, ck, cv, z), dq, dk, dv, gb = _odd_proj(x, gain, p['w'], p['cw'], p['nega'], p['dtb'], perm, B, S)
    o_f, o_r = _gdn(cq, ck, cv, gb, B, S)
    outs, lses = [], []
    for gi, (_, dil) in enumerate(D_GROUPS):
        flat = lambda a: a.reshape(B * S, D_GW)
        o, lse = _band(flat(dq[gi]), flat(dk[gi]), flat(dv[gi]), p['bias'][gi], B * dil, S // dil)
        shape = (B * S, D_GW) if dil == 1 else (B, dil, S // dil, D_GW)
        outs.append(o.reshape(shape))
        lses.append(lse.reshape(shape))
    return _odd_out(x, o_f, o_r, z, p['cg'], outs, lses, unperm, p['wc'], p['wd'], ff, final, B, S)


def _trunk(x3, evens, odds, norm_mix, norm_ff, norm_final, w1, w2):
    B, S, _ = x3.shape
    x = x3.reshape(B * S, D_MODEL)
    cos, sin = _rope_tables(S)
    for layer in range(DEPTH):
        gain = norm_mix[layer][None]
        ff = (norm_ff[layer][None], w1[layer], w2[layer], norm_final[None])
        final = layer == DEPTH - 1
        if layer % 2 == 0:
            x = _even_layer(x, gain, evens[layer // 2], ff, final, cos, sin, B, S)
        else:
            x = _odd_layer(x, gain, odds[layer // 2], ff, final, B, S)
    return x.reshape(B, S, D_MODEL)


def kernel(x_prompt, x_sample, rel_bias, norm_mix, norm_ff, norm_final, w_ff1, w_ff2, w_in_e, w_out_e,
           a_qnorm, a_knorm, b_conv_w, b_conv_b, b_wr, b_br, b_wi, b_bi, b_lambda, w_in_o, w_out_o,
           c_conv_w, c_a_log, c_dt_bias, c_norm):
    evens = [_prep_even(w_in_e[j], w_out_e[j], a_qnorm[j], a_knorm[j], b_conv_w[j], b_conv_b[j],
                        b_wr[j], b_br[j], b_wi[j], b_bi[j], b_lambda[j]) for j in range((DEPTH + 1) // 2)]
    odds = [_prep_odd(w_in_o[j], w_out_o[j], c_conv_w[j], c_a_log[j], c_dt_bias[j], c_norm[j], rel_bias)
            for j in range(DEPTH // 2)]
    w1 = w_ff1.astype(BF16)
    w2 = w_ff2.astype(BF16)
    nm = norm_mix.astype(F32)
    nf = norm_ff.astype(F32)
    ng = norm_final.astype(F32)
    y_prompt = _trunk(x_prompt, evens, odds, nm, nf, ng, w1, w2)
    y_sample = _trunk(x_sample, evens, odds, nm, nf, ng, w1, w2)
    return (y_prompt, y_sample)
```

```python
import functools
import math

import numpy as np
import jax
import jax.numpy as jnp
from jax import lax
from jax.experimental import pallas as pl
from jax.experimental.pallas import tpu as pltpu

F32 = jnp.float32
BF16 = jnp.bfloat16

D_MODEL = 1024
D_FF = 4 * D_MODEL
DEPTH = 4
EPS = 1e-6
NEG_INF = -1e30
GRID_W = 64
LANES = 128

A_HEADS = 8
A_KV_HEADS = 2
A_HEAD_DIM = 64
A_Q = A_HEADS * A_HEAD_DIM
A_KV = A_KV_HEADS * A_HEAD_DIM
ROPE_THETA = 10000.0
B_WIDTH = 512
B_BLOCKS = 8
B_BLOCK_DIM = B_WIDTH // B_BLOCKS
RG_C = 8.0
C_HEADS = 4
C_HEAD_DIM = 128
C_WIDTH = C_HEADS * C_HEAD_DIM
C_CHUNK = 64
D_GROUPS = ((128, 1), (512, 4), (2048, 16))
D_HEADS_PER_GROUP = 4
D_HEAD_DIM = 64
D_NHEADS = len(D_GROUPS) * D_HEADS_PER_GROUP
D_WIDTH = D_NHEADS * D_HEAD_DIM
D_GW = D_HEADS_PER_GROUP * D_HEAD_DIM
D_STEPS = 64
N_BUCKETS = 32
MAX_DISTANCE = 1024
HALO = 8
Q_SCALE = A_HEAD_DIM ** -0.5 * math.log2(math.e)
MLP_TF = 1024
PROJ_ROWS = 256
PERM_TM = 512
BAND_QT = 128
BAND_TQ = 1024
GQA_TILE = 256
VT_PAD = 16
VT_ROWS = A_KV_HEADS * (A_HEAD_DIM + VT_PAD)

VMEM_LIMIT = 56 * 1024 * 1024


def _cparams(*sem):
    return pltpu.CompilerParams(dimension_semantics=sem, vmem_limit_bytes=VMEM_LIMIT)


def _rms(x, gain):
    return x * lax.rsqrt(jnp.mean(x * x, axis=-1, keepdims=True) + EPS) * gain


def _softplus(x):
    return jnp.maximum(x, 0.0) + jnp.log1p(jnp.exp(-jnp.abs(x)))


def _sigmoid(x):
    return 0.5 * jnp.tanh(0.5 * x) + 0.5


def _silu(x):
    return x * _sigmoid(x)


def _gelu_tanh(x):
    return 0.5 * x * (1.0 + jnp.tanh(math.sqrt(2.0 / math.pi) * (x + 0.044715 * (x * x * x))))


def _conv4(prev, cur, nxt, w, first, last):
    rows = cur.shape[0]
    prev = jnp.where(first, 0.0, prev)
    nxt = jnp.where(last, 0.0, nxt)
    full = jnp.concatenate([prev, cur, nxt], axis=0)
    y = full[HALO - 2:HALO - 2 + rows] * w[0:1]
    for j in range(1, 4):
        y = y + full[HALO - 2 + j:HALO - 2 + j + rows] * w[j:j + 1]
    return y


def _even_proj_kernel(x_ref, g_ref, w_ref, wvt_ref, qg_ref, kg_ref, cos_ref, sin_ref,
                      q_ref, k_ref, vt_ref, xr_ref, gr_ref):
    tm = x_ref.shape[0]
    parts = []
    for r0 in range(0, tm, PROJ_ROWS):
        xn = _rms(x_ref[r0:r0 + PROJ_ROWS, :], g_ref[...]).astype(BF16)
        proj = jnp.dot(xn, w_ref[...], preferred_element_type=F32)
        vt = lax.dot_general(wvt_ref[...], xn, (((1,), (1,)), ((), ())), preferred_element_type=F32)
        parts.append((r0, proj, vt))
    lane = lax.broadcasted_iota(jnp.int32, (PROJ_ROWS, LANES), 1)
    first_half = (lane % 32) < 16
    lo = lane < A_HEAD_DIM
    ones = jnp.ones((VT_PAD, LANES), BF16)
    for r0, proj, vt in parts:
        rows = slice(r0, r0 + PROJ_ROWS)
        cos = cos_ref[rows, :]
        sin = sin_ref[rows, :]

        def rope(t):
            swapped = jnp.where(first_half, pltpu.roll(t, LANES - 16, 1), pltpu.roll(t, 16, 1))
            return t * cos + swapped * sin

        vt = vt.astype(BF16)
        for c in range(PROJ_ROWS // LANES):
            cols = slice(c * LANES, (c + 1) * LANES)
            vt_ref[r0 // LANES + c] = jnp.concatenate(
                [vt[:A_HEAD_DIM, cols], ones, vt[A_HEAD_DIM:, cols], ones], axis=0)
        for h in range(A_HEADS):
            t = proj[:, h * LANES:(h + 1) * LANES]
            r = lax.rsqrt(jnp.sum(t * t, axis=-1, keepdims=True) * (1.0 / A_HEAD_DIM) + EPS)
            t = rope(t * r * qg_ref[:, h * LANES:(h + 1) * LANES])
            q_ref[rows, h * LANES:(h + 1) * LANES] = (t * Q_SCALE).astype(BF16)
        off = A_HEADS * LANES
        t = proj[:, off:off + LANES]
        t2 = t * t
        s_lo = jnp.sum(jnp.where(lo, t2, 0.0), axis=-1, keepdims=True)
        s_hi = jnp.sum(jnp.where(lo, 0.0, t2), axis=-1, keepdims=True)
        r = lax.rsqrt(jnp.where(lo, s_lo, s_hi) * (1.0 / A_HEAD_DIM) + EPS)
        k_ref[rows, :] = rope(t * r * kg_ref[...]).astype(BF16)
        off += LANES
        xr_ref[rows, :] = proj[:, off:off + B_WIDTH]
        off += B_WIDTH
        gr_ref[rows, :] = proj[:, off:off + B_WIDTH]


def _even_proj(x, gain, w, wvt, qg, kg, cos, sin, S, tm=512):
    M = x.shape[0]
    nS = S // tm
    n_in = w.shape[1]
    row = lambda i: (i, 0)
    const = lambda i: (0, 0)
    return pl.pallas_call(
        _even_proj_kernel,
        grid=(M // tm,),
        in_specs=[pl.BlockSpec((tm, D_MODEL), row), pl.BlockSpec((1, D_MODEL), const),
                  pl.BlockSpec((D_MODEL, n_in), const), pl.BlockSpec((A_KV, D_MODEL), const),
                  pl.BlockSpec((1, A_HEADS * LANES), const), pl.BlockSpec((1, LANES), const),
                  pl.BlockSpec((tm, LANES), lambda i: (i % nS, 0)),
                  pl.BlockSpec((tm, LANES), lambda i: (i % nS, 0))],
        out_specs=[pl.BlockSpec((tm, A_HEADS * LANES), row), pl.BlockSpec((tm, LANES), row),
                   pl.BlockSpec((tm // LANES, VT_ROWS, LANES), lambda i: (i, 0, 0)),
                   pl.BlockSpec((tm, B_WIDTH), row), pl.BlockSpec((tm, B_WIDTH), row)],
        out_shape=[jax.ShapeDtypeStruct((M, A_HEADS * LANES), BF16),
                   jax.ShapeDtypeStruct((M, LANES), BF16),
                   jax.ShapeDtypeStruct((M // LANES, VT_ROWS, LANES), BF16),
                   jax.ShapeDtypeStruct((M, B_WIDTH), F32), jax.ShapeDtypeStruct((M, B_WIDTH), F32)],
        compiler_params=_cparams("parallel"),
        name="even_proj",
    )(x, gain, w, wvt, qg, kg, cos, sin)


def _gqa_kernel(q_ref, k_ref, vt_ref, o_ref, m_ref, acc_ref, *, tq, kstep, nstep, unroll, lookahead):
    hpt = GQA_TILE // tq
    ntile = A_HEADS // hpt
    qt = [jnp.concatenate([q_ref[:, h * LANES:(h + 1) * LANES] for h in range(p * hpt, (p + 1) * hpt)], axis=0)
          for p in range(ntile)]
    m_ref[...] = jnp.full(m_ref.shape, NEG_INF, F32)
    acc_ref[...] = jnp.zeros(acc_ref.shape, F32)
    vrows = VT_ROWS // A_KV_HEADS

    per = kstep // LANES

    def scores(j, u, p):
        ks = k_ref[pl.ds(pl.multiple_of((j * unroll + u) * kstep, kstep), kstep), :]
        return lax.dot_general(ks, qt[p], (((1,), (1,)), ((), ())), preferred_element_type=F32)

    def update(j, u, p, st):
        g = (p * hpt) // (A_HEADS // A_KV_HEADS)
        m_old = m_ref[p]
        m_new = jnp.maximum(m_old, jnp.max(st, axis=0, keepdims=True))
        alpha = jnp.exp2(m_old - m_new)
        pt = jnp.exp2(st - m_new).astype(BF16)
        first = (j * unroll + u) * per
        vts = jnp.concatenate([vt_ref[first + i, g * vrows:(g + 1) * vrows, :] for i in range(per)], axis=1)
        acc_ref[p] = alpha * acc_ref[p] + jnp.dot(vts, pt, preferred_element_type=F32)
        m_ref[p] = m_new

    pairs = [(u, p) for u in range(unroll) for p in range(ntile)]
    niter = nstep // unroll

    def body(j, pending):
        pending = list(pending)
        j_next = jnp.minimum(j + 1, niter - 1)
        for idx in range(len(pairs)):
            ahead = idx + lookahead
            if ahead < len(pairs):
                pending.append(scores(j, *pairs[ahead]))
            else:
                pending.append(scores(j_next, *pairs[ahead - len(pairs)]))
            update(j, *pairs[idx], pending.pop(0))
        return tuple(pending)

    lax.fori_loop(0, niter, body, tuple(scores(0, *pairs[i]) for i in range(lookahead)))
    heads = []
    for p in range(ntile):
        acc = acc_ref[p]
        o = acc[:A_HEAD_DIM] / acc[A_HEAD_DIM:A_HEAD_DIM + 1]
        heads += [o[:, i * tq:(i + 1) * tq] for i in range(hpt)]
    for e in range(A_HEADS // 2):
        both = jnp.concatenate(heads[2 * e:2 * e + 2], axis=0)
        o_ref[:, e * LANES:(e + 1) * LANES] = jnp.transpose(both).astype(o_ref.dtype)


def _gqa(q, k, vt, B, S, kstep=256, lookahead=5):
    tq, unroll = (GQA_TILE, 8) if S <= 2048 else (GQA_TILE // 2, 32)
    M = q.shape[0]
    nq = S // tq
    nslab = S // LANES
    nstep = S // kstep
    unroll = min(unroll, nstep)
    kern = functools.partial(_gqa_kernel, tq=tq, kstep=kstep, nstep=nstep, unroll=unroll, lookahead=lookahead)
    return pl.pallas_call(
        kern,
        grid=(B, nq),
        in_specs=[pl.BlockSpec((tq, A_HEADS * LANES), lambda b, i: (b * nq + i, 0)),
                  pl.BlockSpec((S, LANES), lambda b, i: (b, 0)),
                  pl.BlockSpec((nslab, VT_ROWS, LANES), lambda b, i: (b, 0, 0))],
        out_specs=pl.BlockSpec((tq, A_Q), lambda b, i: (b * nq + i, 0)),
        out_shape=jax.ShapeDtypeStruct((M, A_Q), BF16),
        scratch_shapes=[pltpu.VMEM((A_HEADS * tq // GQA_TILE, 1, GQA_TILE), F32),
                        pltpu.VMEM((A_HEADS * tq // GQA_TILE, VT_ROWS // A_KV_HEADS, GQA_TILE), F32)],
        compiler_params=_cparams("parallel", "parallel"),
        name="gqa",
    )(q, k, vt)


def _rglru_kernel(xp_ref, xc_ref, xn_ref, yp_ref, yc_ref, yn_ref, cw_ref, cb_ref, wg_ref, bg_ref, lam_ref,
                  hf_ref, hr_ref, a_s, u_s, carry_s, *, tm, nblk):
    t = pl.program_id(1)

    @pl.when(t == 0)
    def _():
        carry_s[...] = jnp.zeros_like(carry_s)

    cw = cw_ref[...]
    ngrp = tm // 8
    row8 = lax.broadcasted_iota(jnp.int32, (ngrp, 8, B_WIDTH), 1)
    for d, (p_ref, c_ref, n_ref) in enumerate(((xp_ref, xc_ref, xn_ref), (yp_ref, yc_ref, yn_ref))):
        tt = t if d == 0 else nblk - 1 - t
        xc = _conv4(p_ref[...], c_ref[...], n_ref[...], cw, tt == 0, tt == nblk - 1) + cb_ref[...]
        xb = xc.astype(BF16)
        r = _sigmoid(jnp.dot(xb, wg_ref[2 * d], preferred_element_type=F32) + bg_ref[2 * d:2 * d + 1])
        i = _sigmoid(jnp.dot(xb, wg_ref[2 * d + 1], preferred_element_type=F32) + bg_ref[2 * d + 1:2 * d + 2])
        log_a = (-RG_C) * r * _softplus(-lam_ref[d:d + 1])
        a = jnp.exp(log_a).reshape(ngrp, 8, B_WIDTH)
        u = (jnp.sqrt(1.0 - jnp.exp(2.0 * log_a)) * (i * xc)).reshape(ngrp, 8, B_WIDTH)
        for sh in (1, 2, 4):
            keep = (row8 >= sh) if d == 0 else (row8 < 8 - sh)
            shift = sh if d == 0 else 8 - sh
            u = u + a * jnp.where(keep, pltpu.roll(u, shift, 1), 0.0)
            a = a * jnp.where(keep, pltpu.roll(a, shift, 1), 1.0)
        a_s[d] = a.reshape(tm, B_WIDTH)
        u_s[d] = u.reshape(tm, B_WIDTH)

    def body(gidx, carry):
        out = []
        for d, out_ref in enumerate((hf_ref, hr_ref)):
            gg = gidx if d == 0 else ngrp - 1 - gidx
            start = pl.multiple_of(gg * 8, 8)
            hg = u_s[d, pl.ds(start, 8), :] + a_s[d, pl.ds(start, 8), :] * carry[d]
            out_ref[pl.ds(start, 8), :] = hg
            edge = hg[7:8, :] if d == 0 else hg[0:1, :]
            out.append(jnp.broadcast_to(edge, (8, B_WIDTH)))
        return tuple(out)

    ends = lax.fori_loop(0, ngrp, body, (carry_s[0], carry_s[1]), unroll=4)
    carry_s[0] = ends[0]
    carry_s[1] = ends[1]


def _rglru(xr, cw, cb, wg, bg, lam, B, S, tm=512):
    M = xr.shape[0]
    nblk = S // tm
    r = tm // HALO
    nrb = M // HALO
    fwd = lambda b, t: b * nblk + t
    rev = lambda b, t: b * nblk + (nblk - 1 - t)
    specs = []
    for blk in (fwd, rev):
        specs += [pl.BlockSpec((HALO, B_WIDTH), lambda b, t, blk=blk: (jnp.maximum(blk(b, t) * r - 1, 0), 0)),
                  pl.BlockSpec((tm, B_WIDTH), lambda b, t, blk=blk: (blk(b, t), 0)),
                  pl.BlockSpec((HALO, B_WIDTH), lambda b, t, blk=blk: (jnp.minimum((blk(b, t) + 1) * r, nrb - 1), 0))]
    const2 = lambda b, t: (0, 0)
    const3 = lambda b, t: (0, 0, 0)
    kern = functools.partial(_rglru_kernel, tm=tm, nblk=nblk)
    return pl.pallas_call(
        kern,
        grid=(B, nblk),
        in_specs=specs + [pl.BlockSpec((4, B_WIDTH), const2), pl.BlockSpec((1, B_WIDTH), const2),
                          pl.BlockSpec((4, B_WIDTH, B_WIDTH), const3), pl.BlockSpec((4, B_WIDTH), const2),
                          pl.BlockSpec((2, B_WIDTH), const2)],
        out_specs=[pl.BlockSpec((tm, B_WIDTH), lambda b, t: (fwd(b, t), 0)),
                   pl.BlockSpec((tm, B_WIDTH), lambda b, t: (rev(b, t), 0))],
        out_shape=[jax.ShapeDtypeStruct((M, B_WIDTH), F32), jax.ShapeDtypeStruct((M, B_WIDTH), F32)],
        scratch_shapes=[pltpu.VMEM((2, tm, B_WIDTH), F32), pltpu.VMEM((2, tm, B_WIDTH), F32),
                        pltpu.VMEM((2, 8, B_WIDTH), F32)],
        compiler_params=_cparams("arbitrary", "arbitrary"),
        name="rglru",
    )(xr, xr, xr, xr, xr, xr, cw, cb, wg, bg, lam)


def _mlp_tail(x, ff, final):
    gff_ref, w1_ref, w2_ref, gfin_ref = ff
    xn = _rms(x, gff_ref[...]).astype(BF16)
    acc = x
    for c in range(D_FF // MLP_TF):
        h = jnp.dot(xn, w1_ref[:, c * MLP_TF:(c + 1) * MLP_TF], preferred_element_type=F32)
        h = jnp.square(jnp.maximum(h, 0.0)).astype(BF16)
        acc = acc + jnp.dot(h, w2_ref[c * MLP_TF:(c + 1) * MLP_TF, :], preferred_element_type=F32)
    return _rms(acc, gfin_ref[...]) if final else acc


def _resident(shape):
    return pl.BlockSpec(shape, lambda *_: (0,) * len(shape), pipeline_mode=pl.Buffered(1))


def _ff_specs():
    return [_resident((1, D_MODEL)), _resident((D_MODEL, D_FF)), _resident((D_FF, D_MODEL)),
            _resident((1, D_MODEL))]


def _even_out_kernel(x_ref, ya_ref, hf_ref, hr_ref, gr_ref, wa_ref, wb_ref, *rest, final):
    *ff, o_ref = rest
    yb = ((hf_ref[...] + hr_ref[...]) * _gelu_tanh(gr_ref[...])).astype(BF16)
    y = jnp.dot(ya_ref[...], wa_ref[...], preferred_element_type=F32)
    y = y + jnp.dot(yb, wb_ref[...], preferred_element_type=F32)
    o_ref[...] = _mlp_tail(x_ref[...] + y, ff, final)


def _even_out(x, ya, hf, hr, gr, wa, wb, ff, final, tm=512):
    M = x.shape[0]
    row = lambda i: (i, 0)
    return pl.pallas_call(
        functools.partial(_even_out_kernel, final=final),
        grid=(M // tm,),
        in_specs=[pl.BlockSpec((tm, D_MODEL), row), pl.BlockSpec((tm, A_Q), row),
                  pl.BlockSpec((tm, B_WIDTH), row), pl.BlockSpec((tm, B_WIDTH), row),
                  pl.BlockSpec((tm, B_WIDTH), row),
                  _resident((A_Q, D_MODEL)), _resident((B_WIDTH, D_MODEL))] + _ff_specs(),
        out_specs=pl.BlockSpec((tm, D_MODEL), row),
        out_shape=jax.ShapeDtypeStruct((M, D_MODEL), F32),
        compiler_params=_cparams("parallel"),
        name="even_out_mlp",
    )(x, ya, hf, hr, gr, wa, wb, *ff)


def _rope_tables(S):
    rows = S // GRID_W
    row = jnp.repeat(jnp.arange(rows, dtype=F32), GRID_W)
    col = jnp.tile(jnp.arange(GRID_W, dtype=F32), rows)
    n_freq = A_HEAD_DIM // 4
    inv = ROPE_THETA ** (-jnp.arange(n_freq, dtype=F32) / n_freq)
    ang_r = row[:, None] * inv
    ang_c = col[:, None] * inv
    cos = jnp.concatenate([jnp.cos(ang_r)] * 2 + [jnp.cos(ang_c)] * 2, axis=1)
    sin = jnp.concatenate([-jnp.sin(ang_r), jnp.sin(ang_r), -jnp.sin(ang_c), jnp.sin(ang_c)], axis=1)
    return jnp.tile(cos, (1, 2)), jnp.tile(sin, (1, 2))


def _prep_even(w_in, w_out, qn, kn, conv_w, conv_b, wr, br, wi, bi, lam):
    G = A_HEADS // A_KV_HEADS
    wq = w_in[:, :A_Q].reshape(D_MODEL, A_HEADS, A_HEAD_DIM)
    zeros = jnp.zeros_like(wq)
    half = (jnp.arange(A_HEADS) // G)[None, :, None]
    wq_pad = jnp.concatenate([jnp.where(half == 0, wq, zeros), jnp.where(half == 1, wq, zeros)], axis=-1)
    w = jnp.concatenate([wq_pad.reshape(D_MODEL, A_HEADS * LANES), w_in[:, A_Q:A_Q + A_KV],
                         w_in[:, A_Q + 2 * A_KV:]], axis=1).astype(BF16)
    wvt = w_in[:, A_Q + A_KV:A_Q + 2 * A_KV].T.astype(BF16)
    qg =jnp.tile(qn.astype(F32), 2 * A_HEADS)[None]
    kg = jnp.tile(kn.astype(F32), 2)[None]

    def dense(blocks):
        eye = jnp.eye(B_BLOCKS, dtype=blocks.dtype)
        return jnp.einsum('nde,nm->ndme', blocks, eye).reshape(B_WIDTH, B_WIDTH)

    wg = jnp.stack([dense(wr[0]), dense(wi[0]), dense(wr[1]), dense(wi[1])]).astype(BF16)
    bg = jnp.stack([br[0], bi[0], br[1], bi[1]]).astype(F32)
    return dict(w=w, wvt=wvt, qg=qg, kg=kg, cw=conv_w.astype(F32), cb=conv_b.astype(F32)[None], wg=wg, bg=bg,
                lam=lam.astype(F32), wa=w_out[:A_Q].astype(BF16), wb=w_out[A_Q:].astype(BF16))


def _even_layer(x, gain, p, ff, final, cos, sin, B, S):
    q, k, vt, xr, gr = _even_proj(x, gain, p['w'], p['wvt'], p['qg'], p['kg'], cos, sin, S)
    ya = _gqa(q, k, vt, B, S)
    hf, hr = _rglru(xr, p['cw'], p['cb'], p['wg'], p['bg'], p['lam'], B, S)
    return _even_out(x, ya, hf, hr, gr, p['wa'], p['wb'], ff, final)


def _odd_proj_kernel(xp_ref, x_ref, xn_ref, g_ref, w_ref, cw_ref, nega_ref, dtb_ref, perm_ref,
                     cq_ref, ck_ref, cv_ref, z_ref, *rest, n_s):
    *d_refs, gb_ref = rest
    tm = x_ref.shape[0]
    i = pl.program_id(0)
    gain = g_ref[...]
    xn = [_rms(x_ref[r0:r0 + PROJ_ROWS, :], gain).astype(BF16) for r0 in range(0, tm, PROJ_ROWS)]
    ext = jnp.concatenate([_rms(xp_ref[...], gain).astype(BF16)] + xn + [_rms(xn_ref[...], gain).astype(BF16)],
                          axis=0)
    n_c = 3 * C_WIDTH
    pq = jnp.dot(ext, w_ref[:, :n_c], preferred_element_type=F32)
    proj = jnp.concatenate([jnp.dot(part, w_ref[:, n_c:], preferred_element_type=F32) for part in xn], axis=0)
    y = _silu(_conv4(pq[:HALO], pq[HALO:HALO + tm], pq[HALO + tm:], cw_ref[...], i % n_s == 0, i % n_s == n_s - 1))
    for h in range(C_HEADS):
        for part, ref, scale in ((0, cq_ref, C_HEAD_DIM ** -0.5), (1, ck_ref, 1.0)):
            a = y[:, part * C_WIDTH + h * LANES:part * C_WIDTH + (h + 1) * LANES]
            a = a * lax.rsqrt(jnp.sum(a * a, axis=-1, keepdims=True) + EPS)
            ref[:, h * LANES:(h + 1) * LANES] = a * scale
    cv_ref[...] = y[:, 2 * C_WIDTH:]
    z_ref[...] = proj[:, :C_WIDTH]
    off = C_WIDTH
    for ti, scale in enumerate((D_HEAD_DIM ** -0.5, 1.0, 1.0)):
        for gi, (_, dil) in enumerate(D_GROUPS):
            xg = (proj[:, off + gi * D_GW:off + (gi + 1) * D_GW] * scale).astype(BF16)
            ref = d_refs[ti * len(D_GROUPS) + gi]
            if dil == 1:
                ref[...] = xg
            else:
                y = jnp.dot(perm_ref[gi - 1], xg, preferred_element_type=F32)
                ref[...] = y.reshape(dil, tm // dil, D_GW).astype(BF16)
        off += D_WIDTH
    t = proj[:, off:off + LANES]
    lane = lax.broadcasted_iota(jnp.int32, t.shape, 1)
    gate = jnp.where(lane < 2 * C_HEADS, _sigmoid(t), nega_ref[...] * _softplus(t + dtb_ref[...]))
    gb_ref[...] = jnp.transpose(gate)[:4 * C_HEADS]


def _class_major_spec(dil, S, tm):
    n_s = S // tm
    return pl.BlockSpec((None, dil, tm // dil, D_GW), lambda i: (i // n_s, 0, i % n_s, 0))


def _odd_proj(x, gain, w, cw, nega, dtb, perm, B, S):
    tm = PERM_TM
    M = x.shape[0]
    row = lambda i: (i, 0)
    const = lambda i: (0, 0)
    hb = tm // HALO
    last_hb = M // HALO - 1
    d_specs, d_shapes = [], []
    for _ in range(3):
        for _, dil in D_GROUPS:
            if dil == 1:
                d_specs.append(pl.BlockSpec((tm, D_GW), row))
                d_shapes.append(jax.ShapeDtypeStruct((M, D_GW), BF16))
            else:
                d_specs.append(_class_major_spec(dil, S, tm))
                d_shapes.append(jax.ShapeDtypeStruct((B, dil, S // dil, D_GW), BF16))
    wide = pl.BlockSpec((tm, C_WIDTH), row)
    outs = pl.pallas_call(
        functools.partial(_odd_proj_kernel, n_s=S // tm),
        grid=(M // tm,),
        in_specs=[pl.BlockSpec((HALO, D_MODEL), lambda i: (jnp.maximum(i * hb - 1, 0), 0)),
                  pl.BlockSpec((tm, D_MODEL), row),
                  pl.BlockSpec((HALO, D_MODEL), lambda i: (jnp.minimum((i + 1) * hb, last_hb), 0)),
                  pl.BlockSpec((1, D_MODEL), const), pl.BlockSpec((D_MODEL, w.shape[1]), const),
                  pl.BlockSpec((4, 3 * C_WIDTH), const),
                  pl.BlockSpec((1, LANES), const), pl.BlockSpec((1, LANES), const),
                  pl.BlockSpec(perm.shape, lambda i: (0, 0, 0))],
        out_specs=[wide, wide, wide, wide] + d_specs + [pl.BlockSpec((4 * C_HEADS, tm), lambda i: (0, i))],
        out_shape=[jax.ShapeDtypeStruct((M, C_WIDTH), F32)] * 4 + d_shapes
        + [jax.ShapeDtypeStruct((4 * C_HEADS, M), F32)],
        compiler_params=_cparams("parallel"),
        name="odd_proj",
    )(x, x, x, gain, w, cw, nega, dtb, perm)
    ng = len(D_GROUPS)
    return outs[:4], outs[4:4 + ng], outs[4 + ng:4 + 2 * ng], outs[4 + 2 * ng:4 + 3 * ng], outs[-1]


def _nt_dot(a, b):
    return lax.dot_general(a, b, (((1,), (1,)), ((), ())), preferred_element_type=F32)


def _gdn_kernel(qf_ref, kf_ref, vf_ref, gf_ref, qr_ref, kr_ref, vr_ref, gr_ref, of_ref, or_ref, state, *, tm):
    C = C_CHUNK
    SB = LANES
    nsb = tm // SB
    nc = tm // C
    t = pl.program_id(1)

    @pl.when(t == 0)
    def _():
        state[...] = jnp.zeros_like(state)

    ri = lax.broadcasted_iota(jnp.int32, (SB, SB), 0)
    ci = lax.broadcasted_iota(jnp.int32, (SB, SB), 1)
    same = (ri // C) == (ci // C)
    eye = jnp.where(ri == ci, 1.0, 0.0)
    incl, strict, levels = [], [], []
    for d in range(2):
        hi_i, lo_i = (ci, ri) if d else (ri, ci)
        incl.append(same & (hi_i >= lo_i))
        strict.append(same & (hi_i > lo_i))
        lv = []
        s = 1
        while s < C:
            lv.append((((hi_i // s) % 2) == 1) & ((lo_i // s) == (hi_i // s) - 1))
            s *= 2
        levels.append(lv)

    pos = lax.broadcasted_iota(jnp.int32, (4 * C_HEADS, tm), 1) % C
    rows, cols = [], []
    for d, g_ref in enumerate((gf_ref, gr_ref)):
        gt = g_ref[...]
        x = gt
        s = 1
        while s < C:
            if d == 0:
                x = x + jnp.where(pos >= s, pltpu.roll(x, s, 1), 0.0)
            else:
                x = x + jnp.where(pos < C - s, pltpu.roll(x, tm - s, 1), 0.0)
            s *= 2
        lo = d * C_HEADS
        rw = jnp.concatenate([gt[lo:lo + C_HEADS], x[2 * C_HEADS + lo:2 * C_HEADS + lo + C_HEADS]], axis=0)
        rows.append(rw)
        cols.append(jnp.transpose(jnp.concatenate([rw, jnp.zeros((LANES - 2 * C_HEADS, tm), F32)], axis=0)))

    refs = ((qf_ref, kf_ref, vf_ref, of_ref), (qr_ref, kr_ref, vr_ref, or_ref))
    chains = []
    for d in range(2):
        for h in range(C_HEADS):
            for sb in range(nsb):
                rs = slice(sb * SB, (sb + 1) * SB)
                sl = slice(h * LANES, (h + 1) * LANES)
                q = refs[d][0][rs, sl]
                k = refs[d][1][rs, sl]
                v = refs[d][2][rs, sl]
                beta = jnp.broadcast_to(cols[d][rs, h:h + 1], (SB, LANES))
                gcc = jnp.broadcast_to(cols[d][rs, C_HEADS + h:C_HEADS + h + 1], (SB, LANES))
                gcr = jnp.broadcast_to(rows[d][C_HEADS + h:C_HEADS + h + 1, rs], (SB, SB))
                dec = jnp.exp(jnp.where(incl[d], gcc - gcr, NEG_INF))
                eg = jnp.exp(gcc)
                kbf = k.astype(BF16)
                kb = k * beta
                lmat = jnp.where(strict[d], _nt_dot(kb.astype(BF16), kbf) * dec, 0.0)
                amat = jnp.where(incl[d], _nt_dot(q.astype(BF16), kbf) * dec, 0.0).astype(BF16)
                chains.append(dict(d=d, h=h, sb=sb, k=k, gcc=gcc, lmat=lmat, amat=amat,
                                   rhs=jnp.concatenate([v * beta, kb * eg], axis=1).astype(BF16),
                                   qe=(q * eg).astype(BF16)))

    for ch in chains:
        ch['x'] = eye - jnp.where(levels[ch['d']][0], ch['lmat'], 0.0)
    for li in range(1, len(levels[0])):
        for ch in chains:
            ch['xb'] = ch['x'].astype(BF16)
            cmat = jnp.where(levels[ch['d']][li], ch['lmat'], 0.0).astype(BF16)
            ch['cx'] = jnp.dot(cmat, ch['xb'], preferred_element_type=F32).astype(BF16)
        for ch in chains:
            ch['x'] = ch['x'] - jnp.dot(ch['xb'], ch['cx'], preferred_element_type=F32)
    for ch in chains:
        sol = jnp.dot(ch['x'].astype(BF16), ch['rhs'], preferred_element_type=F32)
        ch['u'] = sol[:, :LANES]
        ch['w'] = sol[:, LANES:].astype(BF16)
        ch['vn'] = [None] * (SB // C)
        ch['ost'] = [None] * (SB // C)

    by_key = {(ch['d'], ch['h'], ch['sb']): ch for ch in chains}
    st = {(d, h): state[d, h] for d in range(2) for h in range(C_HEADS)}
    for step in range(nc):
        work = []
        for (d, h), s_val in st.items():
            c = nc - 1 - step if d else step
            ch = by_key[(d, h, c // (SB // C))]
            lc = c % (SB // C)
            rs = slice(lc * C, (lc + 1) * C)
            stb = s_val.astype(BF16)
            vn = ch['u'][rs] - jnp.dot(ch['w'][rs], stb, preferred_element_type=F32)
            ch['ost'][lc] = jnp.dot(ch['qe'][rs], stb, preferred_element_type=F32)
            ch['vn'][lc] = vn
            last = lc * C if d else (lc + 1) * C - 1
            gl = ch['gcc'][last:last + 1, :]
            kd = (ch['k'][rs] * jnp.exp(gl - ch['gcc'][rs])).astype(BF16)
            work.append(((d, h), s_val * jnp.exp(gl), kd, vn.astype(BF16)))
        for key, decayed, kd, vnb in work:
            st[key] = decayed + lax.dot_general(kd, vnb, (((0,), (0,)), ((), ())), preferred_element_type=F32)
    for (d, h), s_val in st.items():
        state[d, h] = s_val
    for ch in chains:
        rs = slice(ch['sb'] * SB, (ch['sb'] + 1) * SB)
        sl = slice(ch['h'] * LANES, (ch['h'] + 1) * LANES)
        vn_all = jnp.concatenate(ch['vn'], axis=0).astype(BF16)
        refs[ch['d']][3][rs, sl] = (jnp.concatenate(ch['ost'], axis=0)
                                   + jnp.dot(ch['amat'], vn_all, preferred_element_type=F32))


def _gdn(q, k, v, gbt, B, S, tm=512):
    M = q.shape[0]
    nblk = S // tm
    fwd = lambda b, t: b * nblk + t
    rev = lambda b, t: b * nblk + (nblk - 1 - t)
    specs = []
    for blk in (fwd, rev):
        specs += [pl.BlockSpec((tm, C_WIDTH), lambda b, t, blk=blk: (blk(b, t), 0))] * 3
        specs += [pl.BlockSpec((4 * C_HEADS, tm), lambda b, t, blk=blk: (0, blk(b, t)))]
    return pl.pallas_call(
        functools.partial(_gdn_kernel, tm=tm),
        grid=(B, nblk),
        in_specs=specs,
        out_specs=[pl.BlockSpec((tm, C_WIDTH), lambda b, t: (fwd(b, t), 0)),
                   pl.BlockSpec((tm, C_WIDTH), lambda b, t: (rev(b, t), 0))],
        out_shape=[jax.ShapeDtypeStruct((M, C_WIDTH), F32)] * 2,
        scratch_shapes=[pltpu.VMEM((2, C_HEADS, C_HEAD_DIM, C_HEAD_DIM), F32)],
        compiler_params=_cparams("arbitrary", "arbitrary"),
        name="gdn",
    )(q, k, v, gbt, q, k, v, gbt)


def _band_kernel(q_ref, kp_ref, kc_ref, kn_ref, vp_ref, vc_ref, vn_ref, bias_ref, o_ref, lse_ref, *, tq, m_len):
    t = pl.program_id(1)
    QT = BAND_QT
    nkeys = QT + 2 * D_STEPS
    kwin = jnp.concatenate([kp_ref[...], kc_ref[...], kn_ref[...]], axis=0)
    vwin = jnp.concatenate([vp_ref[...], vc_ref[...], vn_ref[...]], axis=0)
    lane = lax.broadcasted_iota(jnp.int32, (QT, D_GW), 1)
    mine = [(lane // D_HEAD_DIM) == h for h in range(D_HEADS_PER_GROUP)]
    col = lax.broadcasted_iota(jnp.int32, (QT, nkeys), 1)
    tiles = []
    for i in range(tq // QT):
        q = q_ref[i * QT:(i + 1) * QT, :]
        kpos = t * tq + i * QT - D_STEPS + col
        valid = (kpos >= 0) & (kpos < m_len)
        kw = kwin[i * QT:i * QT + nkeys]
        for h in range(D_HEADS_PER_GROUP):
            qh = jnp.where(mine[h], q, jnp.zeros_like(q))
            s = lax.dot_general(qh, kw, (((1,), (1,)), ((), ())), preferred_element_type=F32) + bias_ref[h]
            tiles.append([i, h, jnp.where(valid, s, NEG_INF)])
    for tile in tiles:
        s = tile[2]
        m = jnp.max(s, axis=-1, keepdims=True)
        e = jnp.exp(s - m)
        l = jnp.sum(e, axis=-1, keepdims=True)
        tile[2] = (e * (1.0 / l)).astype(BF16)
        tile.append(m + jnp.log(l))
    for i in range(tq // QT):
        o = jnp.zeros((QT, D_GW), F32)
        lse_full = jnp.zeros((QT, D_GW), F32)
        vw = vwin[i * QT:i * QT + nkeys]
        for _, h, p, lse in tiles[i * D_HEADS_PER_GROUP:(i + 1) * D_HEADS_PER_GROUP]:
            o = jnp.where(mine[h], jnp.dot(p, vw, preferred_element_type=F32), o)
            lse_full = jnp.where(mine[h], lse, lse_full)
        o_ref[i * QT:(i + 1) * QT, :] = o
        lse_ref[i * QT:(i + 1) * QT, :] = lse_full


def _band(q, k, v, bias, Z, m_len):
    M = q.shape[0]
    tq = min(BAND_TQ, m_len)
    nq = m_len // tq
    r = tq // D_STEPS
    nhb = M // D_STEPS
    cur = lambda z, t: (z * nq + t, 0)
    prev = lambda z, t: (jnp.maximum((z * nq + t) * r - 1, 0), 0)
    nxt = lambda z, t: (jnp.minimum((z * nq + t + 1) * r, nhb - 1), 0)
    halo = lambda f: pl.BlockSpec((D_STEPS, D_GW), f)
    full = pl.BlockSpec((tq, D_GW), cur)
    return pl.pallas_call(
        functools.partial(_band_kernel, tq=tq, m_len=m_len),
        grid=(Z, nq),
        in_specs=[full, halo(prev), full, halo(nxt), halo(prev), full, halo(nxt),
                  pl.BlockSpec((D_HEADS_PER_GROUP, BAND_QT, BAND_QT + 2 * D_STEPS), lambda z, t: (0, 0, 0))],
        out_specs=[full, full],
        out_shape=[jax.ShapeDtypeStruct((M, D_GW), F32)] * 2,
        compiler_params=_cparams("parallel", "parallel"),
        name="band_attn",
    )(q, k, k, k, v, v, v, bias)


def _odd_out_kernel(x_ref, of_ref, or_ref, z_ref, cg_ref, o0_ref, o1_ref, o2_ref, l0_ref, l1_ref, l2_ref,
                    unperm_ref, wc_ref, wd_ref, *rest, final):
    *ff, out_ref = rest
    tm = x_ref.shape[0]
    oc = of_ref[...] + or_ref[...]
    z = z_ref[...]
    parts = []
    for h in range(C_HEADS):
        sl = slice(h * LANES, (h + 1) * LANES)
        parts.append(_rms(oc[:, sl], cg_ref[...]) * _silu(z[:, sl]))
    yc = jnp.concatenate(parts, axis=1).astype(BF16)

    def token_order(ref, gi):
        if D_GROUPS[gi][1] == 1:
            return ref[...]
        val = ref[...].reshape(tm, D_GW)
        hi = val.astype(BF16)
        lo = (val - hi.astype(F32)).astype(BF16)
        pt = unperm_ref[gi - 1]
        return jnp.dot(pt, hi, preferred_element_type=F32) + jnp.dot(pt, lo, preferred_element_type=F32)

    o0, o1, o2 = (token_order(r, gi) for gi, r in enumerate((o0_ref, o1_ref, o2_ref)))
    l0, l1, l2 = (token_order(r, gi) for gi, r in enumerate((l0_ref, l1_ref, l2_ref)))
    m = jnp.maximum(jnp.maximum(l0, l1), l2)
    e0, e1, e2 = jnp.exp(l0 - m), jnp.exp(l1 - m), jnp.exp(l2 - m)
    inv = 1.0 / (e0 + e1 + e2)
    yd = ((e0 * o0 + e1 * o1 + e2 * o2) * inv).astype(BF16)
    y = jnp.dot(yc, wc_ref[...], preferred_element_type=F32) + jnp.dot(yd, wd_ref[...], preferred_element_type=F32)
    out_ref[...] = _mlp_tail(x_ref[...] + y, ff, final)


def _odd_out(x, o_f, o_r, z, cg, os_, ls_, unperm, wc, wd, ff, final, B, S):
    tm = PERM_TM
    M = x.shape[0]
    row = lambda i: (i, 0)
    wide = pl.BlockSpec((tm, C_WIDTH), row)
    grp = [pl.BlockSpec((tm, D_GW), row) if dil == 1 else _class_major_spec(dil, S, tm) for _, dil in D_GROUPS]
    return pl.pallas_call(
        functools.partial(_odd_out_kernel, final=final),
        grid=(M // tm,),
        in_specs=[pl.BlockSpec((tm, D_MODEL), row), wide, wide, wide, _resident((1, LANES))] + grp + grp
        + [_resident(unperm.shape), _resident((C_WIDTH, D_MODEL)), _resident((D_GW, D_MODEL))] + _ff_specs(),
        out_specs=pl.BlockSpec((tm, D_MODEL), row),
        out_shape=jax.ShapeDtypeStruct((M, D_MODEL), F32),
        compiler_params=_cparams("parallel"),
        name="odd_out_mlp",
    )(x, o_f, o_r, z, cg, *os_, *ls_, unperm, wc, wd, *ff)


def _class_perms():
    mats = []
    for _, dil in D_GROUPS[1:]:
        out_row = np.arange(PERM_TM)
        src = (out_row % (PERM_TM // dil)) * dil + out_row // (PERM_TM // dil)
        mats.append(np.eye(PERM_TM, dtype=np.float32)[src])
    perm = np.stack(mats)
    return jnp.asarray(perm, BF16), jnp.asarray(perm.transpose(0, 2, 1), BF16)


def _t5_bucket(rel):
    nb = N_BUCKETS // 2
    max_exact = nb // 2
    n = np.abs(rel)
    large = max_exact + (np.log(np.maximum(n, 1) / max_exact) / math.log(MAX_DISTANCE / max_exact)
                         * (nb - max_exact)).astype(np.int64)
    large = np.minimum(large, nb - 1)
    return (np.where(rel > 0, nb, 0) + np.where(n < max_exact, n, large)).astype(np.int32)


def _band_bias(rel_bias):
    nkeys = BAND_QT + 2 * D_STEPS
    tiles = []
    for gi, (window, dil) in enumerate(D_GROUPS):
        steps = window // (2 * dil)
        assert steps == D_STEPS
        buckets = _t5_bucket(np.arange(-steps, steps + 1) * dil)
        hsl = slice(gi * D_HEADS_PER_GROUP, (gi + 1) * D_HEADS_PER_GROUP)
        b = rel_bias.astype(F32)[jnp.asarray(buckets)][:, hsl].T
        period = jnp.concatenate([b, jnp.full((D_HEADS_PER_GROUP, nkeys + 1 - b.shape[1]), NEG_INF, F32)], axis=1)
        flat = jnp.tile(period, (1, BAND_QT))[:, :BAND_QT * nkeys]
        tiles.append(flat.reshape(D_HEADS_PER_GROUP, BAND_QT, nkeys))
    return jnp.stack(tiles)


def _prep_odd(w_in, w_out, conv_w, a_log, dt_bias, o_gain, rel_bias):
    n_c = 4 * C_WIDTH
    n_ba = 4 * C_HEADS
    w = jnp.concatenate([w_in[:, :n_c], w_in[:, n_c + n_ba:], w_in[:, n_c:n_c + n_ba],
                         jnp.zeros((D_MODEL, LANES - n_ba), w_in.dtype)], axis=1).astype(BF16)
    pad = jnp.zeros((LANES - n_ba,), F32)
    nega = jnp.concatenate([jnp.zeros((2 * C_HEADS,), F32), -jnp.exp(a_log.astype(F32)).reshape(-1), pad])[None]
    dtb = jnp.concatenate([jnp.zeros((2 * C_HEADS,), F32), dt_bias.astype(F32).reshape(-1), pad])[None]
    return dict(w=w, nega=nega, dtb=dtb, cw=conv_w.astype(F32), cg=o_gain.astype(F32)[None],
                bias=_band_bias(rel_bias), wc=w_out[:C_WIDTH].astype(BF16), wd=w_out[C_WIDTH:].astype(BF16))


def _odd_layer(x, gain, p, ff, final, B, S):
    perm, unperm = _class_perms()
    (cq, ck, cv, z), dq, dk, dv, gb = _odd_proj(x, gain, p['w'], p['cw'], p['nega'], p['dtb'], perm, B, S)
    o_f, o_r = _gdn(cq, ck, cv, gb, B, S)
    outs, lses = [], []
    for gi, (_, dil) in enumerate(D_GROUPS):
        flat = lambda a: a.reshape(B * S, D_GW)
        o, lse = _band(flat(dq[gi]), flat(dk[gi]), flat(dv[gi]), p['bias'][gi], B * dil, S // dil)
        shape = (B * S, D_GW) if dil == 1 else (B, dil, S // dil, D_GW)
        outs.append(o.reshape(shape))
        lses.append(lse.reshape(shape))
    return _odd_out(x, o_f, o_r, z, p['cg'], outs, lses, unperm, p['wc'], p['wd'], ff, final, B, S)


def _trunk(x3, evens, odds, norm_mix, norm_ff, norm_final, w1, w2):
    B, S, _ = x3.shape
    x = x3.reshape(B * S, D_MODEL)
    cos, sin = _rope_tables(S)
    for layer in range(DEPTH):
        gain = norm_mix[layer][None]
        ff = (norm_ff[layer][None], w1[layer], w2[layer], norm_final[None])
        final = layer == DEPTH - 1
        if layer % 2 == 0:
            x = _even_layer(x, gain, evens[layer // 2], ff, final, cos, sin, B, S)
        else:
            x = _odd_layer(x, gain, odds[layer // 2], ff, final, B, S)
    return x.reshape(B, S, D_MODEL)


def kernel(x_prompt, x_sample, rel_bias, norm_mix, norm_ff, norm_final, w_ff1, w_ff2, w_in_e, w_out_e,
           a_qnorm, a_knorm, b_conv_w, b_conv_b, b_wr, b_br, b_wi, b_bi, b_lambda, w_in_o, w_out_o,
           c_conv_w, c_a_log, c_dt_bias, c_norm):
    evens = [_prep_even(w_in_e[j], w_out_e[j], a_qnorm[j], a_knorm[j], b_conv_w[j], b_conv_b[j],
                        b_wr[j], b_br[j], b_wi[j], b_bi[j], b_lambda[j]) for j in range((DEPTH + 1) // 2)]
    odds = [_prep_odd(w_in_o[j], w_out_o[j], c_conv_w[j], c_a_log[j], c_dt_bias[j], c_norm[j], rel_bias)
            for j in range(DEPTH // 2)]
    w1 = w_ff1.astype(BF16)
    w2 = w_ff2.astype(BF16)
    nm = norm_mix.astype(F32)
    nf = norm_ff.astype(F32)
    ng = norm_final.astype(F32)
    y_prompt = _trunk(x_prompt, evens, odds, nm, nf, ng, w1, w2)
    y_sample = _trunk(x_sample, evens, odds, nm, nf, ng, w1, w2)
    return (y_prompt, y_sample)
```

```python
import functools
import math

import numpy as np
import jax
import jax.numpy as jnp
from jax import lax
from jax.experimental import pallas as pl
from jax.experimental.pallas import tpu as pltpu

F32 = jnp.float32
BF16 = jnp.bfloat16

D_MODEL = 1024
D_FF = 4 * D_MODEL
DEPTH = 4
EPS = 1e-6
NEG_INF = -1e30
GRID_W = 64
LANES = 128

A_HEADS = 8
A_KV_HEADS = 2
A_HEAD_DIM = 64
A_Q = A_HEADS * A_HEAD_DIM
A_KV = A_KV_HEADS * A_HEAD_DIM
ROPE_THETA = 10000.0
B_WIDTH = 512
B_BLOCKS = 8
RG_C = 8.0
C_HEADS = 4
C_HEAD_DIM = 128
C_WIDTH = C_HEADS * C_HEAD_DIM
C_CHUNK = 64
D_GROUPS = ((128, 1), (512, 4), (2048, 16))
D_HEADS_PER_GROUP = 4
D_HEAD_DIM = 64
D_NHEADS = len(D_GROUPS) * D_HEADS_PER_GROUP
D_WIDTH = D_NHEADS * D_HEAD_DIM
D_GW = D_HEADS_PER_GROUP * D_HEAD_DIM
D_STEPS = 64
N_BUCKETS = 32
MAX_DISTANCE = 1024
HALO = 8
Q_SCALE = A_HEAD_DIM ** -0.5 * math.log2(math.e)
MLP_TF = 1024
PROJ_ROWS = 256
PERM_TM = 512
BAND_QT = 128
BAND_TQ = 1024
GQA_TILE = 256
VT_PAD = 16
VT_ROWS = A_KV_HEADS * (A_HEAD_DIM + VT_PAD)

VMEM_LIMIT = 56 * 1024 * 1024


def _cparams(*sem):
    return pltpu.CompilerParams(dimension_semantics=sem, vmem_limit_bytes=VMEM_LIMIT)


def _rms(x, gain):
    return x * lax.rsqrt(jnp.mean(x * x, axis=-1, keepdims=True) + EPS) * gain


def _softplus(x):
    return jnp.maximum(x, 0.0) + jnp.log1p(jnp.exp(-jnp.abs(x)))


def _sigmoid(x):
    return 0.5 * jnp.tanh(0.5 * x) + 0.5


def _silu(x):
    return x * _sigmoid(x)


def _gelu_tanh(x):
    return 0.5 * x * (1.0 + jnp.tanh(math.sqrt(2.0 / math.pi) * (x + 0.044715 * (x * x * x))))


def _conv4(prev, cur, nxt, w, first, last):
    rows = cur.shape[0]
    prev = jnp.where(first, 0.0, prev)
    nxt = jnp.where(last, 0.0, nxt)
    full = jnp.concatenate([prev, cur, nxt], axis=0)
    y = full[HALO - 2:HALO - 2 + rows] * w[0:1]
    for j in range(1, 4):
        y = y + full[HALO - 2 + j:HALO - 2 + j + rows] * w[j:j + 1]
    return y


def _even_proj_kernel(x_ref, g_ref, w_ref, wvt_ref, qg_ref, kg_ref, cos_ref, sin_ref,
                      q_ref, k_ref, vt_ref, xr_ref, gr_ref):
    tm = x_ref.shape[0]
    parts = []
    for r0 in range(0, tm, PROJ_ROWS):
        xn = _rms(x_ref[r0:r0 + PROJ_ROWS, :], g_ref[...]).astype(BF16)
        proj = jnp.dot(xn, w_ref[...], preferred_element_type=F32)
        vt = lax.dot_general(wvt_ref[...], xn, (((1,), (1,)), ((), ())), preferred_element_type=F32)
        parts.append((r0, proj, vt))
    lane = lax.broadcasted_iota(jnp.int32, (PROJ_ROWS, LANES), 1)
    first_half = (lane % 32) < 16
    lo = lane < A_HEAD_DIM
    ones = jnp.ones((VT_PAD, LANES), BF16)
    for r0, proj, vt in parts:
        rows = slice(r0, r0 + PROJ_ROWS)
        cos = cos_ref[rows, :]
        sin = sin_ref[rows, :]

        def rope(t):
            swapped = jnp.where(first_half, pltpu.roll(t, LANES - 16, 1), pltpu.roll(t, 16, 1))
            return t * cos + swapped * sin

        vt = vt.astype(BF16)
        for c in range(PROJ_ROWS // LANES):
            cols = slice(c * LANES, (c + 1) * LANES)
            vt_ref[r0 // LANES + c] = jnp.concatenate(
                [vt[:A_HEAD_DIM, cols], ones, vt[A_HEAD_DIM:, cols], ones], axis=0)
        for h in range(A_HEADS):
            t = proj[:, h * LANES:(h + 1) * LANES]
            r = lax.rsqrt(jnp.sum(t * t, axis=-1, keepdims=True) * (1.0 / A_HEAD_DIM) + EPS)
            t = rope(t * r * qg_ref[:, h * LANES:(h + 1) * LANES])
            q_ref[rows, h * LANES:(h + 1) * LANES] = (t * Q_SCALE).astype(BF16)
        off = A_HEADS * LANES
        t = proj[:, off:off + LANES]
        t2 = t * t
        s_lo = jnp.sum(jnp.where(lo, t2, 0.0), axis=-1, keepdims=True)
        s_hi = jnp.sum(jnp.where(lo, 0.0, t2), axis=-1, keepdims=True)
        r = lax.rsqrt(jnp.where(lo, s_lo, s_hi) * (1.0 / A_HEAD_DIM) + EPS)
        k_ref[rows, :] = rope(t * r * kg_ref[...]).astype(BF16)
        off += LANES
        xr_ref[rows, :] = proj[:, off:off + B_WIDTH]
        off += B_WIDTH
        gr_ref[rows, :] = proj[:, off:off + B_WIDTH]


def _even_proj(x, gain, w, wvt, qg, kg, cos, sin, S, tm=512):
    M = x.shape[0]
    nS = S // tm
    n_in = w.shape[1]
    row = lambda i: (i, 0)
    const = lambda i: (0, 0)
    return pl.pallas_call(
        _even_proj_kernel,
        grid=(M // tm,),
        in_specs=[pl.BlockSpec((tm, D_MODEL), row), pl.BlockSpec((1, D_MODEL), const),
                  pl.BlockSpec((D_MODEL, n_in), const), pl.BlockSpec((A_KV, D_MODEL), const),
                  pl.BlockSpec((1, A_HEADS * LANES), const), pl.BlockSpec((1, LANES), const),
                  pl.BlockSpec((tm, LANES), lambda i: (i % nS, 0)),
                  pl.BlockSpec((tm, LANES), lambda i: (i % nS, 0))],
        out_specs=[pl.BlockSpec((tm, A_HEADS * LANES), row), pl.BlockSpec((tm, LANES), row),
                   pl.BlockSpec((tm // LANES, VT_ROWS, LANES), lambda i: (i, 0, 0)),
                   pl.BlockSpec((tm, B_WIDTH), row), pl.BlockSpec((tm, B_WIDTH), row)],
        out_shape=[jax.ShapeDtypeStruct((M, A_HEADS * LANES), BF16),
                   jax.ShapeDtypeStruct((M, LANES), BF16),
                   jax.ShapeDtypeStruct((M // LANES, VT_ROWS, LANES), BF16),
                   jax.ShapeDtypeStruct((M, B_WIDTH), F32), jax.ShapeDtypeStruct((M, B_WIDTH), F32)],
        compiler_params=_cparams("parallel"),
        name="even_proj",
    )(x, gain, w, wvt, qg, kg, cos, sin)


def _gqa_kernel(q_ref, k_ref, vt_ref, o_ref, m_ref, acc_ref, *, tq, kstep, nstep, unroll, lookahead):
    hpt = GQA_TILE // tq
    ntile = A_HEADS // hpt
    qt = [jnp.concatenate([q_ref[:, h * LANES:(h + 1) * LANES] for h in range(p * hpt, (p + 1) * hpt)], axis=0)
          for p in range(ntile)]
    m_ref[...] = jnp.full(m_ref.shape, NEG_INF, F32)
    acc_ref[...] = jnp.zeros(acc_ref.shape, F32)
    vrows = VT_ROWS // A_KV_HEADS

    per = kstep // LANES

    def scores(j, u, p):
        ks = k_ref[pl.ds(pl.multiple_of((j * unroll + u) * kstep, kstep), kstep), :]
        return lax.dot_general(ks, qt[p], (((1,), (1,)), ((), ())), preferred_element_type=F32)

    def update(j, u, p, st):
        g = (p * hpt) // (A_HEADS // A_KV_HEADS)
        m_old = m_ref[p]
        m_new = jnp.maximum(m_old, jnp.max(st, axis=0, keepdims=True))
        alpha = jnp.exp2(m_old - m_new)
        pt = jnp.exp2(st - m_new).astype(BF16)
        first = (j * unroll + u) * per
        vts = jnp.concatenate([vt_ref[first + i, g * vrows:(g + 1) * vrows, :] for i in range(per)], axis=1)
        acc_ref[p] = alpha * acc_ref[p] + jnp.dot(vts, pt, preferred_element_type=F32)
        m_ref[p] = m_new

    pairs = [(u, p) for u in range(unroll) for p in range(ntile)]
    niter = nstep // unroll

    def body(j, pending):
        pending = list(pending)
        j_next = jnp.minimum(j + 1, niter - 1)
        for idx in range(len(pairs)):
            ahead = idx + lookahead
            if ahead < len(pairs):
                pending.append(scores(j, *pairs[ahead]))
            else:
                pending.append(scores(j_next, *pairs[ahead - len(pairs)]))
            update(j, *pairs[idx], pending.pop(0))
        return tuple(pending)

    lax.fori_loop(0, niter, body, tuple(scores(0, *pairs[i]) for i in range(lookahead)))
    heads = []
    for p in range(ntile):
        acc = acc_ref[p]
        o = acc[:A_HEAD_DIM] / acc[A_HEAD_DIM:A_HEAD_DIM + 1]
        heads += [o[:, i * tq:(i + 1) * tq] for i in range(hpt)]
    for e in range(A_HEADS // 2):
        both = jnp.concatenate(heads[2 * e:2 * e + 2], axis=0)
        o_ref[:, e * LANES:(e + 1) * LANES] = jnp.transpose(both).astype(o_ref.dtype)


def _gqa(q, k, vt, B, S, kstep=256, lookahead=5):
    tq, unroll = (GQA_TILE, 8) if S <= 2048 else (GQA_TILE // 2, 32)
    M = q.shape[0]
    nq = S // tq
    nslab = S // LANES
    nstep = S // kstep
    unroll = min(unroll, nstep)
    kern = functools.partial(_gqa_kernel, tq=tq, kstep=kstep, nstep=nstep, unroll=unroll, lookahead=lookahead)
    return pl.pallas_call(
        kern,
        grid=(B, nq),
        in_specs=[pl.BlockSpec((tq, A_HEADS * LANES), lambda b, i: (b * nq + i, 0)),
                  pl.BlockSpec((S, LANES), lambda b, i: (b, 0)),
                  pl.BlockSpec((nslab, VT_ROWS, LANES), lambda b, i: (b, 0, 0))],
        out_specs=pl.BlockSpec((tq, A_Q), lambda b, i: (b * nq + i, 0)),
        out_shape=jax.ShapeDtypeStruct((M, A_Q), BF16),
        scratch_shapes=[pltpu.VMEM((A_HEADS * tq // GQA_TILE, 1, GQA_TILE), F32),
                        pltpu.VMEM((A_HEADS * tq // GQA_TILE, VT_ROWS // A_KV_HEADS, GQA_TILE), F32)],
        compiler_params=_cparams("parallel", "parallel"),
        name="gqa",
    )(q, k, vt)


def _rglru_kernel(xp_ref, xc_ref, xn_ref, yp_ref, yc_ref, yn_ref, cw_ref, cb_ref, wg_ref, bg_ref, lam_ref,
                  hf_ref, hr_ref, a_s, u_s, carry_s, *, tm, nblk):
    t = pl.program_id(1)

    @pl.when(t == 0)
    def _():
        carry_s[...] = jnp.zeros_like(carry_s)

    cw = cw_ref[...]
    ngrp = tm // 8
    row8 = lax.broadcasted_iota(jnp.int32, (ngrp, 8, B_WIDTH), 1)
    for d, (p_ref, c_ref, n_ref) in enumerate(((xp_ref, xc_ref, xn_ref), (yp_ref, yc_ref, yn_ref))):
        tt = t if d == 0 else nblk - 1 - t
        xc = _conv4(p_ref[...], c_ref[...], n_ref[...], cw, tt == 0, tt == nblk - 1) + cb_ref[...]
        xb = xc.astype(BF16)
        r = _sigmoid(jnp.dot(xb, wg_ref[2 * d], preferred_element_type=F32) + bg_ref[2 * d:2 * d + 1])
        i = _sigmoid(jnp.dot(xb, wg_ref[2 * d + 1], preferred_element_type=F32) + bg_ref[2 * d + 1:2 * d + 2])
        log_a = (-RG_C) * r * _softplus(-lam_ref[d:d + 1])
        a = jnp.exp(log_a).reshape(ngrp, 8, B_WIDTH)
        u = (jnp.sqrt(1.0 - jnp.exp(2.0 * log_a)) * (i * xc)).reshape(ngrp, 8, B_WIDTH)
        for sh in (1, 2, 4):
            keep = (row8 >= sh) if d == 0 else (row8 < 8 - sh)
            shift = sh if d == 0 else 8 - sh
            u = u + a * jnp.where(keep, pltpu.roll(u, shift, 1), 0.0)
            a = a * jnp.where(keep, pltpu.roll(a, shift, 1), 1.0)
        a_s[d] = a.reshape(tm, B_WIDTH)
        u_s[d] = u.reshape(tm, B_WIDTH)

    def body(gidx, carry):
        out = []
        for d, out_ref in enumerate((hf_ref, hr_ref)):
            gg = gidx if d == 0 else ngrp - 1 - gidx
            start = pl.multiple_of(gg * 8, 8)
            hg = u_s[d, pl.ds(start, 8), :] + a_s[d, pl.ds(start, 8), :] * carry[d]
            out_ref[pl.ds(start, 8), :] = hg
            edge = hg[7:8, :] if d == 0 else hg[0:1, :]
            out.append(jnp.broadcast_to(edge, (8, B_WIDTH)))
        return tuple(out)

    ends = lax.fori_loop(0, ngrp, body, (carry_s[0], carry_s[1]), unroll=4)
    carry_s[0] = ends[0]
    carry_s[1] = ends[1]


def _rglru(xr, cw, cb, wg, bg, lam, B, S, tm=512):
    M = xr.shape[0]
    nblk = S // tm
    r = tm // HALO
    nrb = M // HALO
    fwd = lambda b, t: b * nblk + t
    rev = lambda b, t: b * nblk + (nblk - 1 - t)
    specs = []
    for blk in (fwd, rev):
        specs += [pl.BlockSpec((HALO, B_WIDTH), lambda b, t, blk=blk: (jnp.maximum(blk(b, t) * r - 1, 0), 0)),
                  pl.BlockSpec((tm, B_WIDTH), lambda b, t, blk=blk: (blk(b, t), 0)),
                  pl.BlockSpec((HALO, B_WIDTH), lambda b, t, blk=blk: (jnp.minimum((blk(b, t) + 1) * r, nrb - 1), 0))]
    const2 = lambda b, t: (0, 0)
    const3 = lambda b, t: (0, 0, 0)
    kern = functools.partial(_rglru_kernel, tm=tm, nblk=nblk)
    return pl.pallas_call(
        kern,
        grid=(B, nblk),
        in_specs=specs + [pl.BlockSpec((4, B_WIDTH), const2), pl.BlockSpec((1, B_WIDTH), const2),
                          pl.BlockSpec((4, B_WIDTH, B_WIDTH), const3), pl.BlockSpec((4, B_WIDTH), const2),
                          pl.BlockSpec((2, B_WIDTH), const2)],
        out_specs=[pl.BlockSpec((tm, B_WIDTH), lambda b, t: (fwd(b, t), 0)),
                   pl.BlockSpec((tm, B_WIDTH), lambda b, t: (rev(b, t), 0))],
        out_shape=[jax.ShapeDtypeStruct((M, B_WIDTH), F32), jax.ShapeDtypeStruct((M, B_WIDTH), F32)],
        scratch_shapes=[pltpu.VMEM((2, tm, B_WIDTH), F32), pltpu.VMEM((2, tm, B_WIDTH), F32),
                        pltpu.VMEM((2, 8, B_WIDTH), F32)],
        compiler_params=_cparams("arbitrary", "arbitrary"),
        name="rglru",
    )(xr, xr, xr, xr, xr, xr, cw, cb, wg, bg, lam)


def _mlp_tail(x, ff, final):
    gff_ref, w1_ref, w2_ref, gfin_ref = ff
    xn = _rms(x, gff_ref[...]).astype(BF16)
    acc = x
    for c in range(D_FF // MLP_TF):
        h = jnp.dot(xn, w1_ref[:, c * MLP_TF:(c + 1) * MLP_TF], preferred_element_type=F32)
        h = jnp.square(jnp.maximum(h, 0.0)).astype(BF16)
        acc = acc + jnp.dot(h, w2_ref[c * MLP_TF:(c + 1) * MLP_TF, :], preferred_element_type=F32)
    return _rms(acc, gfin_ref[...]) if final else acc


def _resident(shape):
    return pl.BlockSpec(shape, lambda *_: (0,) * len(shape), pipeline_mode=pl.Buffered(1))


def _ff_specs():
    return [_resident((1, D_MODEL)), _resident((D_MODEL, D_FF)), _resident((D_FF, D_MODEL)),
            _resident((1, D_MODEL))]


def _even_out_kernel(x_ref, ya_ref, hf_ref, hr_ref, gr_ref, wa_ref, wb_ref, *rest, final):
    *ff, o_ref = rest
    yb = ((hf_ref[...] + hr_ref[...]) * _gelu_tanh(gr_ref[...])).astype(BF16)
    y = jnp.dot(ya_ref[...], wa_ref[...], preferred_element_type=F32)
    y = y + jnp.dot(yb, wb_ref[...], preferred_element_type=F32)
    o_ref[...] = _mlp_tail(x_ref[...] + y, ff, final)


def _even_out(x, ya, hf, hr, gr, wa, wb, ff, final, tm=512):
    M = x.shape[0]
    row = lambda i: (i, 0)
    return pl.pallas_call(
        functools.partial(_even_out_kernel, final=final),
        grid=(M // tm,),
        in_specs=[pl.BlockSpec((tm, D_MODEL), row), pl.BlockSpec((tm, A_Q), row),
                  pl.BlockSpec((tm, B_WIDTH), row), pl.BlockSpec((tm, B_WIDTH), row),
                  pl.BlockSpec((tm, B_WIDTH), row),
                  _resident((A_Q, D_MODEL)), _resident((B_WIDTH, D_MODEL))] + _ff_specs(),
        out_specs=pl.BlockSpec((tm, D_MODEL), row),
        out_shape=jax.ShapeDtypeStruct((M, D_MODEL), F32),
        compiler_params=_cparams("parallel"),
        name="even_out_mlp",
    )(x, ya, hf, hr, gr, wa, wb, *ff)


def _rope_tables(S):
    rows = S // GRID_W
    row = jnp.repeat(jnp.arange(rows, dtype=F32), GRID_W)
    col = jnp.tile(jnp.arange(GRID_W, dtype=F32), rows)
    n_freq = A_HEAD_DIM // 4
    inv = ROPE_THETA ** (-jnp.arange(n_freq, dtype=F32) / n_freq)
    ang_r = row[:, None] * inv
    ang_c = col[:, None] * inv
    cos = jnp.concatenate([jnp.cos(ang_r)] * 2 + [jnp.cos(ang_c)] * 2, axis=1)
    sin = jnp.concatenate([-jnp.sin(ang_r), jnp.sin(ang_r), -jnp.sin(ang_c), jnp.sin(ang_c)], axis=1)
    return jnp.tile(cos, (1, 2)), jnp.tile(sin, (1, 2))


def _prep_even(w_in, w_out, qn, kn, conv_w, conv_b, wr, br, wi, bi, lam):
    G = A_HEADS // A_KV_HEADS
    wq = w_in[:, :A_Q].reshape(D_MODEL, A_HEADS, A_HEAD_DIM)
    zeros = jnp.zeros_like(wq)
    half = (jnp.arange(A_HEADS) // G)[None, :, None]
    wq_pad = jnp.concatenate([jnp.where(half == 0, wq, zeros), jnp.where(half == 1, wq, zeros)], axis=-1)
    w = jnp.concatenate([wq_pad.reshape(D_MODEL, A_HEADS * LANES), w_in[:, A_Q:A_Q + A_KV],
                         w_in[:, A_Q + 2 * A_KV:]], axis=1).astype(BF16)
    wvt = w_in[:, A_Q + A_KV:A_Q + 2 * A_KV].T.astype(BF16)
    qg =jnp.tile(qn.astype(F32), 2 * A_HEADS)[None]
    kg = jnp.tile(kn.astype(F32), 2)[None]

    def dense(blocks):
        eye = jnp.eye(B_BLOCKS, dtype=blocks.dtype)
        return jnp.einsum('nde,nm->ndme', blocks, eye).reshape(B_WIDTH, B_WIDTH)

    wg = jnp.stack([dense(wr[0]), dense(wi[0]), dense(wr[1]), dense(wi[1])]).astype(BF16)
    bg = jnp.stack([br[0], bi[0], br[1], bi[1]]).astype(F32)
    return dict(w=w, wvt=wvt, qg=qg, kg=kg, cw=conv_w.astype(F32), cb=conv_b.astype(F32)[None], wg=wg, bg=bg,
                lam=lam.astype(F32), wa=w_out[:A_Q].astype(BF16), wb=w_out[A_Q:].astype(BF16))


def _even_layer(x, gain, p, ff, final, cos, sin, B, S):
    q, k, vt, xr, gr = _even_proj(x, gain, p['w'], p['wvt'], p['qg'], p['kg'], cos, sin, S)
    ya = _gqa(q, k, vt, B, S)
    hf, hr = _rglru(xr, p['cw'], p['cb'], p['wg'], p['bg'], p['lam'], B, S)
    return _even_out(x, ya, hf, hr, gr, p['wa'], p['wb'], ff, final)


def _odd_proj_kernel(xp_ref, x_ref, xn_ref, g_ref, w_ref, cw_ref, nega_ref, dtb_ref, perm_ref,
                     cq_ref, ck_ref, cv_ref, z_ref, *rest, n_s):
    *d_refs, gb_ref = rest
    tm = x_ref.shape[0]
    i = pl.program_id(0)
    gain = g_ref[...]
    xn = _rms(x_ref[...], gain).astype(BF16)
    ext = jnp.concatenate([_rms(xp_ref[...], gain).astype(BF16), xn, _rms(xn_ref[...], gain).astype(BF16)], axis=0)
    n_c = 3 * C_WIDTH
    pq = jnp.dot(ext, w_ref[:, :n_c], preferred_element_type=F32)
    proj = jnp.dot(xn, w_ref[:, n_c:], preferred_element_type=F32)
    y = _silu(_conv4(pq[:HALO], pq[HALO:HALO + tm], pq[HALO + tm:], cw_ref[...], i % n_s == 0, i % n_s == n_s - 1))
    for h in range(C_HEADS):
        for part, ref, scale in ((0, cq_ref, C_HEAD_DIM ** -0.5), (1, ck_ref, 1.0)):
            a = y[:, part * C_WIDTH + h * LANES:part * C_WIDTH + (h + 1) * LANES]
            a = a * lax.rsqrt(jnp.sum(a * a, axis=-1, keepdims=True) + EPS)
            ref[:, h * LANES:(h + 1) * LANES] = a * scale
    cv_ref[...] = y[:, 2 * C_WIDTH:]
    z_ref[...] = proj[:, :C_WIDTH]
    off = C_WIDTH
    for ti, scale in enumerate((D_HEAD_DIM ** -0.5, 1.0, 1.0)):
        for gi, (_, dil) in enumerate(D_GROUPS):
            xg = (proj[:, off + gi * D_GW:off + (gi + 1) * D_GW] * scale).astype(BF16)
            ref = d_refs[ti * len(D_GROUPS) + gi]
            if dil == 1:
                ref[...] = xg
            else:
                y = jnp.dot(perm_ref[gi - 1], xg, preferred_element_type=F32)
                ref[...] = y.reshape(dil, tm // dil, D_GW).astype(BF16)
        off += D_WIDTH
    t = proj[:, off:off + LANES]
    lane = lax.broadcasted_iota(jnp.int32, t.shape, 1)
    gate = jnp.where(lane < 2 * C_HEADS, _sigmoid(t), nega_ref[...] * _softplus(t + dtb_ref[...]))
    gb_ref[...] = jnp.transpose(gate)[:4 * C_HEADS]


def _class_major_spec(dil, S, tm):
    n_s = S // tm
    return pl.BlockSpec((None, dil, tm // dil, D_GW), lambda i: (i // n_s, 0, i % n_s, 0))


def _odd_proj(x, gain, w, cw, nega, dtb, perm, B, S):
    tm = PERM_TM
    M = x.shape[0]
    row = lambda i: (i, 0)
    const = lambda i: (0, 0)
    hb = tm // HALO
    last_hb = M // HALO - 1
    d_specs, d_shapes = [], []
    for _ in range(3):
        for _, dil in D_GROUPS:
            if dil == 1:
                d_specs.append(pl.BlockSpec((tm, D_GW), row))
                d_shapes.append(jax.ShapeDtypeStruct((M, D_GW), BF16))
            else:
                d_specs.append(_class_major_spec(dil, S, tm))
                d_shapes.append(jax.ShapeDtypeStruct((B, dil, S // dil, D_GW), BF16))
    wide = pl.BlockSpec((tm, C_WIDTH), row)
    outs = pl.pallas_call(
        functools.partial(_odd_proj_kernel, n_s=S // tm),
        grid=(M // tm,),
        in_specs=[pl.BlockSpec((HALO, D_MODEL), lambda i: (jnp.maximum(i * hb - 1, 0), 0)),
                  pl.BlockSpec((tm, D_MODEL), row),
                  pl.BlockSpec((HALO, D_MODEL), lambda i: (jnp.minimum((i + 1) * hb, last_hb), 0)),
                  pl.BlockSpec((1, D_MODEL), const), pl.BlockSpec((D_MODEL, w.shape[1]), const),
                  pl.BlockSpec((4, 3 * C_WIDTH), const),
                  pl.BlockSpec((1, LANES), const), pl.BlockSpec((1, LANES), const),
                  pl.BlockSpec(perm.shape, lambda i: (0, 0, 0))],
        out_specs=[wide, wide, wide, wide] + d_specs + [pl.BlockSpec((4 * C_HEADS, tm), lambda i: (0, i))],
        out_shape=[jax.ShapeDtypeStruct((M, C_WIDTH), F32)] * 4 + d_shapes
        + [jax.ShapeDtypeStruct((4 * C_HEADS, M), F32)],
        compiler_params=_cparams("parallel"),
        name="odd_proj",
    )(x, x, x, gain, w, cw, nega, dtb, perm)
    ng = len(D_GROUPS)
    return outs[:4], outs[4:4 + ng], outs[4 + ng:4 + 2 * ng], outs[4 + 2 * ng:4 + 3 * ng], outs[-1]


def _nt_dot(a, b):
    return lax.dot_general(a, b, (((1,), (1,)), ((), ())), preferred_element_type=F32)


def _gdn_kernel(qf_ref, kf_ref, vf_ref, gf_ref, qr_ref, kr_ref, vr_ref, gr_ref, of_ref, or_ref, state, *, tm):
    C = C_CHUNK
    SB = LANES
    nsb = tm // SB
    nc = tm // C
    t = pl.program_id(1)

    @pl.when(t == 0)
    def _():
        state[...] = jnp.zeros_like(state)

    ri = lax.broadcasted_iota(jnp.int32, (SB, SB), 0)
    ci = lax.broadcasted_iota(jnp.int32, (SB, SB), 1)
    same = (ri // C) == (ci // C)
    eye = jnp.where(ri == ci, 1.0, 0.0)
    incl, strict, levels = [], [], []
    for d in range(2):
        hi_i, lo_i = (ci, ri) if d else (ri, ci)
        incl.append(same & (hi_i >= lo_i))
        strict.append(same & (hi_i > lo_i))
        lv = []
        s = 1
        while s < C:
            lv.append((((hi_i // s) % 2) == 1) & ((lo_i // s) == (hi_i // s) - 1))
            s *= 2
        levels.append(lv)

    pos = lax.broadcasted_iota(jnp.int32, (4 * C_HEADS, tm), 1) % C
    rows, cols = [], []
    for d, g_ref in enumerate((gf_ref, gr_ref)):
        gt = g_ref[...]
        x = gt
        s = 1
        while s < C:
            if d == 0:
                x = x + jnp.where(pos >= s, pltpu.roll(x, s, 1), 0.0)
            else:
                x = x + jnp.where(pos < C - s, pltpu.roll(x, tm - s, 1), 0.0)
            s *= 2
        lo = d * C_HEADS
        rw = jnp.concatenate([gt[lo:lo + C_HEADS], x[2 * C_HEADS + lo:2 * C_HEADS + lo + C_HEADS]], axis=0)
        rows.append(rw)
        cols.append(jnp.transpose(jnp.concatenate([rw, jnp.zeros((LANES - 2 * C_HEADS, tm), F32)], axis=0)))

    refs = ((qf_ref, kf_ref, vf_ref, of_ref), (qr_ref, kr_ref, vr_ref, or_ref))
    chains = []
    for d in range(2):
        for h in range(C_HEADS):
            for sb in range(nsb):
                rs = slice(sb * SB, (sb + 1) * SB)
                sl = slice(h * LANES, (h + 1) * LANES)
                q = refs[d][0][rs, sl]
                k = refs[d][1][rs, sl]
                v = refs[d][2][rs, sl]
                beta = jnp.broadcast_to(cols[d][rs, h:h + 1], (SB, LANES))
                gcc = jnp.broadcast_to(cols[d][rs, C_HEADS + h:C_HEADS + h + 1], (SB, LANES))
                gcr = jnp.broadcast_to(rows[d][C_HEADS + h:C_HEADS + h + 1, rs], (SB, SB))
                dec = jnp.exp(jnp.where(incl[d], gcc - gcr, NEG_INF))
                eg = jnp.exp(gcc)
                kbf = k.astype(BF16)
                kb = k * beta
                lmat = jnp.where(strict[d], _nt_dot(kb.astype(BF16), kbf) * dec, 0.0)
                amat = jnp.where(incl[d], _nt_dot(q.astype(BF16), kbf) * dec, 0.0).astype(BF16)
                chains.append(dict(d=d, h=h, sb=sb, k=k, gcc=gcc, lmat=lmat, amat=amat,
                                   rhs=jnp.concatenate([v * beta, kb * eg], axis=1).astype(BF16),
                                   qe=(q * eg).astype(BF16)))

    for ch in chains:
        ch['x'] = eye - jnp.where(levels[ch['d']][0], ch['lmat'], 0.0)
    for li in range(1, len(levels[0])):
        for ch in chains:
            ch['xb'] = ch['x'].astype(BF16)
            cmat = jnp.where(levels[ch['d']][li], ch['lmat'], 0.0).astype(BF16)
            ch['cx'] = jnp.dot(cmat, ch['xb'], preferred_element_type=F32).astype(BF16)
        for ch in chains:
            ch['x'] = ch['x'] - jnp.dot(ch['xb'], ch['cx'], preferred_element_type=F32)
    for ch in chains:
        sol = jnp.dot(ch['x'].astype(BF16), ch['rhs'], preferred_element_type=F32)
        ch['u'] = sol[:, :LANES]
        ch['w'] = sol[:, LANES:].astype(BF16)
        ch['vn'] = [None] * (SB // C)
        ch['ost'] = [None] * (SB // C)

    by_key = {(ch['d'], ch['h'], ch['sb']): ch for ch in chains}
    st = {(d, h): state[d, h] for d in range(2) for h in range(C_HEADS)}
    for step in range(nc):
        work = []
        for (d, h), s_val in st.items():
            c = nc - 1 - step if d else step
            ch = by_key[(d, h, c // (SB // C))]
            lc = c % (SB // C)
            rs = slice(lc * C, (lc + 1) * C)
            stb = s_val.astype(BF16)
            vn = ch['u'][rs] - jnp.dot(ch['w'][rs], stb, preferred_element_type=F32)
            ch['ost'][lc] = jnp.dot(ch['qe'][rs], stb, preferred_element_type=F32)
            ch['vn'][lc] = vn
            last = lc * C if d else (lc + 1) * C - 1
            gl = ch['gcc'][last:last + 1, :]
            kd = (ch['k'][rs] * jnp.exp(gl - ch['gcc'][rs])).astype(BF16)
            work.append(((d, h), s_val * jnp.exp(gl), kd, vn.astype(BF16)))
        for key, decayed, kd, vnb in work:
            st[key] = decayed + lax.dot_general(kd, vnb, (((0,), (0,)), ((), ())), preferred_element_type=F32)
    for (d, h), s_val in st.items():
        state[d, h] = s_val
    for ch in chains:
        rs = slice(ch['sb'] * SB, (ch['sb'] + 1) * SB)
        sl = slice(ch['h'] * LANES, (ch['h'] + 1) * LANES)
        vn_all = jnp.concatenate(ch['vn'], axis=0).astype(BF16)
        refs[ch['d']][3][rs, sl] = (jnp.concatenate(ch['ost'], axis=0)
                                   + jnp.dot(ch['amat'], vn_all, preferred_element_type=F32))


def _gdn(q, k, v, gbt, B, S, tm=512):
    M = q.shape[0]
    nblk = S // tm
    fwd = lambda b, t: b * nblk + t
    rev = lambda b, t: b * nblk + (nblk - 1 - t)
    specs = []
    for blk in (fwd, rev):
        specs += [pl.BlockSpec((tm, C_WIDTH), lambda b, t, blk=blk: (blk(b, t), 0))] * 3
        specs += [pl.BlockSpec((4 * C_HEADS, tm), lambda b, t, blk=blk: (0, blk(b, t)))]
    return pl.pallas_call(
        functools.partial(_gdn_kernel, tm=tm),
        grid=(B, nblk),
        in_specs=specs,
        out_specs=[pl.BlockSpec((tm, C_WIDTH), lambda b, t: (fwd(b, t), 0)),
                   pl.BlockSpec((tm, C_WIDTH), lambda b, t: (rev(b, t), 0))],
        out_shape=[jax.ShapeDtypeStruct((M, C_WIDTH), F32)] * 2,
        scratch_shapes=[pltpu.VMEM((2, C_HEADS, C_HEAD_DIM, C_HEAD_DIM), F32)],
        compiler_params=_cparams("arbitrary", "arbitrary"),
        name="gdn",
    )(q, k, v, gbt, q, k, v, gbt)


def _band_kernel(q_ref, kp_ref, kc_ref, kn_ref, vp_ref, vc_ref, vn_ref, bias_ref, o_ref, lse_ref, *, tq, m_len):
    t = pl.program_id(1)
    QT = BAND_QT
    nkeys = QT + 2 * D_STEPS
    kwin = jnp.concatenate([kp_ref[...], kc_ref[...], kn_ref[...]], axis=0)
    vwin = jnp.concatenate([vp_ref[...], vc_ref[...], vn_ref[...]], axis=0)
    lane = lax.broadcasted_iota(jnp.int32, (QT, D_GW), 1)
    mine = [(lane // D_HEAD_DIM) == h for h in range(D_HEADS_PER_GROUP)]
    col = lax.broadcasted_iota(jnp.int32, (QT, nkeys), 1)
    tiles = []
    for i in range(tq // QT):
        q = q_ref[i * QT:(i + 1) * QT, :]
        kpos = t * tq + i * QT - D_STEPS + col
        valid = (kpos >= 0) & (kpos < m_len)
        kw = kwin[i * QT:i * QT + nkeys]
        for h in range(D_HEADS_PER_GROUP):
            qh = jnp.where(mine[h], q, jnp.zeros_like(q))
            s = lax.dot_general(qh, kw, (((1,), (1,)), ((), ())), preferred_element_type=F32) + bias_ref[h]
            tiles.append([i, h, jnp.where(valid, s, NEG_INF)])
    for tile in tiles:
        s = tile[2]
        m = jnp.max(s, axis=-1, keepdims=True)
        e = jnp.exp(s - m)
        l = jnp.sum(e, axis=-1, keepdims=True)
        tile[2] = (e * (1.0 / l)).astype(BF16)
        tile.append(m + jnp.log(l))
    for i in range(tq // QT):
        o = jnp.zeros((QT, D_GW), F32)
        lse_full = jnp.zeros((QT, D_GW), F32)
        vw = vwin[i * QT:i * QT + nkeys]
        for _, h, p, lse in tiles[i * D_HEADS_PER_GROUP:(i + 1) * D_HEADS_PER_GROUP]:
            o = jnp.where(mine[h], jnp.dot(p, vw, preferred_element_type=F32), o)
            lse_full = jnp.where(mine[h], lse, lse_full)
        o_ref[i * QT:(i + 1) * QT, :] = o
        lse_ref[i * QT:(i + 1) * QT, :] = lse_full


def _band(q, k, v, bias, Z, m_len):
    M = q.shape[0]
    tq = min(BAND_TQ, m_len)
    nq = m_len // tq
    r = tq // D_STEPS
    nhb = M // D_STEPS
    cur = lambda z, t: (z * nq + t, 0)
    prev = lambda z, t: (jnp.maximum((z * nq + t) * r - 1, 0), 0)
    nxt = lambda z, t: (jnp.minimum((z * nq + t + 1) * r, nhb - 1), 0)
    halo = lambda f: pl.BlockSpec((D_STEPS, D_GW), f)
    full = pl.BlockSpec((tq, D_GW), cur)
    return pl.pallas_call(
        functools.partial(_band_kernel, tq=tq, m_len=m_len),
        grid=(Z, nq),
        in_specs=[full, halo(prev), full, halo(nxt), halo(prev), full, halo(nxt),
                  pl.BlockSpec((D_HEADS_PER_GROUP, BAND_QT, BAND_QT + 2 * D_STEPS), lambda z, t: (0, 0, 0))],
        out_specs=[full, full],
        out_shape=[jax.ShapeDtypeStruct((M, D_GW), F32)] * 2,
        compiler_params=_cparams("parallel", "parallel"),
        name="band_attn",
    )(q, k, k, k, v, v, v, bias)


def _odd_out_kernel(x_ref, of_ref, or_ref, z_ref, cg_ref, o0_ref, o1_ref, o2_ref, l0_ref, l1_ref, l2_ref,
                    unperm_ref, wc_ref, wd_ref, *rest, final):
    *ff, out_ref = rest
    tm = x_ref.shape[0]
    oc = of_ref[...] + or_ref[...]
    z = z_ref[...]
    parts = []
    for h in range(C_HEADS):
        sl = slice(h * LANES, (h + 1) * LANES)
        parts.append(_rms(oc[:, sl], cg_ref[...]) * _silu(z[:, sl]))
    yc = jnp.concatenate(parts, axis=1).astype(BF16)

    def token_order(ref, gi):
        if D_GROUPS[gi][1] == 1:
            return ref[...]
        val = ref[...].reshape(tm, D_GW)
        hi = val.astype(BF16)
        lo = (val - hi.astype(F32)).astype(BF16)
        pt = unperm_ref[gi - 1]
        return jnp.dot(pt, hi, preferred_element_type=F32) + jnp.dot(pt, lo, preferred_element_type=F32)

    o0, o1, o2 = (token_order(r, gi) for gi, r in enumerate((o0_ref, o1_ref, o2_ref)))
    l0, l1, l2 = (token_order(r, gi) for gi, r in enumerate((l0_ref, l1_ref, l2_ref)))
    m = jnp.maximum(jnp.maximum(l0, l1), l2)
    e0, e1, e2 = jnp.exp(l0 - m), jnp.exp(l1 - m), jnp.exp(l2 - m)
    inv = 1.0 / (e0 + e1 + e2)
    yd = ((e0 * o0 + e1 * o1 + e2 * o2) * inv).astype(BF16)
    y = jnp.dot(yc, wc_ref[...], preferred_element_type=F32) + jnp.dot(yd, wd_ref[...], preferred_element_type=F32)
    out_ref[...] = _mlp_tail(x_ref[...] + y, ff, final)


def _odd_out(x, o_f, o_r, z, cg, os_, ls_, unperm, wc, wd, ff, final, B, S):
    tm = PERM_TM
    M = x.shape[0]
    row = lambda i: (i, 0)
    wide = pl.BlockSpec((tm, C_WIDTH), row)
    grp = [pl.BlockSpec((tm, D_GW), row) if dil == 1 else _class_major_spec(dil, S, tm) for _, dil in D_GROUPS]
    return pl.pallas_call(
        functools.partial(_odd_out_kernel, final=final),
        grid=(M // tm,),
        in_specs=[pl.BlockSpec((tm, D_MODEL), row), wide, wide, wide, _resident((1, LANES))] + grp + grp
        + [_resident(unperm.shape), _resident((C_WIDTH, D_MODEL)), _resident((D_GW, D_MODEL))] + _ff_specs(),
        out_specs=pl.BlockSpec((tm, D_MODEL), row),
        out_shape=jax.ShapeDtypeStruct((M, D_MODEL), F32),
        compiler_params=_cparams("parallel"),
        name="odd_out_mlp",
    )(x, o_f, o_r, z, cg, *os_, *ls_, unperm, wc, wd, *ff)


def _class_perms():
    mats = []
    for _, dil in D_GROUPS[1:]:
        out_row = np.arange(PERM_TM)
        src = (out_row % (PERM_TM // dil)) * dil + out_row // (PERM_TM // dil)
        mats.append(np.eye(PERM_TM, dtype=np.float32)[src])
    perm = np.stack(mats)
    return jnp.asarray(perm, BF16), jnp.asarray(perm.transpose(0, 2, 1), BF16)


def _t5_bucket(rel):
    nb = N_BUCKETS // 2
    max_exact = nb // 2
    n = np.abs(rel)
    large = max_exact + (np.log(np.maximum(n, 1) / max_exact) / math.log(MAX_DISTANCE / max_exact)
                         * (nb - max_exact)).astype(np.int64)
    large = np.minimum(large, nb - 1)
    return (np.where(rel > 0, nb, 0) + np.where(n < max_exact, n, large)).astype(np.int32)


def _band_bias(rel_bias):
    nkeys = BAND_QT + 2 * D_STEPS
    tiles = []
    for gi, (window, dil) in enumerate(D_GROUPS):
        steps = window // (2 * dil)
        assert steps == D_STEPS
        buckets = _t5_bucket(np.arange(-steps, steps + 1) * dil)
        hsl = slice(gi * D_HEADS_PER_GROUP, (gi + 1) * D_HEADS_PER_GROUP)
        b = rel_bias.astype(F32)[jnp.asarray(buckets)][:, hsl].T
        period = jnp.concatenate([b, jnp.full((D_HEADS_PER_GROUP, nkeys + 1 - b.shape[1]), NEG_INF, F32)], axis=1)
        flat = jnp.tile(period, (1, BAND_QT))[:, :BAND_QT * nkeys]
        tiles.append(flat.reshape(D_HEADS_PER_GROUP, BAND_QT, nkeys))
    return jnp.stack(tiles)


def _prep_odd(w_in, w_out, conv_w, a_log, dt_bias, o_gain, rel_bias):
    n_c = 4 * C_WIDTH
    n_ba = 4 * C_HEADS
    w = jnp.concatenate([w_in[:, :n_c], w_in[:, n_c + n_ba:], w_in[:, n_c:n_c + n_ba],
                         jnp.zeros((D_MODEL, LANES - n_ba), w_in.dtype)], axis=1).astype(BF16)
    pad = jnp.zeros((LANES - n_ba,), F32)
    nega = jnp.concatenate([jnp.zeros((2 * C_HEADS,), F32), -jnp.exp(a_log.astype(F32)).reshape(-1), pad])[None]
    dtb = jnp.concatenate([jnp.zeros((2 * C_HEADS,), F32), dt_bias.astype(F32).reshape(-1), pad])[None]
    return dict(w=w, nega=nega, dtb=dtb, cw=conv_w.astype(F32), cg=o_gain.astype(F32)[None],
                bias=_band_bias(rel_bias), wc=w_out[:C_WIDTH].astype(BF16), wd=w_out[C_WIDTH:].astype(BF16))


def _odd_layer(x, gain, p, ff, final, B, S):
    perm, unperm = _class_perms()
    (cq, ck, cv, z), dq, dk, dv, gb = _odd_proj(x, gain, p['w'], p['cw'], p['nega'], p['dtb'], perm, B, S)
    o_f, o_r = _gdn(cq, ck, cv, gb, B, S)
    outs, lses = [], []
    for gi, (_, dil) in enumerate(D_GROUPS):
        flat = lambda a: a.reshape(B * S, D_GW)
        o, lse = _band(flat(dq[gi]), flat(dk[gi]), flat(dv[gi]), p['bias'][gi], B * dil, S // dil)
        shape = (B * S, D_GW) if dil == 1 else (B, dil, S // dil, D_GW)
        outs.append(o.reshape(shape))
        lses.append(lse.reshape(shape))
    return _odd_out(x, o_f, o_r, z, p['cg'], outs, lses, unperm, p['wc'], p['wd'], ff, final, B, S)


def _trunk(x3, evens, odds, norm_mix, norm_ff, norm_final, w1, w2):
    B, S, width = x3.shape
    assert width == D_MODEL and S % (max(d for _, d in D_GROUPS) * BAND_QT) == 0 and S % PERM_TM == 0
    x = x3.reshape(B * S, D_MODEL)
    cos, sin = _rope_tables(S)
    for layer in range(DEPTH):
        gain = norm_mix[layer][None]
        ff = (norm_ff[layer][None], w1[layer], w2[layer], norm_final[None])
        final = layer == DEPTH - 1
        if layer % 2 == 0:
            x = _even_layer(x, gain, evens[layer // 2], ff, final, cos, sin, B, S)
        else:
            x = _odd_layer(x, gain, odds[layer // 2], ff, final, B, S)
    return x.reshape(B, S, D_MODEL)


def kernel(x_prompt, x_sample, rel_bias, norm_mix, norm_ff, norm_final, w_ff1, w_ff2, w_in_e, w_out_e,
           a_qnorm, a_knorm, b_conv_w, b_conv_b, b_wr, b_br, b_wi, b_bi, b_lambda, w_in_o, w_out_o,
           c_conv_w, c_a_log, c_dt_bias, c_norm):
    evens = [_prep_even(w_in_e[j], w_out_e[j], a_qnorm[j], a_knorm[j], b_conv_w[j], b_conv_b[j],
                        b_wr[j], b_br[j], b_wi[j], b_bi[j], b_lambda[j]) for j in range((DEPTH + 1) // 2)]
    odds = [_prep_odd(w_in_o[j], w_out_o[j], c_conv_w[j], c_a_log[j], c_dt_bias[j], c_norm[j], rel_bias)
            for j in range(DEPTH // 2)]
    w1 = w_ff1.astype(BF16)
    w2 = w_ff2.astype(BF16)
    nm = norm_mix.astype(F32)
    nf = norm_ff.astype(F32)
    ng = norm_final.astype(F32)
    y_prompt = _trunk(x_prompt, evens, odds, nm, nf, ng, w1, w2)
    y_sample = _trunk(x_sample, evens, odds, nm, nf, ng, w1, w2)
    return (y_prompt, y_sample)
```

```python
import functools
import math

import numpy as np
import jax
import jax.numpy as jnp
from jax import lax
from jax.experimental import pallas as pl
from jax.experimental.pallas import tpu as pltpu

F32 = jnp.float32
BF16 = jnp.bfloat16

D_MODEL = 1024
D_FF = 4 * D_MODEL
DEPTH = 4
EPS = 1e-6
NEG_INF = -1e30
GRID_W = 64
LANES = 128

A_HEADS = 8
A_KV_HEADS = 2
A_HEAD_DIM = 64
A_Q = A_HEADS * A_HEAD_DIM
A_KV = A_KV_HEADS * A_HEAD_DIM
ROPE_THETA = 10000.0
B_WIDTH = 512
B_BLOCKS = 8
RG_C = 8.0
C_HEADS = 4
C_HEAD_DIM = 128
C_WIDTH = C_HEADS * C_HEAD_DIM
C_CHUNK = 64
D_GROUPS = ((128, 1), (512, 4), (2048, 16))
D_HEADS_PER_GROUP = 4
D_HEAD_DIM = 64
D_NHEADS = len(D_GROUPS) * D_HEADS_PER_GROUP
D_WIDTH = D_NHEADS * D_HEAD_DIM
D_GW = D_HEADS_PER_GROUP * D_HEAD_DIM
D_STEPS = 64
N_BUCKETS = 32
MAX_DISTANCE = 1024
HALO = 8
Q_SCALE = A_HEAD_DIM ** -0.5 * math.log2(math.e)
MLP_TF = 1024
PROJ_ROWS = 256
PERM_TM = 512
BAND_QT = 128
BAND_TQ = 1024
GQA_TILE = 256
VT_PAD = 16
VT_ROWS = A_KV_HEADS * (A_HEAD_DIM + VT_PAD)

VMEM_LIMIT = 56 * 1024 * 1024


def _cparams(*sem):
    return pltpu.CompilerParams(dimension_semantics=sem, vmem_limit_bytes=VMEM_LIMIT)


def _rms(x, gain):
    return x * lax.rsqrt(jnp.mean(x * x, axis=-1, keepdims=True) + EPS) * gain


def _softplus(x):
    return jnp.maximum(x, 0.0) + jnp.log1p(jnp.exp(-jnp.abs(x)))


def _sigmoid(x):
    return 0.5 * jnp.tanh(0.5 * x) + 0.5


def _silu(x):
    return x * _sigmoid(x)


def _gelu_tanh(x):
    return 0.5 * x * (1.0 + jnp.tanh(math.sqrt(2.0 / math.pi) * (x + 0.044715 * (x * x * x))))


def _conv4(prev, cur, nxt, w, first, last):
    rows = cur.shape[0]
    prev = jnp.where(first, 0.0, prev)
    nxt = jnp.where(last, 0.0, nxt)
    full = jnp.concatenate([prev, cur, nxt], axis=0)
    y = full[HALO - 2:HALO - 2 + rows] * w[0:1]
    for j in range(1, 4):
        y = y + full[HALO - 2 + j:HALO - 2 + j + rows] * w[j:j + 1]
    return y


def _even_proj_kernel(xp_ref, x_ref, xn_ref, g_ref, w_ref, wvt_ref, wxr_ref, cw_ref, cb_ref, qg_ref, kg_ref,
                      cos_ref, sin_ref, q_ref, k_ref, vt_ref, xc_ref, gr_ref, *, n_s):
    tm = x_ref.shape[0]
    i = pl.program_id(0)
    gain = g_ref[...]
    parts = []
    normed = []
    for r0 in range(0, tm, PROJ_ROWS):
        xn = _rms(x_ref[r0:r0 + PROJ_ROWS, :], gain).astype(BF16)
        proj = jnp.dot(xn, w_ref[...], preferred_element_type=F32)
        vt = lax.dot_general(wvt_ref[...], xn, (((1,), (1,)), ((), ())), preferred_element_type=F32)
        parts.append((r0, proj, vt))
        normed.append(xn)
    ext = jnp.concatenate([_rms(xp_ref[...], gain).astype(BF16)] + normed
                          + [_rms(xn_ref[...], gain).astype(BF16)], axis=0)
    pxr = jnp.dot(ext, wxr_ref[...], preferred_element_type=F32)
    xc_ref[...] = _conv4(pxr[:HALO], pxr[HALO:HALO + tm], pxr[HALO + tm:], cw_ref[...],
                         i % n_s == 0, i % n_s == n_s - 1) + cb_ref[...]
    lane = lax.broadcasted_iota(jnp.int32, (PROJ_ROWS, LANES), 1)
    first_half = (lane % 32) < 16
    lo = lane < A_HEAD_DIM
    ones = jnp.ones((VT_PAD, LANES), BF16)
    for r0, proj, vt in parts:
        rows = slice(r0, r0 + PROJ_ROWS)
        cos = cos_ref[rows, :]
        sin = sin_ref[rows, :]

        def rope(t):
            swapped = jnp.where(first_half, pltpu.roll(t, LANES - 16, 1), pltpu.roll(t, 16, 1))
            return t * cos + swapped * sin

        vt = vt.astype(BF16)
        for c in range(PROJ_ROWS // LANES):
            cols = slice(c * LANES, (c + 1) * LANES)
            vt_ref[r0 // LANES + c] = jnp.concatenate(
                [vt[:A_HEAD_DIM, cols], ones, vt[A_HEAD_DIM:, cols], ones], axis=0)
        for h in range(A_HEADS):
            t = proj[:, h * LANES:(h + 1) * LANES]
            r = lax.rsqrt(jnp.sum(t * t, axis=-1, keepdims=True) * (1.0 / A_HEAD_DIM) + EPS)
            t = rope(t * r * qg_ref[:, h * LANES:(h + 1) * LANES])
            q_ref[rows, h * LANES:(h + 1) * LANES] = (t * Q_SCALE).astype(BF16)
        off = A_HEADS * LANES
        t = proj[:, off:off + LANES]
        t2 = t * t
        s_lo = jnp.sum(jnp.where(lo, t2, 0.0), axis=-1, keepdims=True)
        s_hi = jnp.sum(jnp.where(lo, 0.0, t2), axis=-1, keepdims=True)
        r = lax.rsqrt(jnp.where(lo, s_lo, s_hi) * (1.0 / A_HEAD_DIM) + EPS)
        k_ref[rows, :] = rope(t * r * kg_ref[...]).astype(BF16)
        off += LANES
        gr_ref[rows, :] = proj[:, off:off + B_WIDTH]


def _even_proj(x, gain, w, wvt, wxr, cw, cb, qg, kg, cos, sin, S, tm=512):
    M = x.shape[0]
    nS = S // tm
    n_in = w.shape[1]
    row = lambda i: (i, 0)
    const = lambda i: (0, 0)
    hb = tm // HALO
    last_hb = M // HALO - 1
    return pl.pallas_call(
        functools.partial(_even_proj_kernel, n_s=nS),
        grid=(M // tm,),
        in_specs=[pl.BlockSpec((HALO, D_MODEL), lambda i: (jnp.maximum(i * hb - 1, 0), 0)),
                  pl.BlockSpec((tm, D_MODEL), row),
                  pl.BlockSpec((HALO, D_MODEL), lambda i: (jnp.minimum((i + 1) * hb, last_hb), 0)),
                  pl.BlockSpec((1, D_MODEL), const),
                  pl.BlockSpec((D_MODEL, n_in), const), pl.BlockSpec((A_KV, D_MODEL), const),
                  pl.BlockSpec((D_MODEL, B_WIDTH), const), pl.BlockSpec((4, B_WIDTH), const),
                  pl.BlockSpec((1, B_WIDTH), const),
                  pl.BlockSpec((1, A_HEADS * LANES), const), pl.BlockSpec((1, LANES), const),
                  pl.BlockSpec((tm, LANES), lambda i: (i % nS, 0)),
                  pl.BlockSpec((tm, LANES), lambda i: (i % nS, 0))],
        out_specs=[pl.BlockSpec((tm, A_HEADS * LANES), row), pl.BlockSpec((tm, LANES), row),
                   pl.BlockSpec((tm // LANES, VT_ROWS, LANES), lambda i: (i, 0, 0)),
                   pl.BlockSpec((tm, B_WIDTH), row), pl.BlockSpec((tm, B_WIDTH), row)],
        out_shape=[jax.ShapeDtypeStruct((M, A_HEADS * LANES), BF16),
                   jax.ShapeDtypeStruct((M, LANES), BF16),
                   jax.ShapeDtypeStruct((M // LANES, VT_ROWS, LANES), BF16),
                   jax.ShapeDtypeStruct((M, B_WIDTH), F32), jax.ShapeDtypeStruct((M, B_WIDTH), F32)],
        compiler_params=_cparams("parallel"),
        name="even_proj",
    )(x, x, x, gain, w, wvt, wxr, cw, cb, qg, kg, cos, sin)


def _gqa_kernel(q_ref, k_ref, vt_ref, o_ref, m_ref, acc_ref, *, tq, kstep, nstep, unroll, lookahead):
    hpt = GQA_TILE // tq
    ntile = A_HEADS // hpt
    qt = [jnp.concatenate([q_ref[:, h * LANES:(h + 1) * LANES] for h in range(p * hpt, (p + 1) * hpt)], axis=0)
          for p in range(ntile)]
    m_ref[...] = jnp.full(m_ref.shape, NEG_INF, F32)
    acc_ref[...] = jnp.zeros(acc_ref.shape, F32)
    vrows = VT_ROWS // A_KV_HEADS

    per = kstep // LANES

    def scores(j, u, p):
        ks = k_ref[pl.ds(pl.multiple_of((j * unroll + u) * kstep, kstep), kstep), :]
        return lax.dot_general(ks, qt[p], (((1,), (1,)), ((), ())), preferred_element_type=F32)

    def update(j, u, p, st):
        g = (p * hpt) // (A_HEADS // A_KV_HEADS)
        m_old = m_ref[p]
        m_new = jnp.maximum(m_old, jnp.max(st, axis=0, keepdims=True))
        alpha = jnp.exp2(m_old - m_new)
        pt = jnp.exp2(st - m_new).astype(BF16)
        first = (j * unroll + u) * per
        vts = jnp.concatenate([vt_ref[first + i, g * vrows:(g + 1) * vrows, :] for i in range(per)], axis=1)
        acc_ref[p] = alpha * acc_ref[p] + jnp.dot(vts, pt, preferred_element_type=F32)
        m_ref[p] = m_new

    pairs = [(u, p) for u in range(unroll) for p in range(ntile)]
    niter = nstep // unroll

    def body(j, pending):
        pending = list(pending)
        j_next = jnp.minimum(j + 1, niter - 1)
        for idx in range(len(pairs)):
            ahead = idx + lookahead
            if ahead < len(pairs):
                pending.append(scores(j, *pairs[ahead]))
            else:
                pending.append(scores(j_next, *pairs[ahead - len(pairs)]))
            update(j, *pairs[idx], pending.pop(0))
        return tuple(pending)

    lax.fori_loop(0, niter, body, tuple(scores(0, *pairs[i]) for i in range(lookahead)))
    heads = []
    for p in range(ntile):
        acc = acc_ref[p]
        o = acc[:A_HEAD_DIM] / acc[A_HEAD_DIM:A_HEAD_DIM + 1]
        heads += [o[:, i * tq:(i + 1) * tq] for i in range(hpt)]
    for e in range(A_HEADS // 2):
        both = jnp.concatenate(heads[2 * e:2 * e + 2], axis=0)
        o_ref[:, e * LANES:(e + 1) * LANES] = jnp.transpose(both).astype(o_ref.dtype)


def _gqa(q, k, vt, B, S, kstep=256, lookahead=5):
    tq, unroll = (GQA_TILE, 8) if S <= 2048 else (GQA_TILE // 2, 32)
    M = q.shape[0]
    nq = S // tq
    nslab = S // LANES
    nstep = S // kstep
    unroll = min(unroll, nstep)
    kern = functools.partial(_gqa_kernel, tq=tq, kstep=kstep, nstep=nstep, unroll=unroll, lookahead=lookahead)
    return pl.pallas_call(
        kern,
        grid=(B, nq),
        in_specs=[pl.BlockSpec((tq, A_HEADS * LANES), lambda b, i: (b * nq + i, 0)),
                  pl.BlockSpec((S, LANES), lambda b, i: (b, 0)),
                  pl.BlockSpec((nslab, VT_ROWS, LANES), lambda b, i: (b, 0, 0))],
        out_specs=pl.BlockSpec((tq, A_Q), lambda b, i: (b * nq + i, 0)),
        out_shape=jax.ShapeDtypeStruct((M, A_Q), BF16),
        scratch_shapes=[pltpu.VMEM((A_HEADS * tq // GQA_TILE, 1, GQA_TILE), F32),
                        pltpu.VMEM((A_HEADS * tq // GQA_TILE, VT_ROWS // A_KV_HEADS, GQA_TILE), F32)],
        compiler_params=_cparams("parallel", "parallel"),
        name="gqa",
    )(q, k, vt)


def _rglru_kernel(xf_ref, xr_ref, wg_ref, bg_ref, lam_ref, hf_ref, hr_ref, a_s, u_s, carry_s, *, tm):
    t = pl.program_id(1)

    @pl.when(t == 0)
    def _():
        carry_s[...] = jnp.zeros_like(carry_s)

    ngrp = tm // 8
    row8 = lax.broadcasted_iota(jnp.int32, (ngrp, 8, B_WIDTH), 1)
    for d, c_ref in enumerate((xf_ref, xr_ref)):
        xc = c_ref[...]
        xb = xc.astype(BF16)
        r = _sigmoid(jnp.dot(xb, wg_ref[2 * d], preferred_element_type=F32) + bg_ref[2 * d:2 * d + 1])
        i = _sigmoid(jnp.dot(xb, wg_ref[2 * d + 1], preferred_element_type=F32) + bg_ref[2 * d + 1:2 * d + 2])
        log_a = (-RG_C) * r * _softplus(-lam_ref[d:d + 1])
        a = jnp.exp(log_a).reshape(ngrp, 8, B_WIDTH)
        u = (jnp.sqrt(1.0 - jnp.exp(2.0 * log_a)) * (i * xc)).reshape(ngrp, 8, B_WIDTH)
        for sh in (1, 2, 4):
            keep = (row8 >= sh) if d == 0 else (row8 < 8 - sh)
            shift = sh if d == 0 else 8 - sh
            u = u + a * jnp.where(keep, pltpu.roll(u, shift, 1), 0.0)
            a = a * jnp.where(keep, pltpu.roll(a, shift, 1), 1.0)
        a_s[d] = a.reshape(tm, B_WIDTH)
        u_s[d] = u.reshape(tm, B_WIDTH)

    def body(gidx, carry):
        out = []
        for d, out_ref in enumerate((hf_ref, hr_ref)):
            gg = gidx if d == 0 else ngrp - 1 - gidx
            start = pl.multiple_of(gg * 8, 8)
            hg = u_s[d, pl.ds(start, 8), :] + a_s[d, pl.ds(start, 8), :] * carry[d]
            out_ref[pl.ds(start, 8), :] = hg
            edge = hg[7:8, :] if d == 0 else hg[0:1, :]
            out.append(jnp.broadcast_to(edge, (8, B_WIDTH)))
        return tuple(out)

    ends = lax.fori_loop(0, ngrp, body, (carry_s[0], carry_s[1]), unroll=4)
    carry_s[0] = ends[0]
    carry_s[1] = ends[1]


def _rglru(xc, wg, bg, lam, B, S, tm=512):
    M = xc.shape[0]
    nblk = S // tm
    fwd = lambda b, t: b * nblk + t
    rev = lambda b, t: b * nblk + (nblk - 1 - t)
    const2 = lambda b, t: (0, 0)
    const3 = lambda b, t: (0, 0, 0)
    return pl.pallas_call(
        functools.partial(_rglru_kernel, tm=tm),
        grid=(B, nblk),
        in_specs=[pl.BlockSpec((tm, B_WIDTH), lambda b, t: (fwd(b, t), 0)),
                  pl.BlockSpec((tm, B_WIDTH), lambda b, t: (rev(b, t), 0)),
                  pl.BlockSpec((4, B_WIDTH, B_WIDTH), const3), pl.BlockSpec((4, B_WIDTH), const2),
                  pl.BlockSpec((2, B_WIDTH), const2)],
        out_specs=[pl.BlockSpec((tm, B_WIDTH), lambda b, t: (fwd(b, t), 0)),
                   pl.BlockSpec((tm, B_WIDTH), lambda b, t: (rev(b, t), 0))],
        out_shape=[jax.ShapeDtypeStruct((M, B_WIDTH), F32), jax.ShapeDtypeStruct((M, B_WIDTH), F32)],
        scratch_shapes=[pltpu.VMEM((2, tm, B_WIDTH), F32), pltpu.VMEM((2, tm, B_WIDTH), F32),
                        pltpu.VMEM((2, 8, B_WIDTH), F32)],
        compiler_params=_cparams("arbitrary", "arbitrary"),
        name="rglru",
    )(xc, xc, wg, bg, lam)


def _mlp_tail(x, ff, final):
    gff_ref, w1_ref, w2_ref, gfin_ref = ff
    xn = _rms(x, gff_ref[...]).astype(BF16)
    acc = x
    for c in range(D_FF // MLP_TF):
        h = jnp.dot(xn, w1_ref[:, c * MLP_TF:(c + 1) * MLP_TF], preferred_element_type=F32)
        h = jnp.square(jnp.maximum(h, 0.0)).astype(BF16)
        acc = acc + jnp.dot(h, w2_ref[c * MLP_TF:(c + 1) * MLP_TF, :], preferred_element_type=F32)
    return _rms(acc, gfin_ref[...]) if final else acc


def _resident(shape):
    return pl.BlockSpec(shape, lambda *_: (0,) * len(shape), pipeline_mode=pl.Buffered(1))


def _ff_specs():
    return [_resident((1, D_MODEL)), _resident((D_MODEL, D_FF)), _resident((D_FF, D_MODEL)),
            _resident((1, D_MODEL))]


def _even_out_kernel(x_ref, ya_ref, hf_ref, hr_ref, gr_ref, wa_ref, wb_ref, *rest, final):
    *ff, o_ref = rest
    yb = ((hf_ref[...] + hr_ref[...]) * _gelu_tanh(gr_ref[...])).astype(BF16)
    y = jnp.dot(ya_ref[...], wa_ref[...], preferred_element_type=F32)
    y = y + jnp.dot(yb, wb_ref[...], preferred_element_type=F32)
    o_ref[...] = _mlp_tail(x_ref[...] + y, ff, final)


def _even_out(x, ya, hf, hr, gr, wa, wb, ff, final, tm=512):
    M = x.shape[0]
    row = lambda i: (i, 0)
    return pl.pallas_call(
        functools.partial(_even_out_kernel, final=final),
        grid=(M // tm,),
        in_specs=[pl.BlockSpec((tm, D_MODEL), row), pl.BlockSpec((tm, A_Q), row),
                  pl.BlockSpec((tm, B_WIDTH), row), pl.BlockSpec((tm, B_WIDTH), row),
                  pl.BlockSpec((tm, B_WIDTH), row),
                  _resident((A_Q, D_MODEL)), _resident((B_WIDTH, D_MODEL))] + _ff_specs(),
        out_specs=pl.BlockSpec((tm, D_MODEL), row),
        out_shape=jax.ShapeDtypeStruct((M, D_MODEL), F32),
        compiler_params=_cparams("parallel"),
        name="even_out_mlp",
    )(x, ya, hf, hr, gr, wa, wb, *ff)


def _rope_tables(S):
    rows = S // GRID_W
    row = jnp.repeat(jnp.arange(rows, dtype=F32), GRID_W)
    col = jnp.tile(jnp.arange(GRID_W, dtype=F32), rows)
    n_freq = A_HEAD_DIM // 4
    inv = ROPE_THETA ** (-jnp.arange(n_freq, dtype=F32) / n_freq)
    ang_r = row[:, None] * inv
    ang_c = col[:, None] * inv
    cos = jnp.concatenate([jnp.cos(ang_r)] * 2 + [jnp.cos(ang_c)] * 2, axis=1)
    sin = jnp.concatenate([-jnp.sin(ang_r), jnp.sin(ang_r), -jnp.sin(ang_c), jnp.sin(ang_c)], axis=1)
    return jnp.tile(cos, (1, 2)), jnp.tile(sin, (1, 2))


def _prep_even(w_in, w_out, qn, kn, conv_w, conv_b, wr, br, wi, bi, lam):
    G = A_HEADS // A_KV_HEADS
    wq = w_in[:, :A_Q].reshape(D_MODEL, A_HEADS, A_HEAD_DIM)
    zeros = jnp.zeros_like(wq)
    half = (jnp.arange(A_HEADS) // G)[None, :, None]
    wq_pad = jnp.concatenate([jnp.where(half == 0, wq, zeros), jnp.where(half == 1, wq, zeros)], axis=-1)
    off = A_Q + 2 * A_KV
    w = jnp.concatenate([wq_pad.reshape(D_MODEL, A_HEADS * LANES), w_in[:, A_Q:A_Q + A_KV],
                         w_in[:, off + B_WIDTH:]], axis=1).astype(BF16)
    wxr = w_in[:, off:off + B_WIDTH].astype(BF16)
    wvt = w_in[:, A_Q + A_KV:A_Q + 2 * A_KV].T.astype(BF16)
    qg =jnp.tile(qn.astype(F32), 2 * A_HEADS)[None]
    kg = jnp.tile(kn.astype(F32), 2)[None]

    def dense(blocks):
        eye = jnp.eye(B_BLOCKS, dtype=blocks.dtype)
        return jnp.einsum('nde,nm->ndme', blocks, eye).reshape(B_WIDTH, B_WIDTH)

    wg = jnp.stack([dense(wr[0]), dense(wi[0]), dense(wr[1]), dense(wi[1])]).astype(BF16)
    bg = jnp.stack([br[0], bi[0], br[1], bi[1]]).astype(F32)
    return dict(w=w, wvt=wvt, wxr=wxr, qg=qg, kg=kg, cw=conv_w.astype(F32), cb=conv_b.astype(F32)[None], wg=wg, bg=bg,
                lam=lam.astype(F32), wa=w_out[:A_Q].astype(BF16), wb=w_out[A_Q:].astype(BF16))


def _even_layer(x, gain, p, ff, final, cos, sin, B, S):
    q, k, vt, xc, gr = _even_proj(x, gain, p['w'], p['wvt'], p['wxr'], p['cw'], p['cb'], p['qg'], p['kg'],
                                  cos, sin, S)
    ya = _gqa(q, k, vt, B, S)
    hf, hr = _rglru(xc, p['wg'], p['bg'], p['lam'], B, S)
    return _even_out(x, ya, hf, hr, gr, p['wa'], p['wb'], ff, final)


def _odd_proj_kernel(xp_ref, x_ref, xn_ref, g_ref, w_ref, cw_ref, nega_ref, dtb_ref, perm_ref,
                     cq_ref, ck_ref, cv_ref, z_ref, *rest, n_s):
    *d_refs, gb_ref = rest
    tm = x_ref.shape[0]
    i = pl.program_id(0)
    gain = g_ref[...]
    xn = _rms(x_ref[...], gain).astype(BF16)
    ext = jnp.concatenate([_rms(xp_ref[...], gain).astype(BF16), xn, _rms(xn_ref[...], gain).astype(BF16)], axis=0)
    n_c = 3 * C_WIDTH
    pq = jnp.dot(ext, w_ref[:, :n_c], preferred_element_type=F32)
    proj = jnp.dot(xn, w_ref[:, n_c:], preferred_element_type=F32)
    y = _silu(_conv4(pq[:HALO], pq[HALO:HALO + tm], pq[HALO + tm:], cw_ref[...], i % n_s == 0, i % n_s == n_s - 1))
    for h in range(C_HEADS):
        for part, ref, scale in ((0, cq_ref, C_HEAD_DIM ** -0.5), (1, ck_ref, 1.0)):
            a = y[:, part * C_WIDTH + h * LANES:part * C_WIDTH + (h + 1) * LANES]
            a = a * lax.rsqrt(jnp.sum(a * a, axis=-1, keepdims=True) + EPS)
            ref[:, h * LANES:(h + 1) * LANES] = a * scale
    cv_ref[...] = y[:, 2 * C_WIDTH:]
    z_ref[...] = proj[:, :C_WIDTH]
    off = C_WIDTH
    for ti, scale in enumerate((D_HEAD_DIM ** -0.5, 1.0, 1.0)):
        for gi, (_, dil) in enumerate(D_GROUPS):
            xg = (proj[:, off + gi * D_GW:off + (gi + 1) * D_GW] * scale).astype(BF16)
            ref = d_refs[ti * len(D_GROUPS) + gi]
            if dil == 1:
                ref[...] = xg
            else:
                y = jnp.dot(perm_ref[gi - 1], xg, preferred_element_type=F32)
                ref[...] = y.reshape(dil, tm // dil, D_GW).astype(BF16)
        off += D_WIDTH
    t = proj[:, off:off + LANES]
    lane = lax.broadcasted_iota(jnp.int32, t.shape, 1)
    gate = jnp.where(lane < 2 * C_HEADS, _sigmoid(t), nega_ref[...] * _softplus(t + dtb_ref[...]))
    gb_ref[...] = jnp.transpose(gate)[:4 * C_HEADS]


def _class_major_spec(dil, S, tm):
    n_s = S // tm
    return pl.BlockSpec((None, dil, tm // dil, D_GW), lambda i: (i // n_s, 0, i % n_s, 0))


def _odd_proj(x, gain, w, cw, nega, dtb, perm, B, S):
    tm = PERM_TM
    M = x.shape[0]
    row = lambda i: (i, 0)
    const = lambda i: (0, 0)
    hb = tm // HALO
    last_hb = M // HALO - 1
    d_specs, d_shapes = [], []
    for _ in range(3):
        for _, dil in D_GROUPS:
            if dil == 1:
                d_specs.append(pl.BlockSpec((tm, D_GW), row))
                d_shapes.append(jax.ShapeDtypeStruct((M, D_GW), BF16))
            else:
                d_specs.append(_class_major_spec(dil, S, tm))
                d_shapes.append(jax.ShapeDtypeStruct((B, dil, S // dil, D_GW), BF16))
    wide = pl.BlockSpec((tm, C_WIDTH), row)
    outs = pl.pallas_call(
        functools.partial(_odd_proj_kernel, n_s=S // tm),
        grid=(M // tm,),
        in_specs=[pl.BlockSpec((HALO, D_MODEL), lambda i: (jnp.maximum(i * hb - 1, 0), 0)),
                  pl.BlockSpec((tm, D_MODEL), row),
                  pl.BlockSpec((HALO, D_MODEL), lambda i: (jnp.minimum((i + 1) * hb, last_hb), 0)),
                  pl.BlockSpec((1, D_MODEL), const), pl.BlockSpec((D_MODEL, w.shape[1]), const),
                  pl.BlockSpec((4, 3 * C_WIDTH), const),
                  pl.BlockSpec((1, LANES), const), pl.BlockSpec((1, LANES), const),
                  pl.BlockSpec(perm.shape, lambda i: (0, 0, 0))],
        out_specs=[wide, wide, wide, wide] + d_specs + [pl.BlockSpec((4 * C_HEADS, tm), lambda i: (0, i))],
        out_shape=[jax.ShapeDtypeStruct((M, C_WIDTH), F32)] * 4 + d_shapes
        + [jax.ShapeDtypeStruct((4 * C_HEADS, M), F32)],
        compiler_params=_cparams("parallel"),
        name="odd_proj",
    )(x, x, x, gain, w, cw, nega, dtb, perm)
    ng = len(D_GROUPS)
    return outs[:4], outs[4:4 + ng], outs[4 + ng:4 + 2 * ng], outs[4 + 2 * ng:4 + 3 * ng], outs[-1]


def _nt_dot(a, b):
    return lax.dot_general(a, b, (((1,), (1,)), ((), ())), preferred_element_type=F32)


def _gdn_kernel(qf_ref, kf_ref, vf_ref, gf_ref, qr_ref, kr_ref, vr_ref, gr_ref, of_ref, or_ref, state, *, tm):
    C = C_CHUNK
    SB = LANES
    nsb = tm // SB
    nc = tm // C
    t = pl.program_id(1)

    @pl.when(t == 0)
    def _():
        state[...] = jnp.zeros_like(state)

    ri = lax.broadcasted_iota(jnp.int32, (SB, SB), 0)
    ci = lax.broadcasted_iota(jnp.int32, (SB, SB), 1)
    same = (ri // C) == (ci // C)
    eye = jnp.where(ri == ci, 1.0, 0.0)
    incl, strict, levels = [], [], []
    for d in range(2):
        hi_i, lo_i = (ci, ri) if d else (ri, ci)
        incl.append(same & (hi_i >= lo_i))
        strict.append(same & (hi_i > lo_i))
        lv = []
        s = 1
        while s < C:
            lv.append((((hi_i // s) % 2) == 1) & ((lo_i // s) == (hi_i // s) - 1))
            s *= 2
        levels.append(lv)

    pos = lax.broadcasted_iota(jnp.int32, (4 * C_HEADS, tm), 1) % C
    rows, cols = [], []
    for d, g_ref in enumerate((gf_ref, gr_ref)):
        gt = g_ref[...]
        x = gt
        s = 1
        while s < C:
            if d == 0:
                x = x + jnp.where(pos >= s, pltpu.roll(x, s, 1), 0.0)
            else:
                x = x + jnp.where(pos < C - s, pltpu.roll(x, tm - s, 1), 0.0)
            s *= 2
        lo = d * C_HEADS
        rw = jnp.concatenate([gt[lo:lo + C_HEADS], x[2 * C_HEADS + lo:2 * C_HEADS + lo + C_HEADS]], axis=0)
        rows.append(rw)
        cols.append(jnp.transpose(jnp.concatenate([rw, jnp.zeros((LANES - 2 * C_HEADS, tm), F32)], axis=0)))

    refs = ((qf_ref, kf_ref, vf_ref, of_ref), (qr_ref, kr_ref, vr_ref, or_ref))
    chains = []
    for d in range(2):
        for h in range(C_HEADS):
            for sb in range(nsb):
                rs = slice(sb * SB, (sb + 1) * SB)
                sl = slice(h * LANES, (h + 1) * LANES)
                q = refs[d][0][rs, sl]
                k = refs[d][1][rs, sl]
                v = refs[d][2][rs, sl]
                beta = jnp.broadcast_to(cols[d][rs, h:h + 1], (SB, LANES))
                gcc = jnp.broadcast_to(cols[d][rs, C_HEADS + h:C_HEADS + h + 1], (SB, LANES))
                gcr = jnp.broadcast_to(rows[d][C_HEADS + h:C_HEADS + h + 1, rs], (SB, SB))
                dec = jnp.exp(jnp.where(incl[d], gcc - gcr, NEG_INF))
                eg = jnp.exp(gcc)
                kbf = k.astype(BF16)
                kb = k * beta
                lmat = jnp.where(strict[d], _nt_dot(kb.astype(BF16), kbf) * dec, 0.0)
                amat = jnp.where(incl[d], _nt_dot(q.astype(BF16), kbf) * dec, 0.0).astype(BF16)
                chains.append(dict(d=d, h=h, sb=sb, k=k, gcc=gcc, lmat=lmat, amat=amat,
                                   rhs=jnp.concatenate([v * beta, kb * eg], axis=1).astype(BF16),
                                   qe=(q * eg).astype(BF16)))

    for ch in chains:
        ch['x'] = eye - jnp.where(levels[ch['d']][0], ch['lmat'], 0.0)
    for li in range(1, len(levels[0])):
        for ch in chains:
            ch['xb'] = ch['x'].astype(BF16)
            cmat = jnp.where(levels[ch['d']][li], ch['lmat'], 0.0).astype(BF16)
            ch['cx'] = jnp.dot(cmat, ch['xb'], preferred_element_type=F32).astype(BF16)
        for ch in chains:
            ch['x'] = ch['x'] - jnp.dot(ch['xb'], ch['cx'], preferred_element_type=F32)
    for ch in chains:
        sol = jnp.dot(ch['x'].astype(BF16), ch['rhs'], preferred_element_type=F32)
        ch['u'] = sol[:, :LANES]
        ch['w'] = sol[:, LANES:].astype(BF16)
        ch['vn'] = [None] * (SB // C)
        ch['ost'] = [None] * (SB // C)

    by_key = {(ch['d'], ch['h'], ch['sb']): ch for ch in chains}
    st = {(d, h): state[d, h] for d in range(2) for h in range(C_HEADS)}
    for step in range(nc):
        work = []
        for (d, h), s_val in st.items():
            c = nc - 1 - step if d else step
            ch = by_key[(d, h, c // (SB // C))]
            lc = c % (SB // C)
            rs = slice(lc * C, (lc + 1) * C)
            stb = s_val.astype(BF16)
            vn = ch['u'][rs] - jnp.dot(ch['w'][rs], stb, preferred_element_type=F32)
            ch['ost'][lc] = jnp.dot(ch['qe'][rs], stb, preferred_element_type=F32)
            ch['vn'][lc] = vn
            last = lc * C if d else (lc + 1) * C - 1
            gl = ch['gcc'][last:last + 1, :]
            kd = (ch['k'][rs] * jnp.exp(gl - ch['gcc'][rs])).astype(BF16)
            work.append(((d, h), s_val * jnp.exp(gl), kd, vn.astype(BF16)))
        for key, decayed, kd, vnb in work:
            st[key] = decayed + lax.dot_general(kd, vnb, (((0,), (0,)), ((), ())), preferred_element_type=F32)
    for (d, h), s_val in st.items():
        state[d, h] = s_val
    for ch in chains:
        rs = slice(ch['sb'] * SB, (ch['sb'] + 1) * SB)
        sl = slice(ch['h'] * LANES, (ch['h'] + 1) * LANES)
        vn_all = jnp.concatenate(ch['vn'], axis=0).astype(BF16)
        refs[ch['d']][3][rs, sl] = (jnp.concatenate(ch['ost'], axis=0)
                                   + jnp.dot(ch['amat'], vn_all, preferred_element_type=F32))


def _gdn(q, k, v, gbt, B, S, tm=512):
    M = q.shape[0]
    nblk = S // tm
    fwd = lambda b, t: b * nblk + t
    rev = lambda b, t: b * nblk + (nblk - 1 - t)
    specs = []
    for blk in (fwd, rev):
        specs += [pl.BlockSpec((tm, C_WIDTH), lambda b, t, blk=blk: (blk(b, t), 0))] * 3
        specs += [pl.BlockSpec((4 * C_HEADS, tm), lambda b, t, blk=blk: (0, blk(b, t)))]
    return pl.pallas_call(
        functools.partial(_gdn_kernel, tm=tm),
        grid=(B, nblk),
        in_specs=specs,
        out_specs=[pl.BlockSpec((tm, C_WIDTH), lambda b, t: (fwd(b, t), 0)),
                   pl.BlockSpec((tm, C_WIDTH), lambda b, t: (rev(b, t), 0))],
        out_shape=[jax.ShapeDtypeStruct((M, C_WIDTH), F32)] * 2,
        scratch_shapes=[pltpu.VMEM((2, C_HEADS, C_HEAD_DIM, C_HEAD_DIM), F32)],
        compiler_params=_cparams("arbitrary", "arbitrary"),
        name="gdn",
    )(q, k, v, gbt, q, k, v, gbt)


def _band_kernel(q_ref, kp_ref, kc_ref, kn_ref, vp_ref, vc_ref, vn_ref, bias_ref, o_ref, lse_ref, *, tq, m_len):
    t = pl.program_id(1)
    QT = BAND_QT
    nkeys = QT + 2 * D_STEPS
    kwin = jnp.concatenate([kp_ref[...], kc_ref[...], kn_ref[...]], axis=0)
    vwin = jnp.concatenate([vp_ref[...], vc_ref[...], vn_ref[...]], axis=0)
    lane = lax.broadcasted_iota(jnp.int32, (QT, D_GW), 1)
    mine = [(lane // D_HEAD_DIM) == h for h in range(D_HEADS_PER_GROUP)]
    col = lax.broadcasted_iota(jnp.int32, (QT, nkeys), 1)
    tiles = []
    for i in range(tq // QT):
        q = q_ref[i * QT:(i + 1) * QT, :]
        kpos = t * tq + i * QT - D_STEPS + col
        valid = (kpos >= 0) & (kpos < m_len)
        kw = kwin[i * QT:i * QT + nkeys]
        for h in range(D_HEADS_PER_GROUP):
            qh = jnp.where(mine[h], q, jnp.zeros_like(q))
            s = lax.dot_general(qh, kw, (((1,), (1,)), ((), ())), preferred_element_type=F32) + bias_ref[h]
            tiles.append([i, h, jnp.where(valid, s, NEG_INF)])
    for tile in tiles:
        s = tile[2]
        m = jnp.max(s, axis=-1, keepdims=True)
        e = jnp.exp(s - m)
        l = jnp.sum(e, axis=-1, keepdims=True)
        tile[2] = (e * (1.0 / l)).astype(BF16)
        tile.append(m + jnp.log(l))
    for i in range(tq // QT):
        o = jnp.zeros((QT, D_GW), F32)
        lse_full = jnp.zeros((QT, D_GW), F32)
        vw = vwin[i * QT:i * QT + nkeys]
        for _, h, p, lse in tiles[i * D_HEADS_PER_GROUP:(i + 1) * D_HEADS_PER_GROUP]:
            o = jnp.where(mine[h], jnp.dot(p, vw, preferred_element_type=F32), o)
            lse_full = jnp.where(mine[h], lse, lse_full)
        o_ref[i * QT:(i + 1) * QT, :] = o
        lse_ref[i * QT:(i + 1) * QT, :] = lse_full


def _band(q, k, v, bias, Z, m_len):
    M = q.shape[0]
    tq = min(BAND_TQ, m_len)
    nq = m_len // tq
    r = tq // D_STEPS
    nhb = M // D_STEPS
    cur = lambda z, t: (z * nq + t, 0)
    prev = lambda z, t: (jnp.maximum((z * nq + t) * r - 1, 0), 0)
    nxt = lambda z, t: (jnp.minimum((z * nq + t + 1) * r, nhb - 1), 0)
    halo = lambda f: pl.BlockSpec((D_STEPS, D_GW), f)
    full = pl.BlockSpec((tq, D_GW), cur)
    return pl.pallas_call(
        functools.partial(_band_kernel, tq=tq, m_len=m_len),
        grid=(Z, nq),
        in_specs=[full, halo(prev), full, halo(nxt), halo(prev), full, halo(nxt),
                  pl.BlockSpec((D_HEADS_PER_GROUP, BAND_QT, BAND_QT + 2 * D_STEPS), lambda z, t: (0, 0, 0))],
        out_specs=[full, full],
        out_shape=[jax.ShapeDtypeStruct((M, D_GW), F32)] * 2,
        compiler_params=_cparams("parallel", "parallel"),
        name="band_attn",
    )(q, k, k, k, v, v, v, bias)


def _odd_out_kernel(x_ref, of_ref, or_ref, z_ref, cg_ref, o0_ref, o1_ref, o2_ref, l0_ref, l1_ref, l2_ref,
                    unperm_ref, wc_ref, wd_ref, *rest, final):
    *ff, out_ref = rest
    tm = x_ref.shape[0]
    oc = of_ref[...] + or_ref[...]
    z = z_ref[...]
    parts = []
    for h in range(C_HEADS):
        sl = slice(h * LANES, (h + 1) * LANES)
        parts.append(_rms(oc[:, sl], cg_ref[...]) * _silu(z[:, sl]))
    yc = jnp.concatenate(parts, axis=1).astype(BF16)

    def token_order(ref, gi):
        if D_GROUPS[gi][1] == 1:
            return ref[...]
        val = ref[...].reshape(tm, D_GW)
        hi = val.astype(BF16)
        lo = (val - hi.astype(F32)).astype(BF16)
        pt = unperm_ref[gi - 1]
        return jnp.dot(pt, hi, preferred_element_type=F32) + jnp.dot(pt, lo, preferred_element_type=F32)

    o0, o1, o2 = (token_order(r, gi) for gi, r in enumerate((o0_ref, o1_ref, o2_ref)))
    l0, l1, l2 = (token_order(r, gi) for gi, r in enumerate((l0_ref, l1_ref, l2_ref)))
    m = jnp.maximum(jnp.maximum(l0, l1), l2)
    e0, e1, e2 = jnp.exp(l0 - m), jnp.exp(l1 - m), jnp.exp(l2 - m)
    inv = 1.0 / (e0 + e1 + e2)
    yd = ((e0 * o0 + e1 * o1 + e2 * o2) * inv).astype(BF16)
    y = jnp.dot(yc, wc_ref[...], preferred_element_type=F32) + jnp.dot(yd, wd_ref[...], preferred_element_type=F32)
    out_ref[...] = _mlp_tail(x_ref[...] + y, ff, final)


def _odd_out(x, o_f, o_r, z, cg, os_, ls_, unperm, wc, wd, ff, final, B, S):
    tm = PERM_TM
    M = x.shape[0]
    row = lambda i: (i, 0)
    wide = pl.BlockSpec((tm, C_WIDTH), row)
    grp = [pl.BlockSpec((tm, D_GW), row) if dil == 1 else _class_major_spec(dil, S, tm) for _, dil in D_GROUPS]
    return pl.pallas_call(
        functools.partial(_odd_out_kernel, final=final),
        grid=(M // tm,),
        in_specs=[pl.BlockSpec((tm, D_MODEL), row), wide, wide, wide, _resident((1, LANES))] + grp + grp
        + [_resident(unperm.shape), _resident((C_WIDTH, D_MODEL)), _resident((D_GW, D_MODEL))] + _ff_specs(),
        out_specs=pl.BlockSpec((tm, D_MODEL), row),
        out_shape=jax.ShapeDtypeStruct((M, D_MODEL), F32),
        compiler_params=_cparams("parallel"),
        name="odd_out_mlp",
    )(x, o_f, o_r, z, cg, *os_, *ls_, unperm, wc, wd, *ff)


def _class_perms():
    mats = []
    for _, dil in D_GROUPS[1:]:
        out_row = np.arange(PERM_TM)
        src = (out_row % (PERM_TM // dil)) * dil + out_row // (PERM_TM // dil)
        mats.append(np.eye(PERM_TM, dtype=np.float32)[src])
    perm = np.stack(mats)
    return jnp.asarray(perm, BF16), jnp.asarray(perm.transpose(0, 2, 1), BF16)


def _t5_bucket(rel):
    nb = N_BUCKETS // 2
    max_exact = nb // 2
    n = np.abs(rel)
    large = max_exact + (np.log(np.maximum(n, 1) / max_exact) / math.log(MAX_DISTANCE / max_exact)
                         * (nb - max_exact)).astype(np.int64)
    large = np.minimum(large, nb - 1)
    return (np.where(rel > 0, nb, 0) + np.where(n < max_exact, n, large)).astype(np.int32)


def _band_bias(rel_bias):
    nkeys = BAND_QT + 2 * D_STEPS
    tiles = []
    for gi, (window, dil) in enumerate(D_GROUPS):
        steps = window // (2 * dil)
        assert steps == D_STEPS
        buckets = _t5_bucket(np.arange(-steps, steps + 1) * dil)
        hsl = slice(gi * D_HEADS_PER_GROUP, (gi + 1) * D_HEADS_PER_GROUP)
        b = rel_bias.astype(F32)[jnp.asarray(buckets)][:, hsl].T
        period = jnp.concatenate([b, jnp.full((D_HEADS_PER_GROUP, nkeys + 1 - b.shape[1]), NEG_INF, F32)], axis=1)
        flat = jnp.tile(period, (1, BAND_QT))[:, :BAND_QT * nkeys]
        tiles.append(flat.reshape(D_HEADS_PER_GROUP, BAND_QT, nkeys))
    return jnp.stack(tiles)


def _prep_odd(w_in, w_out, conv_w, a_log, dt_bias, o_gain, rel_bias):
    n_c = 4 * C_WIDTH
    n_ba = 4 * C_HEADS
    w = jnp.concatenate([w_in[:, :n_c], w_in[:, n_c + n_ba:], w_in[:, n_c:n_c + n_ba],
                         jnp.zeros((D_MODEL, LANES - n_ba), w_in.dtype)], axis=1).astype(BF16)
    pad = jnp.zeros((LANES - n_ba,), F32)
    nega = jnp.concatenate([jnp.zeros((2 * C_HEADS,), F32), -jnp.exp(a_log.astype(F32)).reshape(-1), pad])[None]
    dtb = jnp.concatenate([jnp.zeros((2 * C_HEADS,), F32), dt_bias.astype(F32).reshape(-1), pad])[None]
    return dict(w=w, nega=nega, dtb=dtb, cw=conv_w.astype(F32), cg=o_gain.astype(F32)[None],
                bias=_band_bias(rel_bias), wc=w_out[:C_WIDTH].astype(BF16), wd=w_out[C_WIDTH:].astype(BF16))


def _odd_layer(x, gain, p, ff, final, B, S):
    perm, unperm = _class_perms()
    (cq, ck, cv, z), dq, dk, dv, gb = _odd_proj(x, gain, p['w'], p['cw'], p['nega'], p['dtb'], perm, B, S)
    o_f, o_r = _gdn(cq, ck, cv, gb, B, S)
    outs, lses = [], []
    for gi, (_, dil) in enumerate(D_GROUPS):
        flat = lambda a: a.reshape(B * S, D_GW)
        o, lse = _band(flat(dq[gi]), flat(dk[gi]), flat(dv[gi]), p['bias'][gi], B * dil, S // dil)
        shape = (B * S, D_GW) if dil == 1 else (B, dil, S // dil, D_GW)
        outs.append(o.reshape(shape))
        lses.append(lse.reshape(shape))
    return _odd_out(x, o_f, o_r, z, p['cg'], outs, lses, unperm, p['wc'], p['wd'], ff, final, B, S)


def _trunk(x3, evens, odds, norm_mix, norm_ff, norm_final, w1, w2):
    B, S, width = x3.shape
    assert width == D_MODEL and S % (max(d for _, d in D_GROUPS) * BAND_QT) == 0 and S % PERM_TM == 0
    x = x3.reshape(B * S, D_MODEL)
    cos, sin = _rope_tables(S)
    for layer in range(DEPTH):
        gain = norm_mix[layer][None]
        ff = (norm_ff[layer][None], w1[layer], w2[layer], norm_final[None])
        final = layer == DEPTH - 1
        if layer % 2 == 0:
            x = _even_layer(x, gain, evens[layer // 2], ff, final, cos, sin, B, S)
        else:
            x = _odd_layer(x, gain, odds[layer // 2], ff, final, B, S)
    return x.reshape(B, S, D_MODEL)


def kernel(x_prompt, x_sample, rel_bias, norm_mix, norm_ff, norm_final, w_ff1, w_ff2, w_in_e, w_out_e,
           a_qnorm, a_knorm, b_conv_w, b_conv_b, b_wr, b_br, b_wi, b_bi, b_lambda, w_in_o, w_out_o,
           c_conv_w, c_a_log, c_dt_bias, c_norm):
    evens = [_prep_even(w_in_e[j], w_out_e[j], a_qnorm[j], a_knorm[j], b_conv_w[j], b_conv_b[j],
                        b_wr[j], b_br[j], b_wi[j], b_bi[j], b_lambda[j]) for j in range((DEPTH + 1) // 2)]
    odds = [_prep_odd(w_in_o[j], w_out_o[j], c_conv_w[j], c_a_log[j], c_dt_bias[j], c_norm[j], rel_bias)
            for j in range(DEPTH // 2)]
    w1 = w_ff1.astype(BF16)
    w2 = w_ff2.astype(BF16)
    nm = norm_mix.astype(F32)
    nf = norm_ff.astype(F32)
    ng = norm_final.astype(F32)
    y_prompt = _trunk(x_prompt, evens, odds, nm, nf, ng, w1, w2)
    y_sample = _trunk(x_sample, evens, odds, nm, nf, ng, w1, w2)
    return (y_prompt, y_sample)
```

```python
import functools
import math

import numpy as np
import jax
import jax.numpy as jnp
from jax import lax
from jax.experimental import pallas as pl
from jax.experimental.pallas import tpu as pltpu

F32 = jnp.float32
BF16 = jnp.bfloat16

D_MODEL = 1024
D_FF = 4 * D_MODEL
DEPTH = 4
EPS = 1e-6
NEG_INF = -1e30
GRID_W = 64
LANES = 128

A_HEADS = 8
A_KV_HEADS = 2
A_HEAD_DIM = 64
A_Q = A_HEADS * A_HEAD_DIM
A_KV = A_KV_HEADS * A_HEAD_DIM
ROPE_THETA = 10000.0
B_WIDTH = 512
B_BLOCKS = 8
RG_C = 8.0
C_HEADS = 4
C_HEAD_DIM = 128
C_WIDTH = C_HEADS * C_HEAD_DIM
C_CHUNK = 64
D_GROUPS = ((128, 1), (512, 4), (2048, 16))
D_HEADS_PER_GROUP = 4
D_HEAD_DIM = 64
D_NHEADS = len(D_GROUPS) * D_HEADS_PER_GROUP
D_WIDTH = D_NHEADS * D_HEAD_DIM
D_GW = D_HEADS_PER_GROUP * D_HEAD_DIM
D_STEPS = 64
N_BUCKETS = 32
MAX_DISTANCE = 1024
HALO = 8
Q_SCALE = A_HEAD_DIM ** -0.5 * math.log2(math.e)
MLP_TF = 1024
PROJ_ROWS = 256
PERM_TM = 512
BAND_QT = 128
BAND_TQ = 1024
GQA_TILE = 256
GQA_RING = 8
VT_PAD = 16
VT_ROWS = A_KV_HEADS * (A_HEAD_DIM + VT_PAD)

VMEM_LIMIT = 56 * 1024 * 1024


def _cparams(*sem):
    return pltpu.CompilerParams(dimension_semantics=sem, vmem_limit_bytes=VMEM_LIMIT)


def _rms(x, gain):
    return x * lax.rsqrt(jnp.mean(x * x, axis=-1, keepdims=True) + EPS) * gain


def _softplus(x):
    return jnp.maximum(x, 0.0) + jnp.log1p(jnp.exp(-jnp.abs(x)))


def _sigmoid(x):
    return 0.5 * jnp.tanh(0.5 * x) + 0.5


def _silu(x):
    return x * _sigmoid(x)


def _gelu_tanh(x):
    return 0.5 * x * (1.0 + jnp.tanh(math.sqrt(2.0 / math.pi) * (x + 0.044715 * (x * x * x))))


def _conv4(prev, cur, nxt, w, first, last):
    rows = cur.shape[0]
    prev = jnp.where(first, 0.0, prev)
    nxt = jnp.where(last, 0.0, nxt)
    full = jnp.concatenate([prev, cur, nxt], axis=0)
    y = full[HALO - 2:HALO - 2 + rows] * w[0:1]
    for j in range(1, 4):
        y = y + full[HALO - 2 + j:HALO - 2 + j + rows] * w[j:j + 1]
    return y


def _even_proj_kernel(xp_ref, x_ref, xn_ref, g_ref, w_ref, wvt_ref, wxr_ref, cw_ref, cb_ref, qg_ref, kg_ref,
                      cos_ref, sin_ref, q_ref, k_ref, vt_ref, xc_ref, gr_ref, *, n_s):
    tm = x_ref.shape[0]
    i = pl.program_id(0)
    gain = g_ref[...]
    parts = []
    normed = []
    for r0 in range(0, tm, PROJ_ROWS):
        xn = _rms(x_ref[r0:r0 + PROJ_ROWS, :], gain).astype(BF16)
        proj = jnp.dot(xn, w_ref[...], preferred_element_type=F32)
        vt = lax.dot_general(wvt_ref[...], xn, (((1,), (1,)), ((), ())), preferred_element_type=F32)
        parts.append((r0, proj, vt))
        normed.append(xn)
    ext = jnp.concatenate([_rms(xp_ref[...], gain).astype(BF16)] + normed
                          + [_rms(xn_ref[...], gain).astype(BF16)], axis=0)
    pxr = jnp.dot(ext, wxr_ref[...], preferred_element_type=F32)
    xc_ref[...] = _conv4(pxr[:HALO], pxr[HALO:HALO + tm], pxr[HALO + tm:], cw_ref[...],
                         i % n_s == 0, i % n_s == n_s - 1) + cb_ref[...]
    lane = lax.broadcasted_iota(jnp.int32, (PROJ_ROWS, LANES), 1)
    first_half = (lane % 32) < 16
    lo = lane < A_HEAD_DIM
    ones = jnp.ones((VT_PAD, LANES), BF16)
    for r0, proj, vt in parts:
        rows = slice(r0, r0 + PROJ_ROWS)
        cos = cos_ref[rows, :]
        sin = sin_ref[rows, :]

        def rope(t):
            swapped = jnp.where(first_half, pltpu.roll(t, LANES - 16, 1), pltpu.roll(t, 16, 1))
            return t * cos + swapped * sin

        vt = vt.astype(BF16)
        for c in range(PROJ_ROWS // LANES):
            cols = slice(c * LANES, (c + 1) * LANES)
            vt_ref[r0 // LANES + c] = jnp.concatenate(
                [vt[:A_HEAD_DIM, cols], ones, vt[A_HEAD_DIM:, cols], ones], axis=0)
        for h in range(A_HEADS):
            t = proj[:, h * LANES:(h + 1) * LANES]
            r = lax.rsqrt(jnp.sum(t * t, axis=-1, keepdims=True) * (1.0 / A_HEAD_DIM) + EPS)
            t = rope(t * r * qg_ref[:, h * LANES:(h + 1) * LANES])
            q_ref[rows, h * LANES:(h + 1) * LANES] = (t * Q_SCALE).astype(BF16)
        off = A_HEADS * LANES
        t = proj[:, off:off + LANES]
        t2 = t * t
        s_lo = jnp.sum(jnp.where(lo, t2, 0.0), axis=-1, keepdims=True)
        s_hi = jnp.sum(jnp.where(lo, 0.0, t2), axis=-1, keepdims=True)
        r = lax.rsqrt(jnp.where(lo, s_lo, s_hi) * (1.0 / A_HEAD_DIM) + EPS)
        k_ref[rows, :] = rope(t * r * kg_ref[...]).astype(BF16)
        off += LANES
        gr_ref[rows, :] = proj[:, off:off + B_WIDTH]


def _even_proj(x, gain, w, wvt, wxr, cw, cb, qg, kg, cos, sin, S, tm=512):
    M = x.shape[0]
    nS = S // tm
    n_in = w.shape[1]
    row = lambda i: (i, 0)
    const = lambda i: (0, 0)
    hb = tm // HALO
    last_hb = M // HALO - 1
    return pl.pallas_call(
        functools.partial(_even_proj_kernel, n_s=nS),
        grid=(M // tm,),
        in_specs=[pl.BlockSpec((HALO, D_MODEL), lambda i: (jnp.maximum(i * hb - 1, 0), 0)),
                  pl.BlockSpec((tm, D_MODEL), row),
                  pl.BlockSpec((HALO, D_MODEL), lambda i: (jnp.minimum((i + 1) * hb, last_hb), 0)),
                  pl.BlockSpec((1, D_MODEL), const),
                  pl.BlockSpec((D_MODEL, n_in), const), pl.BlockSpec((A_KV, D_MODEL), const),
                  pl.BlockSpec((D_MODEL, B_WIDTH), const), pl.BlockSpec((4, B_WIDTH), const),
                  pl.BlockSpec((1, B_WIDTH), const),
                  pl.BlockSpec((1, A_HEADS * LANES), const), pl.BlockSpec((1, LANES), const),
                  pl.BlockSpec((tm, LANES), lambda i: (i % nS, 0)),
                  pl.BlockSpec((tm, LANES), lambda i: (i % nS, 0))],
        out_specs=[pl.BlockSpec((tm, A_HEADS * LANES), row), pl.BlockSpec((tm, LANES), row),
                   pl.BlockSpec((tm // LANES, VT_ROWS, LANES), lambda i: (i, 0, 0)),
                   pl.BlockSpec((tm, B_WIDTH), row), pl.BlockSpec((tm, B_WIDTH), row)],
        out_shape=[jax.ShapeDtypeStruct((M, A_HEADS * LANES), BF16),
                   jax.ShapeDtypeStruct((M, LANES), BF16),
                   jax.ShapeDtypeStruct((M // LANES, VT_ROWS, LANES), BF16),
                   jax.ShapeDtypeStruct((M, B_WIDTH), F32), jax.ShapeDtypeStruct((M, B_WIDTH), F32)],
        compiler_params=_cparams("parallel"),
        name="even_proj",
    )(x, x, x, gain, w, wvt, wxr, cw, cb, qg, kg, cos, sin)


def _gqa_kernel(q_ref, k_ref, vt_ref, o_ref, m_ref, acc_ref, st_ref, *, tq, kstep, nstep, unroll, lookahead):
    hpt = GQA_TILE // tq
    ntile = A_HEADS // hpt
    qt = [jnp.concatenate([q_ref[:, h * LANES:(h + 1) * LANES] for h in range(p * hpt, (p + 1) * hpt)], axis=0)
          for p in range(ntile)]
    m_ref[...] = jnp.full(m_ref.shape, NEG_INF, F32)
    acc_ref[...] = jnp.zeros(acc_ref.shape, F32)
    vrows = VT_ROWS // A_KV_HEADS

    per = kstep // LANES

    def scores(j, u, p):
        ks = k_ref[pl.ds(pl.multiple_of((j * unroll + u) * kstep, kstep), kstep), :]
        return lax.dot_general(ks, qt[p], (((1,), (1,)), ((), ())), preferred_element_type=F32)

    def update(j, u, p, st):
        g = (p * hpt) // (A_HEADS // A_KV_HEADS)
        m_old = m_ref[p]
        m_new = jnp.maximum(m_old, jnp.max(st, axis=0, keepdims=True))
        alpha = jnp.exp2(m_old - m_new)
        pt = jnp.exp2(st - m_new).astype(BF16)
        first = (j * unroll + u) * per
        vts = jnp.concatenate([vt_ref[first + i, g * vrows:(g + 1) * vrows, :] for i in range(per)], axis=1)
        acc_ref[p] = alpha * acc_ref[p] + jnp.dot(vts, pt, preferred_element_type=F32)
        m_ref[p] = m_new

    pairs = [(u, p) for u in range(unroll) for p in range(ntile)]
    niter = nstep // unroll

    ring = st_ref.shape[0]

    def body(j, carry):
        j_next = jnp.minimum(j + 1, niter - 1)
        for idx in range(len(pairs)):
            ahead = idx + lookahead
            if ahead < len(pairs):
                st_ref[ahead % ring] = scores(j, *pairs[ahead])
            else:
                st_ref[ahead % ring] = scores(j_next, *pairs[ahead - len(pairs)])
            update(j, *pairs[idx], st_ref[idx % ring])
        return carry

    for i in range(lookahead):
        st_ref[i] = scores(0, *pairs[i])
    lax.fori_loop(0, niter, body, 0)
    heads = []
    for p in range(ntile):
        acc = acc_ref[p]
        o = acc[:A_HEAD_DIM] / acc[A_HEAD_DIM:A_HEAD_DIM + 1]
        heads += [o[:, i * tq:(i + 1) * tq] for i in range(hpt)]
    for e in range(A_HEADS // 2):
        both = jnp.concatenate(heads[2 * e:2 * e + 2], axis=0)
        o_ref[:, e * LANES:(e + 1) * LANES] = jnp.transpose(both).astype(o_ref.dtype)


def _gqa(q, k, vt, B, S, kstep=256, lookahead=5):
    tq, unroll = (GQA_TILE, 8) if S <= 2048 else (GQA_TILE // 2, 32)
    M = q.shape[0]
    nq = S // tq
    nslab = S // LANES
    nstep = S // kstep
    unroll = min(unroll, nstep)
    kern = functools.partial(_gqa_kernel, tq=tq, kstep=kstep, nstep=nstep, unroll=unroll, lookahead=lookahead)
    return pl.pallas_call(
        kern,
        grid=(B, nq),
        in_specs=[pl.BlockSpec((tq, A_HEADS * LANES), lambda b, i: (b * nq + i, 0)),
                  pl.BlockSpec((S, LANES), lambda b, i: (b, 0)),
                  pl.BlockSpec((nslab, VT_ROWS, LANES), lambda b, i: (b, 0, 0))],
        out_specs=pl.BlockSpec((tq, A_Q), lambda b, i: (b * nq + i, 0)),
        out_shape=jax.ShapeDtypeStruct((M, A_Q), BF16),
        scratch_shapes=[pltpu.VMEM((A_HEADS * tq // GQA_TILE, 1, GQA_TILE), F32),
                        pltpu.VMEM((A_HEADS * tq // GQA_TILE, VT_ROWS // A_KV_HEADS, GQA_TILE), F32),
                        pltpu.VMEM((GQA_RING, kstep, GQA_TILE), F32)],
        compiler_params=_cparams("parallel", "parallel"),
        name="gqa",
    )(q, k, vt)


def _rglru_kernel(xf_ref, xr_ref, wg_ref, bg_ref, lam_ref, hf_ref, hr_ref, a_s, u_s, carry_s, *, tm):
    t = pl.program_id(1)

    @pl.when(t == 0)
    def _():
        carry_s[...] = jnp.zeros_like(carry_s)

    ngrp = tm // 8
    row8 = lax.broadcasted_iota(jnp.int32, (ngrp, 8, B_WIDTH), 1)
    for d, c_ref in enumerate((xf_ref, xr_ref)):
        xc = c_ref[...]
        xb = xc.astype(BF16)
        r = _sigmoid(jnp.dot(xb, wg_ref[2 * d], preferred_element_type=F32) + bg_ref[2 * d:2 * d + 1])
        i = _sigmoid(jnp.dot(xb, wg_ref[2 * d + 1], preferred_element_type=F32) + bg_ref[2 * d + 1:2 * d + 2])
        log_a = (-RG_C) * r * _softplus(-lam_ref[d:d + 1])
        a = jnp.exp(log_a).reshape(ngrp, 8, B_WIDTH)
        u = (jnp.sqrt(1.0 - jnp.exp(2.0 * log_a)) * (i * xc)).reshape(ngrp, 8, B_WIDTH)
        for sh in (1, 2, 4):
            keep = (row8 >= sh) if d == 0 else (row8 < 8 - sh)
            shift = sh if d == 0 else 8 - sh
            u = u + a * jnp.where(keep, pltpu.roll(u, shift, 1), 0.0)
            a = a * jnp.where(keep, pltpu.roll(a, shift, 1), 1.0)
        a_s[d] = a.reshape(tm, B_WIDTH)
        u_s[d] = u.reshape(tm, B_WIDTH)

    def body(gidx, carry):
        out = []
        for d, out_ref in enumerate((hf_ref, hr_ref)):
            gg = gidx if d == 0 else ngrp - 1 - gidx
            start = pl.multiple_of(gg * 8, 8)
            hg = u_s[d, pl.ds(start, 8), :] + a_s[d, pl.ds(start, 8), :] * carry[d]
            out_ref[pl.ds(start, 8), :] = hg
            edge = hg[7:8, :] if d == 0 else hg[0:1, :]
            out.append(jnp.broadcast_to(edge, (8, B_WIDTH)))
        return tuple(out)

    ends = lax.fori_loop(0, ngrp, body, (carry_s[0], carry_s[1]), unroll=4)
    carry_s[0] = ends[0]
    carry_s[1] = ends[1]


def _rglru(xc, wg, bg, lam, B, S, tm=512):
    M = xc.shape[0]
    nblk = S // tm
    fwd = lambda b, t: b * nblk + t
    rev = lambda b, t: b * nblk + (nblk - 1 - t)
    const2 = lambda b, t: (0, 0)
    const3 = lambda b, t: (0, 0, 0)
    return pl.pallas_call(
        functools.partial(_rglru_kernel, tm=tm),
        grid=(B, nblk),
        in_specs=[pl.BlockSpec((tm, B_WIDTH), lambda b, t: (fwd(b, t), 0)),
                  pl.BlockSpec((tm, B_WIDTH), lambda b, t: (rev(b, t), 0)),
                  pl.BlockSpec((4, B_WIDTH, B_WIDTH), const3), pl.BlockSpec((4, B_WIDTH), const2),
                  pl.BlockSpec((2, B_WIDTH), const2)],
        out_specs=[pl.BlockSpec((tm, B_WIDTH), lambda b, t: (fwd(b, t), 0)),
                   pl.BlockSpec((tm, B_WIDTH), lambda b, t: (rev(b, t), 0))],
        out_shape=[jax.ShapeDtypeStruct((M, B_WIDTH), F32), jax.ShapeDtypeStruct((M, B_WIDTH), F32)],
        scratch_shapes=[pltpu.VMEM((2, tm, B_WIDTH), F32), pltpu.VMEM((2, tm, B_WIDTH), F32),
                        pltpu.VMEM((2, 8, B_WIDTH), F32)],
        compiler_params=_cparams("arbitrary", "arbitrary"),
        name="rglru",
    )(xc, xc, wg, bg, lam)


def _mlp_tail(x, ff, final):
    gff_ref, w1_ref, w2_ref, gfin_ref = ff
    xn = _rms(x, gff_ref[...]).astype(BF16)
    acc = x
    for c in range(D_FF // MLP_TF):
        h = jnp.dot(xn, w1_ref[:, c * MLP_TF:(c + 1) * MLP_TF], preferred_element_type=F32)
        h = jnp.square(jnp.maximum(h, 0.0)).astype(BF16)
        acc = acc + jnp.dot(h, w2_ref[c * MLP_TF:(c + 1) * MLP_TF, :], preferred_element_type=F32)
    return _rms(acc, gfin_ref[...]) if final else acc


def _resident(shape):
    return pl.BlockSpec(shape, lambda *_: (0,) * len(shape), pipeline_mode=pl.Buffered(1))


def _ff_specs():
    return [_resident((1, D_MODEL)), _resident((D_MODEL, D_FF)), _resident((D_FF, D_MODEL)),
            _resident((1, D_MODEL))]


def _even_out_kernel(x_ref, ya_ref, hf_ref, hr_ref, gr_ref, wa_ref, wb_ref, *rest, final):
    *ff, o_ref = rest
    yb = ((hf_ref[...] + hr_ref[...]) * _gelu_tanh(gr_ref[...])).astype(BF16)
    y = jnp.dot(ya_ref[...], wa_ref[...], preferred_element_type=F32)
    y = y + jnp.dot(yb, wb_ref[...], preferred_element_type=F32)
    o_ref[...] = _mlp_tail(x_ref[...] + y, ff, final)


def _even_out(x, ya, hf, hr, gr, wa, wb, ff, final, tm=512):
    M = x.shape[0]
    row = lambda i: (i, 0)
    return pl.pallas_call(
        functools.partial(_even_out_kernel, final=final),
        grid=(M // tm,),
        in_specs=[pl.BlockSpec((tm, D_MODEL), row), pl.BlockSpec((tm, A_Q), row),
                  pl.BlockSpec((tm, B_WIDTH), row), pl.BlockSpec((tm, B_WIDTH), row),
                  pl.BlockSpec((tm, B_WIDTH), row),
                  _resident((A_Q, D_MODEL)), _resident((B_WIDTH, D_MODEL))] + _ff_specs(),
        out_specs=pl.BlockSpec((tm, D_MODEL), row),
        out_shape=jax.ShapeDtypeStruct((M, D_MODEL), F32),
        compiler_params=_cparams("parallel"),
        name="even_out_mlp",
    )(x, ya, hf, hr, gr, wa, wb, *ff)


def _rope_tables(S):
    rows = S // GRID_W
    row = jnp.repeat(jnp.arange(rows, dtype=F32), GRID_W)
    col = jnp.tile(jnp.arange(GRID_W, dtype=F32), rows)
    n_freq = A_HEAD_DIM // 4
    inv = ROPE_THETA ** (-jnp.arange(n_freq, dtype=F32) / n_freq)
    ang_r = row[:, None] * inv
    ang_c = col[:, None] * inv
    cos = jnp.concatenate([jnp.cos(ang_r)] * 2 + [jnp.cos(ang_c)] * 2, axis=1)
    sin = jnp.concatenate([-jnp.sin(ang_r), jnp.sin(ang_r), -jnp.sin(ang_c), jnp.sin(ang_c)], axis=1)
    return jnp.tile(cos, (1, 2)), jnp.tile(sin, (1, 2))


def _prep_even(w_in, w_out, qn, kn, conv_w, conv_b, wr, br, wi, bi, lam):
    G = A_HEADS // A_KV_HEADS
    wq = w_in[:, :A_Q].reshape(D_MODEL, A_HEADS, A_HEAD_DIM)
    zeros = jnp.zeros_like(wq)
    half = (jnp.arange(A_HEADS) // G)[None, :, None]
    wq_pad = jnp.concatenate([jnp.where(half == 0, wq, zeros), jnp.where(half == 1, wq, zeros)], axis=-1)
    off = A_Q + 2 * A_KV
    w = jnp.concatenate([wq_pad.reshape(D_MODEL, A_HEADS * LANES), w_in[:, A_Q:A_Q + A_KV],
                         w_in[:, off + B_WIDTH:]], axis=1).astype(BF16)
    wxr = w_in[:, off:off + B_WIDTH].astype(BF16)
    wvt = w_in[:, A_Q + A_KV:A_Q + 2 * A_KV].T.astype(BF16)
    qg =jnp.tile(qn.astype(F32), 2 * A_HEADS)[None]
    kg = jnp.tile(kn.astype(F32), 2)[None]

    def dense(blocks):
        eye = jnp.eye(B_BLOCKS, dtype=blocks.dtype)
        return jnp.einsum('nde,nm->ndme', blocks, eye).reshape(B_WIDTH, B_WIDTH)

    wg = jnp.stack([dense(wr[0]), dense(wi[0]), dense(wr[1]), dense(wi[1])]).astype(BF16)
    bg = jnp.stack([br[0], bi[0], br[1], bi[1]]).astype(F32)
    return dict(w=w, wvt=wvt, wxr=wxr, qg=qg, kg=kg, cw=conv_w.astype(F32), cb=conv_b.astype(F32)[None], wg=wg, bg=bg,
                lam=lam.astype(F32), wa=w_out[:A_Q].astype(BF16), wb=w_out[A_Q:].astype(BF16))


def _even_layer(x, gain, p, ff, final, cos, sin, B, S):
    q, k, vt, xc, gr = _even_proj(x, gain, p['w'], p['wvt'], p['wxr'], p['cw'], p['cb'], p['qg'], p['kg'],
                                  cos, sin, S)
    ya = _gqa(q, k, vt, B, S)
    hf, hr = _rglru(xc, p['wg'], p['bg'], p['lam'], B, S)
    return _even_out(x, ya, hf, hr, gr, p['wa'], p['wb'], ff, final)


def _odd_proj_kernel(xp_ref, x_ref, xn_ref, g_ref, w_ref, cw_ref, nega_ref, dtb_ref, perm_ref,
                     cq_ref, ck_ref, cv_ref, z_ref, *rest, n_s):
    *d_refs, gb_ref = rest
    tm = x_ref.shape[0]
    i = pl.program_id(0)
    gain = g_ref[...]
    xn = _rms(x_ref[...], gain).astype(BF16)
    ext = jnp.concatenate([_rms(xp_ref[...], gain).astype(BF16), xn, _rms(xn_ref[...], gain).astype(BF16)], axis=0)
    n_c = 3 * C_WIDTH
    pq = jnp.dot(ext, w_ref[:, :n_c], preferred_element_type=F32)
    proj = jnp.dot(xn, w_ref[:, n_c:], preferred_element_type=F32)
    y = _silu(_conv4(pq[:HALO], pq[HALO:HALO + tm], pq[HALO + tm:], cw_ref[...], i % n_s == 0, i % n_s == n_s - 1))
    for h in range(C_HEADS):
        for part, ref, scale in ((0, cq_ref, C_HEAD_DIM ** -0.5), (1, ck_ref, 1.0)):
            a = y[:, part * C_WIDTH + h * LANES:part * C_WIDTH + (h + 1) * LANES]
            a = a * lax.rsqrt(jnp.sum(a * a, axis=-1, keepdims=True) + EPS)
            ref[:, h * LANES:(h + 1) * LANES] = a * scale
    cv_ref[...] = y[:, 2 * C_WIDTH:]
    z_ref[...] = proj[:, :C_WIDTH]
    off = C_WIDTH
    for ti, scale in enumerate((D_HEAD_DIM ** -0.5, 1.0, 1.0)):
        for gi, (_, dil) in enumerate(D_GROUPS):
            xg = (proj[:, off + gi * D_GW:off + (gi + 1) * D_GW] * scale).astype(BF16)
            ref = d_refs[ti * len(D_GROUPS) + gi]
            if dil == 1:
                ref[...] = xg
            else:
                y = jnp.dot(perm_ref[gi - 1], xg, preferred_element_type=F32)
                ref[...] = y.reshape(dil, tm // dil, D_GW).astype(BF16)
        off += D_WIDTH
    t = proj[:, off:off + LANES]
    lane = lax.broadcasted_iota(jnp.int32, t.shape, 1)
    gate = jnp.where(lane < 2 * C_HEADS, _sigmoid(t), nega_ref[...] * _softplus(t + dtb_ref[...]))
    gb_ref[...] = jnp.transpose(gate)[:4 * C_HEADS]


def _class_major_spec(dil, S, tm):
    n_s = S // tm
    return pl.BlockSpec((None, dil, tm // dil, D_GW), lambda i: (i // n_s, 0, i % n_s, 0))


def _odd_proj(x, gain, w, cw, nega, dtb, perm, B, S):
    tm = PERM_TM
    M = x.shape[0]
    row = lambda i: (i, 0)
    const = lambda i: (0, 0)
    hb = tm // HALO
    last_hb = M // HALO - 1
    d_specs, d_shapes = [], []
    for _ in range(3):
        for _, dil in D_GROUPS:
            if dil == 1:
                d_specs.append(pl.BlockSpec((tm, D_GW), row))
                d_shapes.append(jax.ShapeDtypeStruct((M, D_GW), BF16))
            else:
                d_specs.append(_class_major_spec(dil, S, tm))
                d_shapes.append(jax.ShapeDtypeStruct((B, dil, S // dil, D_GW), BF16))
    wide = pl.BlockSpec((tm, C_WIDTH), row)
    outs = pl.pallas_call(
        functools.partial(_odd_proj_kernel, n_s=S // tm),
        grid=(M // tm,),
        in_specs=[pl.BlockSpec((HALO, D_MODEL), lambda i: (jnp.maximum(i * hb - 1, 0), 0)),
                  pl.BlockSpec((tm, D_MODEL), row),
                  pl.BlockSpec((HALO, D_MODEL), lambda i: (jnp.minimum((i + 1) * hb, last_hb), 0)),
                  pl.BlockSpec((1, D_MODEL), const), pl.BlockSpec((D_MODEL, w.shape[1]), const),
                  pl.BlockSpec((4, 3 * C_WIDTH), const),
                  pl.BlockSpec((1, LANES), const), pl.BlockSpec((1, LANES), const),
                  pl.BlockSpec(perm.shape, lambda i: (0, 0, 0))],
        out_specs=[wide, wide, wide, wide] + d_specs + [pl.BlockSpec((4 * C_HEADS, tm), lambda i: (0, i))],
        out_shape=[jax.ShapeDtypeStruct((M, C_WIDTH), F32)] * 4 + d_shapes
        + [jax.ShapeDtypeStruct((4 * C_HEADS, M), F32)],
        compiler_params=_cparams("parallel"),
        name="odd_proj",
    )(x, x, x, gain, w, cw, nega, dtb, perm)
    ng = len(D_GROUPS)
    return outs[:4], outs[4:4 + ng], outs[4 + ng:4 + 2 * ng], outs[4 + 2 * ng:4 + 3 * ng], outs[-1]


def _nt_dot(a, b):
    return lax.dot_general(a, b, (((1,), (1,)), ((), ())), preferred_element_type=F32)


def _gdn_kernel(qf_ref, kf_ref, vf_ref, gf_ref, qr_ref, kr_ref, vr_ref, gr_ref, of_ref, or_ref, state, *, tm):
    C = C_CHUNK
    SB = LANES
    nsb = tm // SB
    nc = tm // C
    t = pl.program_id(1)

    @pl.when(t == 0)
    def _():
        state[...] = jnp.zeros_like(state)

    ri = lax.broadcasted_iota(jnp.int32, (SB, SB), 0)
    ci = lax.broadcasted_iota(jnp.int32, (SB, SB), 1)
    same = (ri // C) == (ci // C)
    eye = jnp.where(ri == ci, 1.0, 0.0)
    incl, strict, levels = [], [], []
    for d in range(2):
        hi_i, lo_i = (ci, ri) if d else (ri, ci)
        incl.append(same & (hi_i >= lo_i))
        strict.append(same & (hi_i > lo_i))
        lv = []
        s = 1
        while s < C:
            lv.append((((hi_i // s) % 2) == 1) & ((lo_i // s) == (hi_i // s) - 1))
            s *= 2
        levels.append(lv)

    pos = lax.broadcasted_iota(jnp.int32, (4 * C_HEADS, tm), 1) % C
    rows, cols = [], []
    for d, g_ref in enumerate((gf_ref, gr_ref)):
        gt = g_ref[...]
        x = gt
        s = 1
        while s < C:
            if d == 0:
                x = x + jnp.where(pos >= s, pltpu.roll(x, s, 1), 0.0)
            else:
                x = x + jnp.where(pos < C - s, pltpu.roll(x, tm - s, 1), 0.0)
            s *= 2
        lo = d * C_HEADS
        rw = jnp.concatenate([gt[lo:lo + C_HEADS], x[2 * C_HEADS + lo:2 * C_HEADS + lo + C_HEADS]], axis=0)
        rows.append(rw)
        cols.append(jnp.transpose(jnp.concatenate([rw, jnp.zeros((LANES - 2 * C_HEADS, tm), F32)], axis=0)))

    refs = ((qf_ref, kf_ref, vf_ref, of_ref), (qr_ref, kr_ref, vr_ref, or_ref))
    chains = []
    for d in range(2):
        for h in range(C_HEADS):
            for sb in range(nsb):
                rs = slice(sb * SB, (sb + 1) * SB)
                sl = slice(h * LANES, (h + 1) * LANES)
                q = refs[d][0][rs, sl]
                k = refs[d][1][rs, sl]
                v = refs[d][2][rs, sl]
                beta = jnp.broadcast_to(cols[d][rs, h:h + 1], (SB, LANES))
                gcc = jnp.broadcast_to(cols[d][rs, C_HEADS + h:C_HEADS + h + 1], (SB, LANES))
                gcr = jnp.broadcast_to(rows[d][C_HEADS + h:C_HEADS + h + 1, rs], (SB, SB))
                dec = jnp.exp(jnp.where(incl[d], gcc - gcr, NEG_INF))
                eg = jnp.exp(gcc)
                kbf = k.astype(BF16)
                kb = k * beta
                lmat = jnp.where(strict[d], _nt_dot(kb.astype(BF16), kbf) * dec, 0.0)
                amat = jnp.where(incl[d], _nt_dot(q.astype(BF16), kbf) * dec, 0.0).astype(BF16)
                chains.append(dict(d=d, h=h, sb=sb, k=k, gcc=gcc, lmat=lmat, amat=amat,
                                   rhs=jnp.concatenate([v * beta, kb * eg], axis=1).astype(BF16),
                                   qe=(q * eg).astype(BF16)))

    for ch in chains:
        ch['x'] = eye - jnp.where(levels[ch['d']][0], ch['lmat'], 0.0)
    for li in range(1, len(levels[0])):
        for ch in chains:
            ch['xb'] = ch['x'].astype(BF16)
            cmat = jnp.where(levels[ch['d']][li], ch['lmat'], 0.0).astype(BF16)
            ch['cx'] = jnp.dot(cmat, ch['xb'], preferred_element_type=F32).astype(BF16)
        for ch in chains:
            ch['x'] = ch['x'] - jnp.dot(ch['xb'], ch['cx'], preferred_element_type=F32)
    for ch in chains:
        sol = jnp.dot(ch['x'].astype(BF16), ch['rhs'], preferred_element_type=F32)
        ch['u'] = sol[:, :LANES]
        ch['w'] = sol[:, LANES:].astype(BF16)
        ch['vn'] = [None] * (SB // C)
        ch['ost'] = [None] * (SB // C)

    by_key = {(ch['d'], ch['h'], ch['sb']): ch for ch in chains}
    st = {(d, h): state[d, h] for d in range(2) for h in range(C_HEADS)}
    for step in range(nc):
        work = []
        for (d, h), s_val in st.items():
            c = nc - 1 - step if d else step
            ch = by_key[(d, h, c // (SB // C))]
            lc = c % (SB // C)
            rs = slice(lc * C, (lc + 1) * C)
            stb = s_val.astype(BF16)
            vn = ch['u'][rs] - jnp.dot(ch['w'][rs], stb, preferred_element_type=F32)
            ch['ost'][lc] = jnp.dot(ch['qe'][rs], stb, preferred_element_type=F32)
            ch['vn'][lc] = vn
            last = lc * C if d else (lc + 1) * C - 1
            gl = ch['gcc'][last:last + 1, :]
            kd = (ch['k'][rs] * jnp.exp(gl - ch['gcc'][rs])).astype(BF16)
            work.append(((d, h), s_val * jnp.exp(gl), kd, vn.astype(BF16)))
        for key, decayed, kd, vnb in work:
            st[key] = decayed + lax.dot_general(kd, vnb, (((0,), (0,)), ((), ())), preferred_element_type=F32)
    for (d, h), s_val in st.items():
        state[d, h] = s_val
    for ch in chains:
        rs = slice(ch['sb'] * SB, (ch['sb'] + 1) * SB)
        sl = slice(ch['h'] * LANES, (ch['h'] + 1) * LANES)
        vn_all = jnp.concatenate(ch['vn'], axis=0).astype(BF16)
        refs[ch['d']][3][rs, sl] = (jnp.concatenate(ch['ost'], axis=0)
                                   + jnp.dot(ch['amat'], vn_all, preferred_element_type=F32))


def _gdn(q, k, v, gbt, B, S, tm=512):
    M = q.shape[0]
    nblk = S // tm
    fwd = lambda b, t: b * nblk + t
    rev = lambda b, t: b * nblk + (nblk - 1 - t)
    specs = []
    for blk in (fwd, rev):
        specs += [pl.BlockSpec((tm, C_WIDTH), lambda b, t, blk=blk: (blk(b, t), 0))] * 3
        specs += [pl.BlockSpec((4 * C_HEADS, tm), lambda b, t, blk=blk: (0, blk(b, t)))]
    return pl.pallas_call(
        functools.partial(_gdn_kernel, tm=tm),
        grid=(B, nblk),
        in_specs=specs,
        out_specs=[pl.BlockSpec((tm, C_WIDTH), lambda b, t: (fwd(b, t), 0)),
                   pl.BlockSpec((tm, C_WIDTH), lambda b, t: (rev(b, t), 0))],
        out_shape=[jax.ShapeDtypeStruct((M, C_WIDTH), F32)] * 2,
        scratch_shapes=[pltpu.VMEM((2, C_HEADS, C_HEAD_DIM, C_HEAD_DIM), F32)],
        compiler_params=_cparams("arbitrary", "arbitrary"),
        name="gdn",
    )(q, k, v, gbt, q, k, v, gbt)


def _band_kernel(q_ref, kp_ref, kc_ref, kn_ref, vp_ref, vc_ref, vn_ref, bias_ref, o_ref, lse_ref, *, tq, m_len):
    t = pl.program_id(1)
    QT = BAND_QT
    nkeys = QT + 2 * D_STEPS
    kwin = jnp.concatenate([kp_ref[...], kc_ref[...], kn_ref[...]], axis=0)
    vwin = jnp.concatenate([vp_ref[...], vc_ref[...], vn_ref[...]], axis=0)
    lane = lax.broadcasted_iota(jnp.int32, (QT, D_GW), 1)
    mine = [(lane // D_HEAD_DIM) == h for h in range(D_HEADS_PER_GROUP)]
    col = lax.broadcasted_iota(jnp.int32, (QT, nkeys), 1)
    tiles = []
    for i in range(tq // QT):
        q = q_ref[i * QT:(i + 1) * QT, :]
        kpos = t * tq + i * QT - D_STEPS + col
        valid = (kpos >= 0) & (kpos < m_len)
        kw = kwin[i * QT:i * QT + nkeys]
        for h in range(D_HEADS_PER_GROUP):
            qh = jnp.where(mine[h], q, jnp.zeros_like(q))
            s = lax.dot_general(qh, kw, (((1,), (1,)), ((), ())), preferred_element_type=F32) + bias_ref[h]
            tiles.append([i, h, jnp.where(valid, s, NEG_INF)])
    for tile in tiles:
        s = tile[2]
        m = jnp.max(s, axis=-1, keepdims=True)
        e = jnp.exp(s - m)
        l = jnp.sum(e, axis=-1, keepdims=True)
        tile[2] = (e * (1.0 / l)).astype(BF16)
        tile.append(m + jnp.log(l))
    for i in range(tq // QT):
        o = jnp.zeros((QT, D_GW), F32)
        lse_full = jnp.zeros((QT, D_GW), F32)
        vw = vwin[i * QT:i * QT + nkeys]
        for _, h, p, lse in tiles[i * D_HEADS_PER_GROUP:(i + 1) * D_HEADS_PER_GROUP]:
            o = jnp.where(mine[h], jnp.dot(p, vw, preferred_element_type=F32), o)
            lse_full = jnp.where(mine[h], lse, lse_full)
        o_ref[i * QT:(i + 1) * QT, :] = o
        lse_ref[i * QT:(i + 1) * QT, :] = lse_full


def _band(q, k, v, bias, Z, m_len):
    M = q.shape[0]
    tq = min(BAND_TQ, m_len)
    nq = m_len // tq
    r = tq // D_STEPS
    nhb = M // D_STEPS
    cur = lambda z, t: (z * nq + t, 0)
    prev = lambda z, t: (jnp.maximum((z * nq + t) * r - 1, 0), 0)
    nxt = lambda z, t: (jnp.minimum((z * nq + t + 1) * r, nhb - 1), 0)
    halo = lambda f: pl.BlockSpec((D_STEPS, D_GW), f)
    full = pl.BlockSpec((tq, D_GW), cur)
    return pl.pallas_call(
        functools.partial(_band_kernel, tq=tq, m_len=m_len),
        grid=(Z, nq),
        in_specs=[full, halo(prev), full, halo(nxt), halo(prev), full, halo(nxt),
                  pl.BlockSpec((D_HEADS_PER_GROUP, BAND_QT, BAND_QT + 2 * D_STEPS), lambda z, t: (0, 0, 0))],
        out_specs=[full, full],
        out_shape=[jax.ShapeDtypeStruct((M, D_GW), F32)] * 2,
        compiler_params=_cparams("parallel", "parallel"),
        name="band_attn",
    )(q, k, k, k, v, v, v, bias)


def _odd_out_kernel(x_ref, of_ref, or_ref, z_ref, cg_ref, o0_ref, o1_ref, o2_ref, l0_ref, l1_ref, l2_ref,
                    unperm_ref, wc_ref, wd_ref, *rest, final):
    *ff, out_ref = rest
    tm = x_ref.shape[0]
    oc = of_ref[...] + or_ref[...]
    z = z_ref[...]
    parts = []
    for h in range(C_HEADS):
        sl = slice(h * LANES, (h + 1) * LANES)
        parts.append(_rms(oc[:, sl], cg_ref[...]) * _silu(z[:, sl]))
    yc = jnp.concatenate(parts, axis=1).astype(BF16)

    def token_order(ref, gi):
        if D_GROUPS[gi][1] == 1:
            return ref[...]
        val = ref[...].reshape(tm, D_GW)
        hi = val.astype(BF16)
        lo = (val - hi.astype(F32)).astype(BF16)
        pt = unperm_ref[gi - 1]
        return jnp.dot(pt, hi, preferred_element_type=F32) + jnp.dot(pt, lo, preferred_element_type=F32)

    o0, o1, o2 = (token_order(r, gi) for gi, r in enumerate((o0_ref, o1_ref, o2_ref)))
    l0, l1, l2 = (token_order(r, gi) for gi, r in enumerate((l0_ref, l1_ref, l2_ref)))
    m = jnp.maximum(jnp.maximum(l0, l1), l2)
    e0, e1, e2 = jnp.exp(l0 - m), jnp.exp(l1 - m), jnp.exp(l2 - m)
    inv = 1.0 / (e0 + e1 + e2)
    yd = ((e0 * o0 + e1 * o1 + e2 * o2) * inv).astype(BF16)
    y = jnp.dot(yc, wc_ref[...], preferred_element_type=F32) + jnp.dot(yd, wd_ref[...], preferred_element_type=F32)
    out_ref[...] = _mlp_tail(x_ref[...] + y, ff, final)


def _odd_out(x, o_f, o_r, z, cg, os_, ls_, unperm, wc, wd, ff, final, B, S):
    tm = PERM_TM
    M = x.shape[0]
    row = lambda i: (i, 0)
    wide = pl.BlockSpec((tm, C_WIDTH), row)
    grp = [pl.BlockSpec((tm, D_GW), row) if dil == 1 else _class_major_spec(dil, S, tm) for _, dil in D_GROUPS]
    return pl.pallas_call(
        functools.partial(_odd_out_kernel, final=final),
        grid=(M // tm,),
        in_specs=[pl.BlockSpec((tm, D_MODEL), row), wide, wide, wide, _resident((1, LANES))] + grp + grp
        + [_resident(unperm.shape), _resident((C_WIDTH, D_MODEL)), _resident((D_GW, D_MODEL))] + _ff_specs(),
        out_specs=pl.BlockSpec((tm, D_MODEL), row),
        out_shape=jax.ShapeDtypeStruct((M, D_MODEL), F32),
        compiler_params=_cparams("parallel"),
        name="odd_out_mlp",
    )(x, o_f, o_r, z, cg, *os_, *ls_, unperm, wc, wd, *ff)


def _class_perms():
    mats = []
    for _, dil in D_GROUPS[1:]:
        out_row = np.arange(PERM_TM)
        src = (out_row % (PERM_TM // dil)) * dil + out_row // (PERM_TM // dil)
        mats.append(np.eye(PERM_TM, dtype=np.float32)[src])
    perm = np.stack(mats)
    return jnp.asarray(perm, BF16), jnp.asarray(perm.transpose(0, 2, 1), BF16)


def _t5_bucket(rel):
    nb = N_BUCKETS // 2
    max_exact = nb // 2
    n = np.abs(rel)
    large = max_exact + (np.log(np.maximum(n, 1) / max_exact) / math.log(MAX_DISTANCE / max_exact)
                         * (nb - max_exact)).astype(np.int64)
    large = np.minimum(large, nb - 1)
    return (np.where(rel > 0, nb, 0) + np.where(n < max_exact, n, large)).astype(np.int32)


def _band_bias(rel_bias):
    nkeys = BAND_QT + 2 * D_STEPS
    tiles = []
    for gi, (window, dil) in enumerate(D_GROUPS):
        steps = window // (2 * dil)
        assert steps == D_STEPS
        buckets = _t5_bucket(np.arange(-steps, steps + 1) * dil)
        hsl = slice(gi * D_HEADS_PER_GROUP, (gi + 1) * D_HEADS_PER_GROUP)
        b = rel_bias.astype(F32)[jnp.asarray(buckets)][:, hsl].T
        period = jnp.concatenate([b, jnp.full((D_HEADS_PER_GROUP, nkeys + 1 - b.shape[1]), NEG_INF, F32)], axis=1)
        flat = jnp.tile(period, (1, BAND_QT))[:, :BAND_QT * nkeys]
        tiles.append(flat.reshape(D_HEADS_PER_GROUP, BAND_QT, nkeys))
    return jnp.stack(tiles)


def _prep_odd(w_in, w_out, conv_w, a_log, dt_bias, o_gain, rel_bias):
    n_c = 4 * C_WIDTH
    n_ba = 4 * C_HEADS
    w = jnp.concatenate([w_in[:, :n_c], w_in[:, n_c + n_ba:], w_in[:, n_c:n_c + n_ba],
                         jnp.zeros((D_MODEL, LANES - n_ba), w_in.dtype)], axis=1).astype(BF16)
    pad = jnp.zeros((LANES - n_ba,), F32)
    nega = jnp.concatenate([jnp.zeros((2 * C_HEADS,), F32), -jnp.exp(a_log.astype(F32)).reshape(-1), pad])[None]
    dtb = jnp.concatenate([jnp.zeros((2 * C_HEADS,), F32), dt_bias.astype(F32).reshape(-1), pad])[None]
    return dict(w=w, nega=nega, dtb=dtb, cw=conv_w.astype(F32), cg=o_gain.astype(F32)[None],
                bias=_band_bias(rel_bias), wc=w_out[:C_WIDTH].astype(BF16), wd=w_out[C_WIDTH:].astype(BF16))


def _odd_layer(x, gain, p, ff, final, B, S):
    perm, unperm = _class_perms()
    (cq, ck, cv, z), dq, dk, dv, gb = _odd_proj(x, gain, p['w'], p['cw'], p['nega'], p['dtb'], perm, B, S)
    o_f, o_r = _gdn(cq, ck, cv, gb, B, S)
    outs, lses = [], []
    for gi, (_, dil) in enumerate(D_GROUPS):
        flat = lambda a: a.reshape(B * S, D_GW)
        o, lse = _band(flat(dq[gi]), flat(dk[gi]), flat(dv[gi]), p['bias'][gi], B * dil, S // dil)
        shape = (B * S, D_GW) if dil == 1 else (B, dil, S // dil, D_GW)
        outs.append(o.reshape(shape))
        lses.append(lse.reshape(shape))
    return _odd_out(x, o_f, o_r, z, p['cg'], outs, lses, unperm, p['wc'], p['wd'], ff, final, B, S)


def _trunk(x3, evens, odds, norm_mix, norm_ff, norm_final, w1, w2):
    B, S, width = x3.shape
    assert width == D_MODEL and S % (max(d for _, d in D_GROUPS) * BAND_QT) == 0 and S % PERM_TM == 0
    x = x3.reshape(B * S, D_MODEL)
    cos, sin = _rope_tables(S)
    for layer in range(DEPTH):
        gain = norm_mix[layer][None]
        ff = (norm_ff[layer][None], w1[layer], w2[layer], norm_final[None])
        final = layer == DEPTH - 1
        if layer % 2 == 0:
            x = _even_layer(x, gain, evens[layer // 2], ff, final, cos, sin, B, S)
        else:
            x = _odd_layer(x, gain, odds[layer // 2], ff, final, B, S)
    return x.reshape(B, S, D_MODEL)


def kernel(x_prompt, x_sample, rel_bias, norm_mix, norm_ff, norm_final, w_ff1, w_ff2, w_in_e, w_out_e,
           a_qnorm, a_knorm, b_conv_w, b_conv_b, b_wr, b_br, b_wi, b_bi, b_lambda, w_in_o, w_out_o,
           c_conv_w, c_a_log, c_dt_bias, c_norm):
    evens = [_prep_even(w_in_e[j], w_out_e[j], a_qnorm[j], a_knorm[j], b_conv_w[j], b_conv_b[j],
                        b_wr[j], b_br[j], b_wi[j], b_bi[j], b_lambda[j]) for j in range((DEPTH + 1) // 2)]
    odds = [_prep_odd(w_in_o[j], w_out_o[j], c_conv_w[j], c_a_log[j], c_dt_bias[j], c_norm[j], rel_bias)
            for j in range(DEPTH // 2)]
    w1 = w_ff1.astype(BF16)
    w2 = w_ff2.astype(BF16)
    nm = norm_mix.astype(F32)
    nf = norm_ff.astype(F32)
    ng = norm_final.astype(F32)
    y_prompt = _trunk(x_prompt, evens, odds, nm, nf, ng, w1, w2)
    y_sample = _trunk(x_sample, evens, odds, nm, nf, ng, w1, w2)
    return (y_prompt, y_sample)
```
